```python
import math, functools
import jax, jax.numpy as jnp
from jax import lax
import numpy as np

D_MODEL = 1024
BATCH = 2
SEQ = 8192
DEPTH = 1
DEC_BATCH = 32
DEC_SEQ = 4
PAST_LEN = 16384
PAGE_SIZE = 128

MIX_W = D_MODEL
RW_HEAD = 64
RW_W = MIX_W // 2
RW_H = RW_W // RW_HEAD
DECAY_LORA = 64
AAA_LORA = 64
GATE_LORA = 128
RW_COLS = 3 * RW_W + DECAY_LORA + AAA_LORA + GATE_LORA
RW_EPS = 64e-5
DSA_HEAD = 64
DSA_W = MIX_W - RW_W
DSA_H = DSA_W // DSA_HEAD
IDX_H = 8
IDX_D = 64
TOPK_MAX = 256
Q_BLOCK = 128
DSA_COLS = 3 * DSA_W + IDX_H * IDX_D + IDX_D + IDX_H
IN_COLS = RW_COLS + DSA_COLS
NUM_BUCKETS = 32
MAX_DISTANCE = 128
MEM_LEN = 256
XA_H = 4
XA_HEAD = 128
XA_W = XA_H * XA_HEAD
N_EXPERTS = 32
TOP_K = 4
D_FF = D_MODEL
SWIGLU_LIMIT = 7.0
SWIGLU_ALPHA = 1.702

NORM_EPS = 1e-5
POOL_NUM = 5
POOL_DEN = 4
NEG_INF = -1e30

kernel_name = 'hymba_rwkv7_dsa_moe_step'


def rmsnorm(x, g):
    xf = x.astype(jnp.float32)
    y = xf * lax.rsqrt(jnp.mean(xf * xf, axis=-1, keepdims=True) + NORM_EPS)
    return (y * g.astype(jnp.float32)).astype(x.dtype)


def split_cols(p, sizes):
    return jnp.split(p, np.cumsum(sizes)[:-1].tolist(), axis=-1)


def gather_rows(a, idx):
    return jax.vmap(lambda ab, ib: ab[ib])(a, idx)


def t5_bucket(dist):
    max_exact = NUM_BUCKETS // 2
    d = jnp.maximum(dist, 0)
    large = max_exact + (jnp.log(jnp.maximum(d, 1).astype(jnp.float32) / max_exact)
                         / math.log(MAX_DISTANCE / max_exact)
                         * (NUM_BUCKETS - max_exact)).astype(jnp.int32)
    return jnp.where(d < max_exact, d, jnp.minimum(large, NUM_BUCKETS - 1))


def indexer_topk(qi, wi, ki, q_pos, n_keys, k_sel):
    s = jax.nn.relu(jnp.einsum('bthd,bsd->bths', qi.astype(jnp.float32), ki.astype(jnp.float32)))
    score = jnp.einsum('bths,bth->bts', s, wi.astype(jnp.float32)) * (IDX_H ** -0.5 * IDX_D ** -0.5)
    allowed = jnp.arange(n_keys, dtype=jnp.int32)[None, None, :] <= q_pos[None, :, None]
    score = jnp.where(allowed, score, NEG_INF)
    _, idx = lax.top_k(score, k_sel)
    return idx


def sparse_attend(q, k_sel, v_sel, idx, q_pos, rel_bias):
    logits = jnp.einsum('bthd,btkhd->bthk', q.astype(jnp.float32), k_sel.astype(jnp.float32)) * DSA_HEAD ** -0.5
    dist = q_pos[None, :, None] - idx
    bias = rel_bias.astype(jnp.float32)[t5_bucket(dist)]
    logits = logits + jnp.swapaxes(bias, -1, -2)
    logits = jnp.where((dist >= 0)[:, :, None, :], logits, NEG_INF)
    p = jax.nn.softmax(logits, axis=-1)
    return jnp.einsum('bthk,btkhd->bthd', p, v_sel.astype(jnp.float32)).astype(q.dtype)


def dsa_prompt(q, k, v, qi, ki, wi, rel_bias):
    b, t = q.shape[:2]
    n_blk = t // Q_BLOCK
    k_sel = min(TOPK_MAX, t // 4)

    def to_blocks(a):
        return jnp.moveaxis(a.reshape((b, n_blk, Q_BLOCK) + a.shape[2:]), 1, 0)

    pos = jnp.arange(t, dtype=jnp.int32).reshape(n_blk, Q_BLOCK)

    def block(args):
        qb, qib, wib, pb = args
        idx = indexer_topk(qib, wib, ki, pb, t, k_sel)
        return sparse_attend(qb, gather_rows(k, idx), gather_rows(v, idx), idx, pb, rel_bias)

    out = lax.map(block, (to_blocks(q), to_blocks(qi), to_blocks(wi), pos))
    return jnp.moveaxis(out, 0, 1).reshape(b, t, DSA_H, DSA_HEAD)


def dsa_sample(q, k, v, qi, ki, wi, rel_bias, cache_k, cache_v, cache_ki, page_table):
    b, t = q.shape[:2]
    past = page_table.shape[1] * PAGE_SIZE
    n_keys = past + t
    k_sel = min(TOPK_MAX, n_keys // 4)
    ki_past = cache_ki[page_table].reshape(b, past, IDX_D)
    ki_all = jnp.concatenate([ki_past, ki.astype(ki_past.dtype)], axis=1)
    q_pos = past + jnp.arange(t, dtype=jnp.int32)
    idx = indexer_topk(qi, wi, ki_all, q_pos, n_keys, k_sel)
    pidx = jnp.minimum(idx, past - 1)
    phys = gather_rows(page_table, pidx // PAGE_SIZE)
    off = pidx % PAGE_SIZE
    nidx = jnp.clip(idx - past, 0, t - 1)
    is_new = (idx >= past)[..., None, None]
    k_rows = jnp.where(is_new, gather_rows(k, nidx), cache_k[phys, off])
    v_rows = jnp.where(is_new, gather_rows(v, nidx), cache_v[phys, off])
    return sparse_attend(q, k_rows, v_rows, idx, q_pos, rel_bias)


def rwkv_scan(r, w, k, v, kk, a, s0):
    tm = lambda z: jnp.moveaxis(z.astype(jnp.float32), 1, 0)

    def step(s, inp):
        rt, wt, kt, vt, kkt, at = inp
        sa = jnp.einsum('bhvk,bhk->bhv', s, -kkt)
        s = s * wt[:, :, None, :] + sa[..., None] * (kkt * at)[:, :, None, :] + vt[..., None] * kt[:, :, None, :]
        return s, jnp.einsum('bhvk,bhk->bhv', s, rt)

    s1, y = lax.scan(step, s0.astype(jnp.float32), (tm(r), tm(w), tm(k), tm(v), tm(kk), tm(a)))
    return jnp.moveaxis(y, 0, 1), s1


def rwkv_mix(prw, shift_prev, s0, P):
    b, t, _ = prw.shape
    p_prev = jnp.concatenate([shift_prev[:, None, :].astype(prw.dtype), prw[:, :-1]], axis=1)
    ps = prw + (p_prev - prw) * P['mu_shift']
    r, k, v, wd, ad, gd = split_cols(ps, [RW_W, RW_W, RW_W, DECAY_LORA, AAA_LORA, GATE_LORA])
    w_log = -jax.nn.softplus(-(P['rw_w0'] + jnp.tanh(wd) @ P['rw_w2'])) - 0.5
    decay = jnp.exp(-jnp.exp(w_log.astype(jnp.float32)))
    a = jax.nn.sigmoid(P['rw_a0'] + ad @ P['rw_a2'])
    g = jax.nn.sigmoid(gd) @ P['rw_g2']
    kk = k * P['rw_k_k']
    k = k * (1 + (a - 1) * P['rw_k_a'])
    heads = lambda z: z.reshape(b, t, RW_H, RW_HEAD).astype(jnp.float32)
    r, k, v, decay, a, kk = map(heads, (r, k, v, decay, a, kk))
    kk = kk / jnp.maximum(jnp.sqrt(jnp.sum(kk * kk, axis=-1, keepdims=True)), 1e-12)
    y, s1 = rwkv_scan(r, decay, k, v, kk, a, s0)
    mu = jnp.mean(y, axis=-1, keepdims=True)
    var = jnp.mean(jnp.square(y - mu), axis=-1, keepdims=True)
    yn = (y - mu) * lax.rsqrt(var + RW_EPS)
    yn = yn * P['rw_ln_w'].reshape(RW_H, RW_HEAD).astype(jnp.float32) + P['rw_ln_b'].reshape(RW_H, RW_HEAD).astype(jnp.float32)
    bonus = jnp.sum(r * k * P['rw_r_k'].astype(jnp.float32), axis=-1, keepdims=True) * v
    out = (yn + bonus).reshape(b, t, RW_W) * g.astype(jnp.float32)
    return out.astype(prw.dtype), s1, prw[:, -1]


def mem_kv(mem, g, wk, wv):
    b, m, _ = mem.shape
    h = rmsnorm(mem, g)
    return (h @ wk).reshape(b, m, XA_H, XA_HEAD), (h @ wv).reshape(b, m, XA_H, XA_HEAD)


def cross_attend(h, mk, mv, wq, wo):
    b, t, _ = h.shape
    q = (h @ wq).reshape(b, t, XA_H, XA_HEAD)
    logits = jnp.einsum('bthd,bmhd->bhtm', q.astype(jnp.float32), mk.astype(jnp.float32)) * XA_HEAD ** -0.5
    p = jax.nn.softmax(logits, axis=-1)
    o = jnp.einsum('bhtm,bmhd->bthd', p, mv.astype(jnp.float32)).astype(h.dtype)
    return o.reshape(b, t, XA_W) @ wo


def moe(h, P):
    shp = h.shape
    xf = h.reshape(-1, D_MODEL)
    logits = (xf @ P['router_w'] + P['router_b']).astype(jnp.float32)
    top_v, top_i = lax.top_k(logits, TOP_K)
    gates = jax.nn.softmax(top_v, axis=-1)
    comb = jnp.sum(jax.nn.one_hot(top_i, N_EXPERTS, dtype=jnp.float32) * gates[..., None], axis=1)
    y = jnp.zeros(xf.shape, jnp.float32)
    for e in range(N_EXPERTS):
        hh = xf @ P['moe_w1'][e] + P['moe_b1'][e]
        gate = jnp.minimum(hh[:, :D_FF], SWIGLU_LIMIT)
        up = jnp.clip(hh[:, D_FF:], -SWIGLU_LIMIT, SWIGLU_LIMIT)
        glu = gate * jax.nn.sigmoid(SWIGLU_ALPHA * gate)
        y = y + comb[:, e:e + 1] * (((up + 1) * glu) @ P['moe_w2'][e] + P['moe_b2'][e])
    return y.astype(h.dtype).reshape(shp)


def layer(x, mk, mv, s0, shift0, dsa_fn, P):
    b, t, _ = x.shape
    h = rmsnorm(x, P['norm_mix'])
    proj = h @ P['w_in']
    prw, pdsa = proj[..., :RW_COLS], proj[..., RW_COLS:]
    y_rw, s1, shift1 = rwkv_mix(prw, shift0, s0, P)
    q, k, v, qi, ki, wi = split_cols(pdsa, [DSA_W, DSA_W, DSA_W, IDX_H * IDX_D, IDX_D, IDX_H])
    q = q.reshape(b, t, DSA_H, DSA_HEAD)
    k = k.reshape(b, t, DSA_H, DSA_HEAD)
    v = v.reshape(b, t, DSA_H, DSA_HEAD)
    qi = qi.reshape(b, t, IDX_H, IDX_D)
    y_dsa = dsa_fn(q, k, v, qi, ki, wi)
    mix = jnp.concatenate([y_rw, y_dsa.reshape(b, t, DSA_W)], axis=-1)
    x = x + mix @ P['w_out']
    x = x + cross_attend(rmsnorm(x, P['norm_xattn']), mk, mv, P['xa_wq'], P['xa_wo'])
    x = x + moe(rmsnorm(x, P['norm_ffn']), P)
    return x, (s1, shift1, k, v, ki)


def setup_inputs(seed: int = 0) -> dict:
    key = jax.random.key(seed)
    ks = iter(jax.random.split(key, 64))

    def nrm(shape, scale):
        return jax.random.normal(next(ks), shape, jnp.float32) * scale

    n_pages = PAST_LEN // PAGE_SIZE
    n_pool = (DEC_BATCH * n_pages * POOL_NUM + POOL_DEN - 1) // POOL_DEN
    page_table = jax.random.permutation(next(ks), n_pool)[:DEC_BATCH * n_pages].reshape(DEC_BATCH, n_pages).astype(jnp.int32)
    L = DEPTH
    return {
        'x_prompt': nrm((BATCH, SEQ, D_MODEL), 1.0),
        'mem_prompt': nrm((BATCH, MEM_LEN, D_MODEL), 1.0),
        'x_sample': nrm((DEC_BATCH, DEC_SEQ, D_MODEL), 1.0),
        'cache_k': nrm((L, n_pool, PAGE_SIZE, DSA_H, DSA_HEAD), 1.0),
        'cache_v': nrm((L, n_pool, PAGE_SIZE, DSA_H, DSA_HEAD), 1.0),
        'cache_idx_k': nrm((L, n_pool, PAGE_SIZE, IDX_D), 1.0),
        'cache_mem_k': nrm((L, DEC_BATCH, MEM_LEN, XA_H, XA_HEAD), 1.0),
        'cache_mem_v': nrm((L, DEC_BATCH, MEM_LEN, XA_H, XA_HEAD), 1.0),
        'state_rwkv': nrm((L, DEC_BATCH, RW_H, RW_HEAD, RW_HEAD), 0.1),
        'state_shift': nrm((L, DEC_BATCH, RW_COLS), 1.0),
        'page_table': page_table,
        'rel_bias': nrm((NUM_BUCKETS, DSA_H), 0.5),
        'norm_final': 1.0 + nrm((D_MODEL,), 0.05),
        'norm_mix': 1.0 + nrm((L, D_MODEL), 0.05),
        'w_in': nrm((L, D_MODEL, IN_COLS), D_MODEL ** -0.5),
        'mu_shift': jax.random.uniform(next(ks), (L, RW_COLS), jnp.float32, 0.1, 0.9),
        'rw_w0': nrm((L, RW_W), 0.5),
        'rw_w2': nrm((L, DECAY_LORA, RW_W), 0.5 * DECAY_LORA ** -0.5),
        'rw_a0': nrm((L, RW_W), 0.5),
        'rw_a2': nrm((L, AAA_LORA, RW_W), 0.5 * AAA_LORA ** -0.5),
        'rw_g2': nrm((L, GATE_LORA, RW_W), GATE_LORA ** -0.5),
        'rw_k_k': 0.85 + nrm((L, RW_W), 0.05),
        'rw_k_a': 1.0 + nrm((L, RW_W), 0.05),
        'rw_r_k': nrm((L, RW_H, RW_HEAD), 0.1),
        'rw_ln_w': 1.0 + nrm((L, RW_W), 0.05),
        'rw_ln_b': nrm((L, RW_W), 0.01),
        'w_out': nrm((L, MIX_W, D_MODEL), MIX_W ** -0.5),
        'norm_xattn': 1.0 + nrm((L, D_MODEL), 0.05),
        'norm_mem': 1.0 + nrm((L, D_MODEL), 0.05),
        'xa_wq': nrm((L, D_MODEL, XA_W), D_MODEL ** -0.5),
        'xa_wk': nrm((L, D_MODEL, XA_W), D_MODEL ** -0.5),
        'xa_wv': nrm((L, D_MODEL, XA_W), D_MODEL ** -0.5),
        'xa_wo': nrm((L, XA_W, D_MODEL), XA_W ** -0.5),
        'norm_ffn': 1.0 + nrm((L, D_MODEL), 0.05),
        'router_w': nrm((L, D_MODEL, N_EXPERTS), D_MODEL ** -0.5),
        'router_b': nrm((L, N_EXPERTS), 0.01),
        'moe_w1': nrm((L, N_EXPERTS, D_MODEL, 2 * D_FF), D_MODEL ** -0.5),
        'moe_b1': nrm((L, N_EXPERTS, 2 * D_FF), 0.01),
        'moe_w2': nrm((L, N_EXPERTS, D_FF, D_MODEL), D_FF ** -0.5),
        'moe_b2': nrm((L, N_EXPERTS, D_MODEL), 0.01),
    }


def reference(x_prompt, mem_prompt, x_sample, cache_k, cache_v, cache_idx_k, cache_mem_k, cache_mem_v,
              state_rwkv, state_shift, page_table, rel_bias, norm_final, norm_mix, w_in, mu_shift,
              rw_w0, rw_w2, rw_a0, rw_a2, rw_g2, rw_k_k, rw_k_a, rw_r_k, rw_ln_w, rw_ln_b, w_out,
              norm_xattn, norm_mem, xa_wq, xa_wk, xa_wv, xa_wo, norm_ffn, router_w, router_b,
              moe_w1, moe_b1, moe_w2, moe_b2):
    b_p = x_prompt.shape[0]
    yp, ys = x_prompt, x_sample
    p_s, p_sh, p_k, p_v, p_ki, p_mk, p_mv = [], [], [], [], [], [], []
    s_s, s_sh, s_k, s_v, s_ki = [], [], [], [], []
    for l in range(DEPTH):
        P = {
            'norm_mix': norm_mix[l], 'w_in': w_in[l], 'mu_shift': mu_shift[l],
            'rw_w0': rw_w0[l], 'rw_w2': rw_w2[l], 'rw_a0': rw_a0[l], 'rw_a2': rw_a2[l],
            'rw_g2': rw_g2[l], 'rw_k_k': rw_k_k[l], 'rw_k_a': rw_k_a[l], 'rw_r_k': rw_r_k[l],
            'rw_ln_w': rw_ln_w[l], 'rw_ln_b': rw_ln_b[l], 'w_out': w_out[l],
            'norm_xattn': norm_xattn[l], 'xa_wq': xa_wq[l], 'xa_wo': xa_wo[l],
            'norm_ffn': norm_ffn[l], 'router_w': router_w[l], 'router_b': router_b[l],
            'moe_w1': moe_w1[l], 'moe_b1': moe_b1[l], 'moe_w2': moe_w2[l], 'moe_b2': moe_b2[l],
        }
        mk_p, mv_p = mem_kv(mem_prompt, norm_mem[l], xa_wk[l], xa_wv[l])
        s0_p = jnp.zeros((b_p, RW_H, RW_HEAD, RW_HEAD), jnp.float32)
        sh0_p = jnp.zeros((b_p, RW_COLS), x_prompt.dtype)
        yp, (s1, sh1, k1, v1, ki1) = layer(yp, mk_p, mv_p, s0_p, sh0_p,
                                          functools.partial(dsa_prompt, rel_bias=rel_bias), P)
        p_s.append(s1); p_sh.append(sh1); p_k.append(k1); p_v.append(v1); p_ki.append(ki1)
        p_mk.append(mk_p); p_mv.append(mv_p)
        ys, (s2, sh2, k2, v2, ki2) = layer(
            ys, cache_mem_k[l], cache_mem_v[l], state_rwkv[l], state_shift[l],
            functools.partial(dsa_sample, rel_bias=rel_bias, cache_k=cache_k[l], cache_v=cache_v[l],
                              cache_ki=cache_idx_k[l], page_table=page_table), P)
        s_s.append(s2); s_sh.append(sh2); s_k.append(k2); s_v.append(v2); s_ki.append(ki2)
    y_prompt = rmsnorm(yp, norm_final)
    y_sample = rmsnorm(ys, norm_final)
    p_state_rwkv = jnp.stack(p_s)
    p_state_shift = jnp.stack(p_sh)
    p_k_rows = jnp.stack(p_k)
    p_v_rows = jnp.stack(p_v)
    p_idx_k_rows = jnp.stack(p_ki)
    p_mem_k = jnp.stack(p_mk)
    p_mem_v = jnp.stack(p_mv)
    s_state_rwkv = jnp.stack(s_s)
    s_state_shift = jnp.stack(s_sh)
    s_k_rows = jnp.stack(s_k)
    s_v_rows = jnp.stack(s_v)
    s_idx_k_rows = jnp.stack(s_ki)
    return (y_prompt, y_sample, p_state_rwkv, p_state_shift, p_k_rows, p_v_rows, p_idx_k_rows,
            p_mem_k, p_mem_v, s_state_rwkv, s_state_shift, s_k_rows, s_v_rows, s_idx_k_rows)
```

```python
import functools
import math

import numpy as np
import jax
import jax.numpy as jnp
from jax import lax
from jax.experimental import pallas as pl
from jax.experimental.pallas import tpu as pltpu

F32 = jnp.float32
BF16 = jnp.bfloat16
I32 = jnp.int32

LANES = 128
VMEM_LIMIT = 56 * 1024 * 1024

NORM_EPS = 1e-5
RW_EPS = 64e-5
NEG_INF = -1e30
RW_H = 8
RW_HEAD = 64
RW_W = RW_H * RW_HEAD
DSA_H = 8
DSA_HEAD = 64
DSA_W = DSA_H * DSA_HEAD
IDX_H = 8
IDX_D = 64
TOPK_MAX = 256
XA_H = 4
XA_HEAD = 128
N_EXPERTS = 32
TOP_K = 4
SWIGLU_LIMIT = 7.0
SWIGLU_ALPHA = 1.702
NUM_BUCKETS = 32
MAX_DISTANCE = 128
PAGE_SIZE = 128
IDX_SCALE = IDX_H ** -0.5 * IDX_D ** -0.5

QB = 128
CW = 512
PAGES_PER_STEP = 16

INT_MIN = -2 ** 31
INT_MAX = 2 ** 31 - 1


def _f32_key_np(x):
    b = np.array([x], np.float32).view(np.int32)[0]
    return int(b ^ ((b >> 31) & 0x7FFFFFFF))


NEG_KEY = _f32_key_np(NEG_INF)


def _bucket_starts():
    max_exact = NUM_BUCKETS // 2
    d = np.arange(0, 4 * MAX_DISTANCE)
    large = max_exact + (np.log(np.maximum(d, 1).astype(np.float32) / np.float32(max_exact))
                         / np.float32(math.log(MAX_DISTANCE / max_exact))
                         * np.float32(NUM_BUCKETS - max_exact)).astype(np.int32)
    b = np.where(d < max_exact, d, np.minimum(large, NUM_BUCKETS - 1))
    starts = []
    for j in range(NUM_BUCKETS):
        hit = np.nonzero(b == j)[0]
        starts.append(int(hit[0]) if hit.size else None)
    return starts


BUCKET_START = _bucket_starts()


def _cparams(*sem):
    return pltpu.CompilerParams(dimension_semantics=sem, vmem_limit_bytes=VMEM_LIMIT)


def _rms(x, g):
    return x * lax.rsqrt(jnp.mean(x * x, axis=-1, keepdims=True) + NORM_EPS) * g


def _dot(a, b):
    return jnp.dot(a, b, preferred_element_type=F32)


def _dot_nt(a, b):
    return lax.dot_general(a, b, (((1,), (1,)), ((), ())), preferred_element_type=F32)


def _softplus(x):
    return jnp.maximum(x, 0.0) + jnp.log1p(jnp.exp(-jnp.abs(x)))


def _sigmoid(x):
    return 1.0 / (1.0 + jnp.exp(-x))


def _float_key(x):
    b = lax.bitcast_convert_type(x + 0.0, I32)
    return b ^ (lax.shift_right_arithmetic(b, 31) & 0x7FFFFFFF)


C_RW = 0
C_Q = 1792
C_K = C_Q + DSA_W
C_V = C_K + DSA_W
C_QI = C_V + DSA_W
C_KIW = C_QI + IDX_H * IDX_D
C_END = C_KIW + LANES


def _in_proj_kernel(x_ref, g_ref, w_ref, prw_ref, q_ref, k_ref, v_ref, qi_ref, kiw_ref,
                    kT_ref, vb_ref, kiT_ref):
    h = _rms(x_ref[0], g_ref[...]).astype(BF16)

    def mm(lo, hi):
        return _dot(h, w_ref[:, lo:hi])

    prw_ref[0] = mm(C_RW, C_Q)
    q_ref[0] = mm(C_Q, C_K).astype(BF16)
    k = mm(C_K, C_V)
    k_ref[0] = k
    kT_ref[0] = k.T.astype(BF16)
    v = mm(C_V, C_QI)
    v_ref[0] = v
    vb_ref[0] = v.astype(BF16)
    qi_ref[0] = mm(C_QI, C_KIW).astype(BF16)
    kiw = mm(C_KIW, C_END)
    kiw_ref[0] = kiw
    kiT_ref[0] = kiw.T[0:IDX_D, :].astype(BF16)


def _in_proj(x, g, w_pad, tm):
    b, t, d = x.shape
    grid = (b, t // tm)
    row = lambda w, dt: jax.ShapeDtypeStruct((b, t, w), dt)
    rspec = lambda w: pl.BlockSpec((1, tm, w), lambda i, j: (i, j, 0))
    tspec = lambda w: pl.BlockSpec((1, w, tm), lambda i, j: (i, 0, j))
    return pl.pallas_call(
        _in_proj_kernel,
        grid=grid,
        in_specs=[rspec(d),
                  pl.BlockSpec((1, d), lambda i, j: (0, 0)),
                  pl.BlockSpec((d, C_END), lambda i, j: (0, 0))],
        out_specs=[rspec(C_Q), rspec(DSA_W), rspec(DSA_W), rspec(DSA_W), rspec(IDX_H * IDX_D),
                   rspec(LANES), tspec(DSA_W), rspec(DSA_W), tspec(IDX_D)],
        out_shape=[row(C_Q, F32), row(DSA_W, BF16), row(DSA_W, F32), row(DSA_W, F32),
                   row(IDX_H * IDX_D, BF16), row(LANES, F32),
                   jax.ShapeDtypeStruct((b, DSA_W, t), BF16), row(DSA_W, BF16),
                   jax.ShapeDtypeStruct((b, IDX_D, t), BF16)],
        compiler_params=_cparams("arbitrary", "arbitrary"),
        name="in_proj",
    )(x, g, w_pad)


def _rwkv_kernel(n_steps, prw_ref, sh0_ref, s0_ref, mu_ref, w0_ref, w2_ref, a0_ref, a2_ref, g2_ref,
                 kkw_ref, kaw_ref, rk_ref, lnw_ref, lnb_ref,
                 y_ref, s1_ref, sh1_ref,
                 st_scr, carry_scr, r_s, w_s, k_s, v_s, kk_s, b_s, y_s, g_s):
    nb, tc = prw_ref.shape[0], prw_ref.shape[1]
    c = pl.program_id(1)

    @pl.when(c == 0)
    def _():
        st_scr[...] = s0_ref[...]
        carry_scr[...] = sh0_ref[...]

    row = lax.broadcasted_iota(I32, (tc, 1), 0)
    lane = lax.broadcasted_iota(I32, (1, LANES), 1)
    for b in range(nb):
        x = prw_ref[b]
        prev = jnp.where(row == 0, carry_scr[b], pltpu.roll(x, 1, 0))
        carry_scr[b] = x[n_steps - 1:n_steps, :]
        ps = x + (prev - x) * mu_ref[...]
        r = ps[:, 0:RW_W]
        k = ps[:, RW_W:2 * RW_W]
        v = ps[:, 2 * RW_W:3 * RW_W]
        la = ps[:, 3 * RW_W:3 * RW_W + LANES]
        gd = ps[:, 3 * RW_W + LANES:3 * RW_W + 2 * LANES]
        z = jnp.where(lane < 64, jnp.tanh(la), la).astype(BF16)
        w_log = -_softplus(-(w0_ref[...] + _dot(z, w2_ref[...]))) - 0.5
        decay = jnp.exp(-jnp.exp(w_log))
        a = _sigmoid(a0_ref[...] + _dot(z, a2_ref[...]))
        g_s[b] = _dot(_sigmoid(gd).astype(BF16), g2_ref[...])
        kk = k * kkw_ref[...]
        k2 = k * (1.0 + (a - 1.0) * kaw_ref[...])
        for h in range(RW_H):
            sl = slice(RW_HEAD * h, RW_HEAD * (h + 1))
            kkh = kk[:, sl]
            nrm = jnp.sqrt(jnp.sum(kkh * kkh, axis=-1, keepdims=True))
            kkh = kkh / jnp.maximum(nrm, 1e-12)
            r_s[b, h] = r[:, sl]
            w_s[b, h] = decay[:, sl]
            k_s[b, h] = k2[:, sl]
            v_s[b, h] = v[:, sl]
            kk_s[b, h] = kkh
            b_s[b, h] = kkh * a[:, sl]

    eye = (lax.broadcasted_iota(I32, (RW_HEAD, RW_HEAD), 0)
           == lax.broadcasted_iota(I32, (RW_HEAD, RW_HEAD), 1))

    def step(t, carry):
        for b in range(nb):
            for h in range(RW_H):
                s = st_scr[b, h]
                tt = pl.ds(t, 1)
                sa = -jnp.sum(s * kk_s[b, h, tt, :], axis=1, keepdims=True)
                vcol = jnp.sum(jnp.where(eye, v_s[b, h, tt, :], 0.0), axis=1, keepdims=True)
                s = s * w_s[b, h, tt, :] + sa * b_s[b, h, tt, :] + vcol * k_s[b, h, tt, :]
                st_scr[b, h] = s
                ycol = jnp.sum(s * r_s[b, h, tt, :], axis=1, keepdims=True)
                y_s[b, h, tt, :] = jnp.sum(jnp.where(eye, ycol, 0.0), axis=0, keepdims=True)
        return carry

    lax.fori_loop(0, n_steps, step, 0)

    for b in range(nb):
        for h in range(RW_H):
            sl = slice(RW_HEAD * h, RW_HEAD * (h + 1))
            y = y_s[b, h]
            mean = jnp.mean(y, axis=-1, keepdims=True)
            var = jnp.mean(jnp.square(y - mean), axis=-1, keepdims=True)
            yn = (y - mean) * lax.rsqrt(var + RW_EPS) * lnw_ref[h:h + 1, :] + lnb_ref[h:h + 1, :]
            rr, kk2, vv = r_s[b, h], k_s[b, h], v_s[b, h]
            bonus = jnp.sum(rr * kk2 * rk_ref[h:h + 1, :], axis=-1, keepdims=True) * vv
            y_ref[b, :, sl] = ((yn + bonus) * g_s[b][:, sl]).astype(BF16)
    s1_ref[...] = st_scr[...]
    sh1_ref[...] = carry_scr[...]


def _rwkv(prw, shift0, s0, P, nb, tc, n_valid):
    b, t, cols = prw.shape
    n_steps = min(tc, n_valid)
    grid = (b // nb, t // tc)
    full = lambda a: pl.BlockSpec(a.shape, lambda i, j: (0,) * a.ndim)
    params = [P['mu'], P['w0'], P['w2'], P['a0'], P['a2'], P['g2'], P['kk'], P['ka'], P['rk'],
              P['lnw'], P['lnb']]
    hs = lambda: pltpu.VMEM((nb, RW_H, tc, RW_HEAD), F32)
    return pl.pallas_call(
        functools.partial(_rwkv_kernel, n_steps),
        grid=grid,
        in_specs=[pl.BlockSpec((nb, tc, cols), lambda i, j: (i, j, 0)),
                  pl.BlockSpec((nb, 1, cols), lambda i, j: (i, 0, 0)),
                  pl.BlockSpec((nb, RW_H, RW_HEAD, RW_HEAD), lambda i, j: (i, 0, 0, 0))]
                 + [full(a) for a in params],
        out_specs=[pl.BlockSpec((nb, tc, RW_W), lambda i, j: (i, j, 0)),
                   pl.BlockSpec((nb, RW_H, RW_HEAD, RW_HEAD), lambda i, j: (i, 0, 0, 0)),
                   pl.BlockSpec((nb, 1, cols), lambda i, j: (i, 0, 0))],
        out_shape=[jax.ShapeDtypeStruct((b, t, RW_W), BF16),
                   jax.ShapeDtypeStruct((b, RW_H, RW_HEAD, RW_HEAD), F32),
                   jax.ShapeDtypeStruct((b, 1, cols), F32)],
        scratch_shapes=[pltpu.VMEM((nb, RW_H, RW_HEAD, RW_HEAD), F32),
                        pltpu.VMEM((nb, 1, cols), F32),
                        hs(), hs(), hs(), hs(), hs(), hs(), hs(),
                        pltpu.VMEM((nb, tc, RW_W), F32)],
        compiler_params=_cparams("arbitrary", "arbitrary"),
        name="rwkv",
    )(prw, shift0, s0, *params)


def _topk_mask(key_ref, madd_ref, p_scr, rows, nch, k_sel, lim, idx_bits):
    ntile = CW // LANES
    lane = lax.broadcasted_iota(I32, (rows, LANES), 1)

    def count(pred):
        def body(c, acc):
            off = pl.multiple_of(c * CW, CW)
            x = key_ref[:, pl.ds(off, CW)]
            for t in range(ntile):
                acc = acc + pred(x[:, LANES * t:LANES * (t + 1)], off + LANES * t)
            return acc
        acc = lax.fori_loop(0, nch, body, jnp.zeros((rows, LANES), I32))
        return jnp.sum(acc, axis=1, keepdims=True)

    def bcast(col):
        return jnp.broadcast_to(col, (rows, LANES))

    def value_bit(it, tau):
        cand = tau | jnp.left_shift(jnp.int32(1), 31 - it)
        cb = bcast(cand ^ INT_MIN)
        cnt = count(lambda x, off: jnp.where(x >= cb, 1, 0))
        return jnp.where(cnt >= k_sel, cand, tau)

    tau = lax.fori_loop(0, 32, value_bit, jnp.zeros((rows, 1), I32))
    thr = tau ^ INT_MIN
    thr_b = bcast(thr)
    cnt_gt = count(lambda x, off: jnp.where(x > thr_b, 1, 0))
    cnt_ge = count(lambda x, off: jnp.where(x >= thr_b, 1, 0))
    need = k_sel - cnt_gt
    tie = jnp.where(cnt_ge > k_sel, jnp.where(thr > NEG_KEY, 1, 0), 0)

    p_scr[...] = jnp.full((rows, LANES), INT_MAX, I32)

    @pl.when(jnp.max(tie) > 0)
    def _():
        def index_bit(it, p):
            cand = p | jnp.left_shift(jnp.int32(1), idx_bits - 1 - it)
            cb = bcast(cand)
            cnt = count(lambda x, off: jnp.where(x == thr_b, jnp.where(off + lane < cb, 1, 0), 0))
            return jnp.where(cnt < need, cand, p)
        p = lax.fori_loop(0, idx_bits, index_bit, jnp.zeros((rows, 1), I32))
        p_scr[...] = bcast(p)

    p_b = p_scr[...]
    lim_b = bcast(lim)

    def fin(c, carry):
        off = pl.multiple_of(c * CW, CW)
        x = key_ref[:, pl.ds(off, CW)]
        for t in range(ntile):
            xt = x[:, LANES * t:LANES * (t + 1)]
            idx = off + LANES * t + lane
            sel = jnp.where(xt > thr_b, 0.0,
                            jnp.where(xt == thr_b, jnp.where(idx <= p_b, 0.0, NEG_INF), NEG_INF))
            madd_ref[:, pl.ds(pl.multiple_of(off + LANES * t, LANES), LANES)] = (
                jnp.where(idx <= lim_b, sel, NEG_INF))
        return carry

    lax.fori_loop(0, nch, fin, 0)


def _bias_table(rb_ref, h, d):
    bias = jnp.full(d.shape, rb_ref[0, h], F32)
    for j in range(1, NUM_BUCKETS):
        if BUCKET_START[j] is not None:
            bias = jnp.where(d >= BUCKET_START[j], rb_ref[j, h], bias)
    return bias


def _dsa_prompt_kernel(k_sel, idx_bits, rb_ref, q_ref, qi_ref, kiw_ref, kT_ref, vb_ref, kiT_ref,
                       o_ref, key_scr, madd_scr, p_scr, btab_scr):
    i = pl.program_id(1)
    nch = ((i + 1) * QB + CW - 1) // CW

    @pl.when((pl.program_id(0) == 0) & (i == 0))
    def _():
        d = (lax.broadcasted_iota(I32, (QB, 2 * QB), 0)
             - lax.broadcasted_iota(I32, (QB, 2 * QB), 1) + QB)
        for h in range(DSA_H):
            btab_scr[h] = _bias_table(rb_ref, h, d)

    tq = i * QB + lax.broadcasted_iota(I32, (QB, 1), 0)
    qi = qi_ref[0]
    wi = kiw_ref[0][:, IDX_D:IDX_D + IDX_H]
    qi_h = [qi[:, IDX_D * h:IDX_D * (h + 1)] for h in range(IDX_H)]
    wi_h = [wi[:, h:h + 1] for h in range(IDX_H)]

    def score_chunk(c, carry):
        off = pl.multiple_of(c * CW, CW)
        kic = kiT_ref[0, :, pl.ds(off, CW)]
        acc = jnp.zeros((QB, CW), F32)
        for h in range(IDX_H):
            acc = acc + jnp.maximum(_dot(qi_h[h], kic), 0.0) * wi_h[h]
        spos = off + lax.broadcasted_iota(I32, (1, CW), 1)
        sc = jnp.where(spos <= tq, acc * IDX_SCALE, NEG_INF)
        key_scr[:, pl.ds(off, CW)] = _float_key(sc)
        return carry

    lax.fori_loop(0, nch, score_chunk, 0)

    _topk_mask(key_scr, madd_scr, p_scr, QB, nch, k_sel, tq, idx_bits)

    q = q_ref[0]
    tiles = CW // QB
    for h in range(DSA_H):
        sl = slice(DSA_HEAD * h, DSA_HEAD * (h + 1))
        qh = q[:, sl]
        far = rb_ref[NUM_BUCKETS - 1, h]

        def attend(c, carry, h=h, sl=sl, qh=qh, far=far):
            m, l, acc = carry
            off = pl.multiple_of(c * CW, CW)
            s = _dot(qh, kT_ref[0, sl, pl.ds(off, CW)]) * (DSA_HEAD ** -0.5)
            parts = []
            for t in range(tiles):
                delta = i - (c * tiles + t)
                parts.append(jnp.where(delta == 0, btab_scr[h, :, QB:2 * QB],
                                       jnp.where(delta == 1, btab_scr[h, :, 0:QB], far)))
            s = s + jnp.concatenate(parts, axis=1) + madd_scr[:, pl.ds(off, CW)]
            m_new = jnp.maximum(m, jnp.max(s, axis=1, keepdims=True))
            alpha = jnp.exp(m - m_new)
            p = jnp.exp(s - m_new)
            l = l * alpha + jnp.sum(p, axis=1, keepdims=True)
            acc = acc * alpha + _dot(p.astype(BF16), vb_ref[0, pl.ds(off, CW), sl])
            return m_new, l, acc

        init = (jnp.full((QB, 1), NEG_INF, F32), jnp.zeros((QB, 1), F32),
                jnp.zeros((QB, DSA_HEAD), F32))
        _, l, acc = lax.fori_loop(0, nch, attend, init)
        o_ref[0, :, sl] = (acc / l).astype(BF16)


def _dsa_prompt(rel_bias, q, qi, kiw, kT, vb, kiT):
    b, t, _ = q.shape
    k_sel = min(TOPK_MAX, t // 4)
    idx_bits = max(1, int(math.ceil(math.log2(t))))
    blk = lambda w: pl.BlockSpec((1, QB, w), lambda bi, i: (bi, i, 0))
    return pl.pallas_call(
        functools.partial(_dsa_prompt_kernel, k_sel, idx_bits),
        grid=(b, t // QB),
        in_specs=[pl.BlockSpec(memory_space=pltpu.SMEM),
                  blk(DSA_W), blk(IDX_H * IDX_D), blk(LANES),
                  pl.BlockSpec((1, DSA_W, t), lambda bi, i: (bi, 0, 0)),
                  pl.BlockSpec((1, t, DSA_W), lambda bi, i: (bi, 0, 0)),
                  pl.BlockSpec((1, IDX_D, t), lambda bi, i: (bi, 0, 0))],
        out_specs=blk(DSA_W),
        out_shape=jax.ShapeDtypeStruct((b, t, DSA_W), BF16),
        scratch_shapes=[pltpu.VMEM((QB, t), I32), pltpu.VMEM((QB, t), F32),
                        pltpu.VMEM((QB, LANES), I32),
                        pltpu.VMEM((DSA_H, QB, 2 * QB), F32)],
        compiler_params=_cparams("arbitrary", "arbitrary"),
        name="dsa_prompt",
    )(rel_bias, q, qi, kiw, kT, vb, kiT)


SROWS = 8
SCHUNK = PAGES_PER_STEP * PAGE_SIZE


def _dsa_s_scores_kernel(n_tok, n_past_chunks, pt_ref, qi_ref, wcol_ref, kinew_ref, *rest):
    pages, out_ref = rest[:PAGES_PER_STEP], rest[PAGES_PER_STEP]
    c = pl.program_id(1)
    qi = qi_ref[0]
    wcol = wcol_ref[0]

    def head_sum(s):
        s = jnp.maximum(s, 0.0) * wcol
        return jnp.sum(s.reshape(n_tok, IDX_H, s.shape[-1]), axis=1) * IDX_SCALE + 0.0

    @pl.when(c < n_past_chunks)
    def _():
        for p in range(PAGES_PER_STEP):
            s = _dot_nt(qi, pages[p][0].astype(BF16))
            out_ref[0, 0:n_tok, PAGE_SIZE * p:PAGE_SIZE * (p + 1)] = head_sum(s)
        out_ref[0, n_tok:SROWS, :] = jnp.full((SROWS - n_tok, SCHUNK), NEG_INF, F32)

    @pl.when(c == n_past_chunks)
    def _():
        s = head_sum(_dot_nt(qi, kinew_ref[0]))
        tr = lax.broadcasted_iota(I32, (n_tok, SROWS), 0)
        tc = lax.broadcasted_iota(I32, (n_tok, SROWS), 1)
        out_ref[...] = jnp.full((1, SROWS, SCHUNK), NEG_INF, F32)
        out_ref[0, 0:n_tok, 0:SROWS] = jnp.where(tc <= tr, s, NEG_INF)


def _dsa_s_scores(page_table, cache_ki, qi_rows, wcol, ki_new, n_tok):
    b, n_pages = page_table.shape
    npc = n_pages // PAGES_PER_STEP
    page_specs = [
        pl.BlockSpec((1, PAGE_SIZE, IDX_D),
                     lambda bi, c, pt, j=j: (pt[bi, jnp.minimum(c, npc - 1) * PAGES_PER_STEP + j], 0, 0))
        for j in range(PAGES_PER_STEP)]
    rows = qi_rows.shape[1]
    grid_spec = pltpu.PrefetchScalarGridSpec(
        num_scalar_prefetch=1,
        grid=(b, npc + 1),
        in_specs=[pl.BlockSpec((1, rows, IDX_D), lambda bi, c, pt: (bi, 0, 0)),
                  pl.BlockSpec((1, rows, 1), lambda bi, c, pt: (bi, 0, 0)),
                  pl.BlockSpec((1, SROWS, IDX_D), lambda bi, c, pt: (bi, 0, 0))] + page_specs,
        out_specs=pl.BlockSpec((1, SROWS, SCHUNK), lambda bi, c, pt: (bi, 0, c)),
    )
    return pl.pallas_call(
        functools.partial(_dsa_s_scores_kernel, n_tok, npc),
        grid_spec=grid_spec,
        out_shape=jax.ShapeDtypeStruct((b, SROWS, (npc + 1) * SCHUNK), F32),
        compiler_params=_cparams("arbitrary", "arbitrary"),
        name="dsa_s_scores",
    )(page_table, qi_rows, wcol, ki_new, *([cache_ki] * PAGES_PER_STEP))


def _topk_rows_kernel(k_sel, idx_bits, sc_ref, lim_ref, madd_ref, key_scr, p_scr):
    rows, width = sc_ref.shape
    nch = width // CW

    def to_key(c, carry):
        off = pl.multiple_of(c * CW, CW)
        key_scr[:, pl.ds(off, CW)] = _float_key(sc_ref[:, pl.ds(off, CW)])
        return carry

    lax.fori_loop(0, nch, to_key, 0)
    _topk_mask(key_scr, madd_ref, p_scr, rows, nch, k_sel, lim_ref[...], idx_bits)


def _topk_rows(scores, lim, k_sel, rows_per_step):
    n, width = scores.shape
    idx_bits = max(1, int(math.ceil(math.log2(width))))
    return pl.pallas_call(
        functools.partial(_topk_rows_kernel, k_sel, idx_bits),
        grid=(n // rows_per_step,),
        in_specs=[pl.BlockSpec((rows_per_step, width), lambda i: (i, 0)),
                  pl.BlockSpec((rows_per_step, 1), lambda i: (i, 0))],
        out_specs=pl.BlockSpec((rows_per_step, width), lambda i: (i, 0)),
        out_shape=jax.ShapeDtypeStruct((n, width), F32),
        scratch_shapes=[pltpu.VMEM((rows_per_step, width), I32),
                        pltpu.VMEM((rows_per_step, LANES), I32)],
        compiler_params=_cparams("arbitrary"),
        name="topk_rows",
    )(scores, lim)


def _dsa_s_attn_kernel(n_tok, n_past_chunks, past, pt_ref, qblk_ref, madd_ref, rbrow_ref,
                       knew_ref, vnew_ref, *rest):
    kp = rest[:PAGES_PER_STEP]
    vp = rest[PAGES_PER_STEP:2 * PAGES_PER_STEP]
    o_ref, m_scr, l_scr, acc_scr = rest[2 * PAGES_PER_STEP:]
    c = pl.program_id(1)
    rows = n_tok * DSA_H
    qblk = qblk_ref[0]

    @pl.when(c == 0)
    def _():
        m_scr[...] = jnp.full(m_scr.shape, NEG_INF, F32)
        l_scr[...] = jnp.zeros(l_scr.shape, F32)
        acc_scr[...] = jnp.zeros(acc_scr.shape, F32)

    def bias_for(width, key0):
        tok = lax.broadcasted_iota(I32, (rows, width), 0) // DSA_H
        d = past + tok - (key0 + lax.broadcasted_iota(I32, (rows, width), 1))
        bias = jnp.broadcast_to(rbrow_ref[:, 0:1], (rows, width))
        for j in range(1, NUM_BUCKETS):
            if BUCKET_START[j] is not None:
                bias = jnp.where(d >= BUCKET_START[j], rbrow_ref[:, j:j + 1], bias)
        return bias

    def update(s, pv_fn):
        m = m_scr[...]
        m_new = jnp.maximum(m, jnp.max(s, axis=1, keepdims=True))
        alpha = jnp.exp(m - m_new)
        p = jnp.exp(s - m_new)
        l_scr[...] = l_scr[...] * alpha + jnp.sum(p, axis=1, keepdims=True)
        acc_scr[...] = acc_scr[...] * alpha + pv_fn(p.astype(BF16))
        m_scr[...] = m_new

    def expand_mask(width):
        ma = madd_ref[0, 0:n_tok, 0:width]
        return jnp.broadcast_to(ma[:, None, :], (n_tok, DSA_H, width)).reshape(rows, width)

    @pl.when(c < n_past_chunks)
    def _():
        s = jnp.concatenate([_dot_nt(qblk, kp[p][0].astype(BF16)) for p in range(PAGES_PER_STEP)],
                            axis=1) * (DSA_HEAD ** -0.5)
        near = c == n_past_chunks - 1
        far_bias = rbrow_ref[:, NUM_BUCKETS - 1:NUM_BUCKETS]
        bias = lax.cond(near,
                        lambda: bias_for(SCHUNK, (n_past_chunks - 1) * SCHUNK),
                        lambda: jnp.broadcast_to(far_bias, (rows, SCHUNK)))
        s = s + bias + expand_mask(SCHUNK)

        def pv(pb):
            out = jnp.zeros((rows, DSA_W), F32)
            for p in range(PAGES_PER_STEP):
                out = out + _dot(pb[:, PAGE_SIZE * p:PAGE_SIZE * (p + 1)], vp[p][0].astype(BF16))
            return out
        update(s, pv)

    @pl.when(c == n_past_chunks)
    def _():
        s = _dot_nt(qblk, knew_ref[0]) * (DSA_HEAD ** -0.5)
        s = s + bias_for(SROWS, past) + expand_mask(SROWS)
        update(s, lambda pb: _dot(pb, vnew_ref[0]))
        res = acc_scr[...] / l_scr[...]
        hrow = lax.broadcasted_iota(I32, (DSA_H, DSA_W), 0)
        hcol = lax.broadcasted_iota(I32, (DSA_H, DSA_W), 1) // DSA_HEAD
        own = jnp.where(hrow == hcol, 1.0, 0.0)
        out = jnp.sum(res.reshape(n_tok, DSA_H, DSA_W) * own[None], axis=1)
        o_ref[...] = jnp.zeros(o_ref.shape, o_ref.dtype)
        o_ref[0, 0:n_tok, :] = out.astype(o_ref.dtype)


def _dsa_s_attn(page_table, cache_k, cache_v, qblk, madd, rbrow, k_new, v_new, n_tok):
    b, n_pages = page_table.shape
    npc = n_pages // PAGES_PER_STEP
    rows = n_tok * DSA_H

    def page_spec(j):
        return pl.BlockSpec(
            (1, PAGE_SIZE, DSA_W),
            lambda bi, c, pt, j=j: (pt[bi, jnp.minimum(c, npc - 1) * PAGES_PER_STEP + j], 0, 0))

    page_specs = [page_spec(j) for j in range(PAGES_PER_STEP)]
    grid_spec = pltpu.PrefetchScalarGridSpec(
        num_scalar_prefetch=1,
        grid=(b, npc + 1),
        in_specs=[pl.BlockSpec((1, rows, DSA_W), lambda bi, c, pt: (bi, 0, 0)),
                  pl.BlockSpec((1, SROWS, SCHUNK), lambda bi, c, pt: (bi, 0, c)),
                  pl.BlockSpec((rows, NUM_BUCKETS), lambda bi, c, pt: (0, 0)),
                  pl.BlockSpec((1, SROWS, DSA_W), lambda bi, c, pt: (bi, 0, 0)),
                  pl.BlockSpec((1, SROWS, DSA_W), lambda bi, c, pt: (bi, 0, 0))]
                 + page_specs + page_specs,
        out_specs=pl.BlockSpec((1, SROWS, DSA_W), lambda bi, c, pt: (bi, 0, 0)),
        scratch_shapes=[pltpu.VMEM((rows, 1), F32), pltpu.VMEM((rows, 1), F32),
                        pltpu.VMEM((rows, DSA_W), F32)],
    )
    return pl.pallas_call(
        functools.partial(_dsa_s_attn_kernel, n_tok, npc, n_pages * PAGE_SIZE),
        grid_spec=grid_spec,
        out_shape=jax.ShapeDtypeStruct((b, SROWS, DSA_W), BF16),
        compiler_params=_cparams("arbitrary", "arbitrary"),
        name="dsa_s_attn",
    )(page_table, qblk, madd, rbrow, k_new, v_new,
      *([cache_k] * PAGES_PER_STEP), *([cache_v] * PAGES_PER_STEP))


def _mem_kv_kernel(x_ref, g_ref, w_ref, k_ref, v_ref):
    h = _rms(x_ref[...], g_ref[...]).astype(BF16)
    width = k_ref.shape[-1]
    k_ref[...] = _dot(h, w_ref[:, 0:width])
    v_ref[...] = _dot(h, w_ref[:, width:2 * width])


def _mem_kv(mem, g, wkv, tm):
    n, d = mem.shape
    width = wkv.shape[1] // 2
    return pl.pallas_call(
        _mem_kv_kernel,
        grid=(n // tm,),
        in_specs=[pl.BlockSpec((tm, d), lambda i: (i, 0)),
                  pl.BlockSpec((1, d), lambda i: (0, 0)),
                  pl.BlockSpec(wkv.shape, lambda i: (0, 0))],
        out_specs=[pl.BlockSpec((tm, width), lambda i: (i, 0))] * 2,
        out_shape=[jax.ShapeDtypeStruct((n, width), F32)] * 2,
        compiler_params=_cparams("arbitrary"),
        name="mem_kv",
    )(mem, g, wkv)


def _split_bf16(x):
    hi = x.astype(BF16)
    return hi, (x - hi.astype(F32)).astype(BF16)


def _mix_kernel(x_ref, yrw_ref, ydsa_ref, mk_ref, mv_ref, woa_ref, wob_ref, gx_ref, wq_ref, wo_ref,
                gf_ref, rwh_ref, rwl_ref, rb_ref, x2_ref, h3_ref, comb_ref):
    x1 = x_ref[0] + _dot(yrw_ref[0], woa_ref[...]) + _dot(ydsa_ref[0], wob_ref[...])
    h2 = _rms(x1, gx_ref[...]).astype(BF16)
    qx = _dot(h2, wq_ref[...])
    mk = mk_ref[0].astype(BF16)
    mv = mv_ref[0].astype(BF16)
    heads = []
    for h in range(XA_H):
        sl = slice(XA_HEAD * h, XA_HEAD * (h + 1))
        lg = _dot_nt(qx[:, sl].astype(BF16), mk[:, sl]) * (XA_HEAD ** -0.5)
        p = jnp.exp(lg - jnp.max(lg, axis=1, keepdims=True))
        l = jnp.sum(p, axis=1, keepdims=True)
        heads.append(_dot(p.astype(BF16), mv[:, sl]) / l)
    o = jnp.concatenate(heads, axis=1).astype(BF16)
    x2 = x1 + _dot(o, wo_ref[...])
    x2_ref[0] = x2
    h3 = _rms(x2, gf_ref[...])
    h3_ref[0] = h3.astype(BF16)

    hh, hl = _split_bf16(h3)
    logits = (_dot(hh, rwh_ref[...]) + _dot(hh, rwl_ref[...]) + _dot(hl, rwh_ref[...])
              + rb_ref[...])
    lane = lax.broadcasted_iota(I32, logits.shape, 1)
    work = logits
    vals, idxs = [], []
    for _ in range(TOP_K):
        mx = jnp.max(work, axis=1, keepdims=True)
        ix = jnp.min(jnp.where(work == mx, lane, N_EXPERTS), axis=1, keepdims=True)
        vals.append(mx)
        idxs.append(ix)
        work = jnp.where(lane == ix, -jnp.inf, work)
    es = [jnp.exp(v - vals[0]) for v in vals]
    den = es[0] + es[1] + es[2] + es[3]
    comb = jnp.zeros(logits.shape, F32)
    for e, ix in zip(es, idxs):
        comb = comb + jnp.where(lane == ix, e / den, 0.0)
    comb_ref[0] = comb


def _mix(x, yrw, ydsa, mk, mv, W, tm):
    b, t, d = x.shape
    mem_len, xa_w = mk.shape[1], mk.shape[2]
    full = lambda a: pl.BlockSpec(a.shape, lambda i, j: (0,) * a.ndim)
    rspec = lambda w: pl.BlockSpec((1, tm, w), lambda i, j: (i, j, 0))
    mspec = pl.BlockSpec((1, mem_len, xa_w), lambda i, j: (i, 0, 0))
    params = [W['woa'], W['wob'], W['gx'], W['wq'], W['wo'], W['gf'], W['rwh'], W['rwl'], W['rb']]
    return pl.pallas_call(
        _mix_kernel,
        grid=(b, t // tm),
        in_specs=[rspec(d), rspec(RW_W), rspec(DSA_W), mspec, mspec] + [full(a) for a in params],
        out_specs=[rspec(d), rspec(d), rspec(N_EXPERTS)],
        out_shape=[jax.ShapeDtypeStruct((b, t, d), F32), jax.ShapeDtypeStruct((b, t, d), BF16),
                   jax.ShapeDtypeStruct((b, t, N_EXPERTS), F32)],
        compiler_params=_cparams("arbitrary", "arbitrary"),
        name="mix",
    )(x, yrw, ydsa, mk, mv, *params)


def _moe_kernel(h_ref, comb_ref, x_ref, w1_ref, b1_ref, w2_ref, b2_ref, gfin_ref, o_ref, acc_scr):
    e = pl.program_id(1)
    d_ff = w2_ref.shape[1]

    @pl.when(e == 0)
    def _():
        acc_scr[...] = jnp.zeros(acc_scr.shape, F32)

    h = h_ref[...]
    gate = jnp.minimum(_dot(h, w1_ref[0, :, 0:d_ff]) + b1_ref[0, :, 0:d_ff], SWIGLU_LIMIT)
    up = jnp.clip(_dot(h, w1_ref[0, :, d_ff:2 * d_ff]) + b1_ref[0, :, d_ff:2 * d_ff],
                  -SWIGLU_LIMIT, SWIGLU_LIMIT)
    glu = gate * _sigmoid(SWIGLU_ALPHA * gate)
    out = _dot(((up + 1.0) * glu).astype(BF16), w2_ref[0]) + b2_ref[0]
    comb = comb_ref[...]
    lane = lax.broadcasted_iota(I32, comb.shape, 1)
    wcol = jnp.sum(jnp.where(lane == e, comb, 0.0), axis=1, keepdims=True)
    acc_scr[...] += wcol * out

    @pl.when(e == pl.num_programs(1) - 1)
    def _():
        o_ref[...] = _rms(x_ref[...] + acc_scr[...], gfin_ref[...])


def _moe(h3, comb, x2, w1, b1, w2, b2, gfin, tm):
    n, d = h3.shape
    n_exp, _, two_ff = w1.shape
    d_ff = two_ff // 2
    return pl.pallas_call(
        _moe_kernel,
        grid=(n // tm, n_exp),
        in_specs=[pl.BlockSpec((tm, d), lambda i, e: (i, 0)),
                  pl.BlockSpec((tm, n_exp), lambda i, e: (i, 0)),
                  pl.BlockSpec((tm, d), lambda i, e: (i, 0)),
                  pl.BlockSpec((1, d, two_ff), lambda i, e: (e, 0, 0)),
                  pl.BlockSpec((1, 1, two_ff), lambda i, e: (e, 0, 0)),
                  pl.BlockSpec((1, d_ff, d), lambda i, e: (e, 0, 0)),
                  pl.BlockSpec((1, 1, d), lambda i, e: (e, 0, 0)),
                  pl.BlockSpec((1, d), lambda i, e: (0, 0))],
        out_specs=pl.BlockSpec((tm, d), lambda i, e: (i, 0)),
        out_shape=jax.ShapeDtypeStruct((n, d), F32),
        scratch_shapes=[pltpu.VMEM((tm, d), F32)],
        compiler_params=_cparams("arbitrary", "arbitrary"),
        name="moe",
    )(h3, comb, x2, w1, b1, w2, b2, gfin)


def _pick_tile(n, pref):
    t = min(pref, n)
    while n % t:
        t //= 2
    return t


def kernel(x_prompt, mem_prompt, x_sample, cache_k, cache_v, cache_idx_k, cache_mem_k, cache_mem_v, state_rwkv, state_shift, page_table, rel_bias, norm_final, norm_mix, w_in, mu_shift, rw_w0, rw_w2, rw_a0, rw_a2, rw_g2, rw_k_k, rw_k_a, rw_r_k, rw_ln_w, rw_ln_b, w_out, norm_xattn, norm_mem, xa_wq, xa_wk, xa_wv, xa_wo, norm_ffn, router_w, router_b, moe_w1, moe_b1, moe_w2, moe_b2):
    depth = w_in.shape[0]
    assert depth == 1, "single-layer step"
    l = 0
    bp, tp, d = x_prompt.shape
    bs, ts, _ = x_sample.shape
    n_pool = cache_k.shape[1]
    n_pages = page_table.shape[1]
    past = n_pages * PAGE_SIZE
    mem_len = mem_prompt.shape[1]
    xa_w = XA_H * XA_HEAD
    rw_cols = state_shift.shape[-1]
    assert tp % CW == 0 and n_pages % PAGES_PER_STEP == 0 and ts <= SROWS

    row = lambda a: a.reshape(1, -1)
    bf = lambda a: a.astype(BF16)

    w_pad = bf(jnp.pad(w_in[l], ((0, 0), (0, C_END - w_in.shape[-1]))))
    lora = rw_w2.shape[1]
    P = {
        'mu': row(mu_shift[l]), 'w0': row(rw_w0[l]), 'a0': row(rw_a0[l]),
        'w2': bf(jnp.pad(rw_w2[l], ((0, LANES - lora), (0, 0)))),
        'a2': bf(jnp.pad(rw_a2[l], ((lora, LANES - lora - rw_a2.shape[1]), (0, 0)))),
        'g2': bf(rw_g2[l]), 'kk': row(rw_k_k[l]), 'ka': row(rw_k_a[l]),
        'rk': rw_r_k[l].reshape(RW_H, RW_HEAD),
        'lnw': rw_ln_w[l].reshape(RW_H, RW_HEAD), 'lnb': rw_ln_b[l].reshape(RW_H, RW_HEAD),
    }
    rwh = bf(router_w[l])
    W = {
        'woa': bf(w_out[l][:RW_W]), 'wob': bf(w_out[l][RW_W:]), 'gx': row(norm_xattn[l]),
        'wq': bf(xa_wq[l]), 'wo': bf(xa_wo[l]), 'gf': row(norm_ffn[l]),
        'rwh': rwh, 'rwl': bf(router_w[l] - rwh.astype(F32)), 'rb': row(router_b[l]),
    }
    w1 = bf(moe_w1[l])
    w2 = bf(moe_w2[l])
    b1 = moe_b1[l][:, None, :]
    b2 = moe_b2[l][:, None, :]
    gfin = row(norm_final)
    gmix = row(norm_mix[l])

    prw, q, k, v, qi, kiw, kT, vb, kiT = _in_proj(x_prompt, gmix, w_pad, _pick_tile(tp, 256))
    y_rw, p_s1, p_sh1 = _rwkv(prw, jnp.zeros((bp, 1, rw_cols), F32),
                              jnp.zeros((bp, RW_H, RW_HEAD, RW_HEAD), F32), P,
                              nb=bp, tc=_pick_tile(tp, 256), n_valid=tp)
    y_dsa = _dsa_prompt(rel_bias, q, qi, kiw, kT, vb, kiT)
    wkv = bf(jnp.concatenate([xa_wk[l], xa_wv[l]], axis=1))
    mk_p, mv_p = _mem_kv(mem_prompt.reshape(bp * mem_len, d), row(norm_mem[l]), wkv,
                         _pick_tile(bp * mem_len, 256))
    mk_p = mk_p.reshape(bp, mem_len, xa_w)
    mv_p = mv_p.reshape(bp, mem_len, xa_w)
    x2, h3, comb = _mix(x_prompt, y_rw, y_dsa, mk_p, mv_p, W, _pick_tile(tp, 256))
    n_p = bp * tp
    y_prompt = _moe(h3.reshape(n_p, d), comb.reshape(n_p, N_EXPERTS), x2.reshape(n_p, d),
                    w1, b1, w2, b2, gfin, _pick_tile(n_p, 512)).reshape(bp, tp, d)

    xs = jnp.pad(x_sample, ((0, 0), (0, SROWS - ts), (0, 0)))
    n_s = bs * SROWS
    sprw, sq, sk, sv, sqi, skiw, _, svb, _ = _in_proj(xs.reshape(1, n_s, d), gmix, w_pad,
                                                      _pick_tile(n_s, 256))
    seq = lambda a: a.reshape(bs, SROWS, a.shape[-1])
    sprw, sq, sk, sv, sqi, skiw, svb = map(seq, (sprw, sq, sk, sv, sqi, skiw, svb))
    sy_rw, s_s1, s_sh1 = _rwkv(sprw, state_shift[l][:, None, :], state_rwkv[l], P,
                               nb=_pick_tile(bs, 4), tc=SROWS, n_valid=ts)

    qi_rows = sqi[:, :ts].reshape(bs, ts * IDX_H, IDX_D)
    wcol = skiw[:, :ts, IDX_D:IDX_D + IDX_H].reshape(bs, ts * IDX_H, 1)
    ki_new = bf(skiw[:, :, :IDX_D])
    scores = _dsa_s_scores(page_table, cache_idx_k[l], qi_rows, wcol, ki_new, ts)
    width = scores.shape[-1]
    tok = jnp.arange(SROWS, dtype=I32)
    lim = jnp.tile(jnp.where(tok < ts, past + tok, -1), bs).reshape(n_s, 1)
    k_sel = min(TOPK_MAX, (past + ts) // 4)
    madd = _topk_rows(scores.reshape(n_s, width), lim, k_sel, _pick_tile(n_s, 64))
    madd = madd.reshape(bs, SROWS, width)
    head_of_col = jnp.arange(DSA_W, dtype=I32) // DSA_HEAD
    own = head_of_col[None, :] == jnp.arange(DSA_H, dtype=I32)[:, None]
    qblk = jnp.where(own[None, None], sq[:, :ts, None, :], jnp.zeros((), BF16))
    qblk = qblk.reshape(bs, ts * DSA_H, DSA_W)
    rbrow = jnp.tile(rel_bias.T, (ts, 1))
    sy_dsa = _dsa_s_attn(page_table, cache_k[l].reshape(n_pool, PAGE_SIZE, DSA_W),
                         cache_v[l].reshape(n_pool, PAGE_SIZE, DSA_W), qblk, madd, rbrow,
                         bf(sk), svb, ts)
    sx2, sh3, scomb = _mix(xs, sy_rw, sy_dsa, cache_mem_k[l].reshape(bs, mem_len, xa_w),
                           cache_mem_v[l].reshape(bs, mem_len, xa_w), W, SROWS)
    y_s = _moe(sh3.reshape(n_s, d), scomb.reshape(n_s, N_EXPERTS), sx2.reshape(n_s, d),
               w1, b1, w2, b2, gfin, _pick_tile(n_s, 256)).reshape(bs, SROWS, d)

    heads = lambda a, b_, t_: a.reshape(1, b_, t_, DSA_H, DSA_HEAD)
    return (y_prompt, y_s[:, :ts],
            p_s1[None], p_sh1.reshape(1, bp, rw_cols),
            heads(k, bp, tp), heads(v, bp, tp), kiw[:, :, :IDX_D][None],
            mk_p.reshape(1, bp, mem_len, XA_H, XA_HEAD), mv_p.reshape(1, bp, mem_len, XA_H, XA_HEAD),
            s_s1[None], s_sh1.reshape(1, bs, rw_cols),
            heads(sk[:, :ts], bs, ts), heads(sv[:, :ts], bs, ts), skiw[:, :ts, :IDX_D][None])
```

```python
import functools
import math

import numpy as np
import jax
import jax.numpy as jnp
from jax import lax
from jax.experimental import pallas as pl
from jax.experimental.pallas import tpu as pltpu

F32 = jnp.float32
BF16 = jnp.bfloat16
I32 = jnp.int32

LANES = 128
VMEM_LIMIT = 56 * 1024 * 1024

NORM_EPS = 1e-5
RW_EPS = 64e-5
NEG_INF = -1e30
RW_H = 8
RW_HEAD = 64
RW_W = RW_H * RW_HEAD
DSA_H = 8
DSA_HEAD = 64
DSA_W = DSA_H * DSA_HEAD
IDX_H = 8
IDX_D = 64
TOPK_MAX = 256
XA_H = 4
XA_HEAD = 128
N_EXPERTS = 32
TOP_K = 4
SWIGLU_LIMIT = 7.0
SWIGLU_ALPHA = 1.702
NUM_BUCKETS = 32
MAX_DISTANCE = 128
PAGE_SIZE = 128
IDX_SCALE = IDX_H ** -0.5 * IDX_D ** -0.5

QB = 128
CW = 512
ATTN_GROUP = 8
PAGES_PER_STEP = 16

INT_MIN = -2 ** 31
INT_MAX = 2 ** 31 - 1


def _bucket_starts():
    max_exact = NUM_BUCKETS // 2
    d = np.arange(0, 4 * MAX_DISTANCE)
    large = max_exact + (np.log(np.maximum(d, 1).astype(np.float32) / np.float32(max_exact))
                         / np.float32(math.log(MAX_DISTANCE / max_exact))
                         * np.float32(NUM_BUCKETS - max_exact)).astype(np.int32)
    b = np.where(d < max_exact, d, np.minimum(large, NUM_BUCKETS - 1))
    starts = []
    for j in range(NUM_BUCKETS):
        hit = np.nonzero(b == j)[0]
        starts.append(int(hit[0]) if hit.size else None)
    return starts


BUCKET_START = _bucket_starts()


def _cparams(*sem):
    return pltpu.CompilerParams(dimension_semantics=sem, vmem_limit_bytes=VMEM_LIMIT)


def _rms(x, g):
    return x * lax.rsqrt(jnp.mean(x * x, axis=-1, keepdims=True) + NORM_EPS) * g


def _dot(a, b):
    return jnp.dot(a, b, preferred_element_type=F32)


def _dot_nt(a, b):
    return lax.dot_general(a, b, (((1,), (1,)), ((), ())), preferred_element_type=F32)


def _softplus(x):
    return jnp.maximum(x, 0.0) + jnp.log1p(jnp.exp(-jnp.abs(x)))


def _sigmoid(x):
    return 1.0 / (1.0 + jnp.exp(-x))


C_RW = 0
C_Q = 1792
C_K = C_Q + DSA_W
C_V = C_K + DSA_W
C_QI = C_V + DSA_W
C_KIW = C_QI + IDX_H * IDX_D
C_END = C_KIW + LANES


def _in_proj_kernel(x_ref, g_ref, w_ref, prw_ref, q_ref, k_ref, v_ref, qi_ref, kiw_ref,
                    kT_ref, vb_ref, kiT_ref):
    h = _rms(x_ref[0], g_ref[...]).astype(BF16)

    def mm(lo, hi):
        return _dot(h, w_ref[:, lo:hi])

    prw_ref[0] = mm(C_RW, C_Q)
    q_ref[0] = mm(C_Q, C_K).astype(BF16)
    k = mm(C_K, C_V)
    k_ref[0] = k
    kT_ref[0] = k.T.astype(BF16)
    v = mm(C_V, C_QI)
    v_ref[0] = v
    vb_ref[0] = v.astype(BF16)
    qi_ref[0] = mm(C_QI, C_KIW).astype(BF16)
    kiw = mm(C_KIW, C_END)
    kiw_ref[0] = kiw
    kiT_ref[0] = kiw.T[0:IDX_D, :].astype(BF16)


def _in_proj(x, g, w_pad, tm):
    b, t, d = x.shape
    grid = (b, t // tm)
    row = lambda w, dt: jax.ShapeDtypeStruct((b, t, w), dt)
    rspec = lambda w: pl.BlockSpec((1, tm, w), lambda i, j: (i, j, 0))
    tspec = lambda w: pl.BlockSpec((1, w, tm), lambda i, j: (i, 0, j))
    return pl.pallas_call(
        _in_proj_kernel,
        grid=grid,
        in_specs=[rspec(d),
                  pl.BlockSpec((1, d), lambda i, j: (0, 0)),
                  pl.BlockSpec((d, C_END), lambda i, j: (0, 0))],
        out_specs=[rspec(C_Q), rspec(DSA_W), rspec(DSA_W), rspec(DSA_W), rspec(IDX_H * IDX_D),
                   rspec(LANES), tspec(DSA_W), rspec(DSA_W), tspec(IDX_D)],
        out_shape=[row(C_Q, F32), row(DSA_W, BF16), row(DSA_W, F32), row(DSA_W, F32),
                   row(IDX_H * IDX_D, BF16), row(LANES, F32),
                   jax.ShapeDtypeStruct((b, DSA_W, t), BF16), row(DSA_W, BF16),
                   jax.ShapeDtypeStruct((b, IDX_D, t), BF16)],
        compiler_params=_cparams("arbitrary", "arbitrary"),
        name="in_proj",
    )(x, g, w_pad)


def _rwkv_kernel(n_steps, prw_ref, sh0_ref, s0_ref, mu_ref, w0_ref, w2_ref, a0_ref, a2_ref, g2_ref,
                 kkw_ref, kaw_ref, rk_ref, lnw_ref, lnb_ref,
                 y_ref, s1_ref, sh1_ref,
                 st_scr, carry_scr, r_s, w_s, k_s, v_s, kk_s, b_s, y_s, g_s):
    nb, tc = prw_ref.shape[0], prw_ref.shape[1]
    c = pl.program_id(1)

    @pl.when(c == 0)
    def _():
        st_scr[...] = s0_ref[...]
        carry_scr[...] = sh0_ref[...]

    row = lax.broadcasted_iota(I32, (tc, 1), 0)
    lane = lax.broadcasted_iota(I32, (1, LANES), 1)
    for b in range(nb):
        x = prw_ref[b]
        prev = jnp.where(row == 0, carry_scr[b], pltpu.roll(x, 1, 0))
        carry_scr[b] = x[n_steps - 1:n_steps, :]
        ps = x + (prev - x) * mu_ref[...]
        r = ps[:, 0:RW_W]
        k = ps[:, RW_W:2 * RW_W]
        v = ps[:, 2 * RW_W:3 * RW_W]
        la = ps[:, 3 * RW_W:3 * RW_W + LANES]
        gd = ps[:, 3 * RW_W + LANES:3 * RW_W + 2 * LANES]
        z = jnp.where(lane < 64, jnp.tanh(la), la).astype(BF16)
        w_log = -_softplus(-(w0_ref[...] + _dot(z, w2_ref[...]))) - 0.5
        decay = jnp.exp(-jnp.exp(w_log))
        a = _sigmoid(a0_ref[...] + _dot(z, a2_ref[...]))
        g_s[b] = _dot(_sigmoid(gd).astype(BF16), g2_ref[...])
        kk = k * kkw_ref[...]
        k2 = k * (1.0 + (a - 1.0) * kaw_ref[...])
        for h in range(RW_H):
            sl = slice(RW_HEAD * h, RW_HEAD * (h + 1))
            kkh = kk[:, sl]
            nrm = jnp.sqrt(jnp.sum(kkh * kkh, axis=-1, keepdims=True))
            kkh = kkh / jnp.maximum(nrm, 1e-12)
            r_s[b, h] = r[:, sl]
            w_s[b, h] = decay[:, sl]
            k_s[b, h] = k2[:, sl]
            v_s[b, h] = v[:, sl]
            kk_s[b, h] = kkh
            b_s[b, h] = kkh * a[:, sl]

    eye = (lax.broadcasted_iota(I32, (RW_HEAD, RW_HEAD), 0)
           == lax.broadcasted_iota(I32, (RW_HEAD, RW_HEAD), 1))

    def step(t, carry):
        for b in range(nb):
            for h in range(RW_H):
                s = st_scr[b, h]
                tt = pl.ds(t, 1)
                sa = -jnp.sum(s * kk_s[b, h, tt, :], axis=1, keepdims=True)
                vcol = jnp.sum(jnp.where(eye, v_s[b, h, tt, :], 0.0), axis=1, keepdims=True)
                s = s * w_s[b, h, tt, :] + sa * b_s[b, h, tt, :] + vcol * k_s[b, h, tt, :]
                st_scr[b, h] = s
                ycol = jnp.sum(s * r_s[b, h, tt, :], axis=1, keepdims=True)
                y_s[b, h, tt, :] = jnp.sum(jnp.where(eye, ycol, 0.0), axis=0, keepdims=True)
        return carry

    lax.fori_loop(0, n_steps, step, 0)

    for b in range(nb):
        for h in range(RW_H):
            sl = slice(RW_HEAD * h, RW_HEAD * (h + 1))
            y = y_s[b, h]
            mean = jnp.mean(y, axis=-1, keepdims=True)
            var = jnp.mean(jnp.square(y - mean), axis=-1, keepdims=True)
            yn = (y - mean) * lax.rsqrt(var + RW_EPS) * lnw_ref[h:h + 1, :] + lnb_ref[h:h + 1, :]
            rr, kk2, vv = r_s[b, h], k_s[b, h], v_s[b, h]
            bonus = jnp.sum(rr * kk2 * rk_ref[h:h + 1, :], axis=-1, keepdims=True) * vv
            y_ref[b, :, sl] = ((yn + bonus) * g_s[b][:, sl]).astype(BF16)
    s1_ref[...] = st_scr[...]
    sh1_ref[...] = carry_scr[...]


def _rwkv(prw, shift0, s0, P, nb, tc, n_valid):
    b, t, cols = prw.shape
    n_steps = min(tc, n_valid)
    grid = (b // nb, t // tc)
    full = lambda a: pl.BlockSpec(a.shape, lambda i, j: (0,) * a.ndim)
    params = [P['mu'], P['w0'], P['w2'], P['a0'], P['a2'], P['g2'], P['kk'], P['ka'], P['rk'],
              P['lnw'], P['lnb']]
    hs = lambda: pltpu.VMEM((nb, RW_H, tc, RW_HEAD), F32)
    return pl.pallas_call(
        functools.partial(_rwkv_kernel, n_steps),
        grid=grid,
        in_specs=[pl.BlockSpec((nb, tc, cols), lambda i, j: (i, j, 0)),
                  pl.BlockSpec((nb, 1, cols), lambda i, j: (i, 0, 0)),
                  pl.BlockSpec((nb, RW_H, RW_HEAD, RW_HEAD), lambda i, j: (i, 0, 0, 0))]
                 + [full(a) for a in params],
        out_specs=[pl.BlockSpec((nb, tc, RW_W), lambda i, j: (i, j, 0)),
                   pl.BlockSpec((nb, RW_H, RW_HEAD, RW_HEAD), lambda i, j: (i, 0, 0, 0)),
                   pl.BlockSpec((nb, 1, cols), lambda i, j: (i, 0, 0))],
        out_shape=[jax.ShapeDtypeStruct((b, t, RW_W), BF16),
                   jax.ShapeDtypeStruct((b, RW_H, RW_HEAD, RW_HEAD), F32),
                   jax.ShapeDtypeStruct((b, 1, cols), F32)],
        scratch_shapes=[pltpu.VMEM((nb, RW_H, RW_HEAD, RW_HEAD), F32),
                        pltpu.VMEM((nb, 1, cols), F32),
                        hs(), hs(), hs(), hs(), hs(), hs(), hs(),
                        pltpu.VMEM((nb, tc, RW_W), F32)],
        compiler_params=_cparams("arbitrary", "arbitrary"),
        name="rwkv",
    )(prw, shift0, s0, *params)


RC = 64
RC_GROUP = 8


def _split3(x):
    hi = x.astype(BF16)
    r1 = x - hi.astype(F32)
    mid = r1.astype(BF16)
    lo = (r1 - mid.astype(F32)).astype(BF16)
    return hi, mid, lo


def _mm(a, b, passes):
    if passes == 1:
        return _dot(a.astype(BF16), b.astype(BF16))
    ah, al = _split_bf16(a)
    bh, bl = _split_bf16(b)
    return _dot(ah, bh) + _dot(ah, bl) + _dot(al, bh)


def _mm_nt(a, b, passes):
    if passes == 1:
        return _dot_nt(a.astype(BF16), b.astype(BF16))
    ah, al = _split_bf16(a)
    bh, bl = _split_bf16(b)
    return _dot_nt(ah, bh) + _dot_nt(ah, bl) + _dot_nt(al, bh)


def _exact_left(m_bf16, x):
    hi, mid, lo = _split3(x)
    return _dot(m_bf16, hi) + _dot(m_bf16, mid) + _dot(m_bf16, lo)


def _exact_right(x, m_bf16):
    hi, mid, lo = _split3(x)
    return _dot(hi, m_bf16) + _dot(mid, m_bf16) + _dot(lo, m_bf16)


def _rwkv_chunk_kernel(passes, prw_ref, sh0_ref, s0_ref, mu_ref, w0_ref, w2_ref, a0_ref, a2_ref,
                       g2_ref, kkw_ref, kaw_ref, rk_ref, lnw_ref, lnb_ref, ltri_ref, lones_ref,
                       hones_ref, y_ref, s1_ref, sh1_ref,
                       st_scr, carry_scr, rh_s, kkh_s, bt_s, kt_s, bg_s, kg_s, v_s, gc_s, y_s,
                       bonus_s, g_s):
    nb, tc = prw_ref.shape[0], prw_ref.shape[1]
    nh = nb * RW_H
    c = pl.program_id(1)

    @pl.when(c == 0)
    def _():
        for b in range(nb):
            for h in range(RW_H):
                st_scr[b * RW_H + h] = s0_ref[b, h]
        carry_scr[...] = sh0_ref[...]

    row = lax.broadcasted_iota(I32, (tc, 1), 0)
    lane = lax.broadcasted_iota(I32, (1, LANES), 1)
    hones = hones_ref[...]
    for b in range(nb):
        x = prw_ref[b]
        prev = jnp.where(row == 0, carry_scr[b], pltpu.roll(x, 1, 0))
        carry_scr[b] = x[tc - 1:tc, :]
        ps = x + (prev - x) * mu_ref[...]
        r = ps[:, 0:RW_W]
        k = ps[:, RW_W:2 * RW_W]
        v = ps[:, 2 * RW_W:3 * RW_W]
        la = ps[:, 3 * RW_W:3 * RW_W + LANES]
        gd = ps[:, 3 * RW_W + LANES:3 * RW_W + 2 * LANES]
        z = jnp.where(lane < 64, jnp.tanh(la), la).astype(BF16)
        w_log = -_softplus(-(w0_ref[...] + _dot(z, w2_ref[...]))) - 0.5
        lw = -jnp.exp(w_log)
        a = _sigmoid(a0_ref[...] + _dot(z, a2_ref[...]))
        g_s[b] = _dot(_sigmoid(gd).astype(BF16), g2_ref[...])
        kk = k * kkw_ref[...]
        kk = kk / jnp.maximum(jnp.sqrt(_exact_right(kk * kk, hones)), 1e-12)
        k2 = k * (1.0 + (a - 1.0) * kaw_ref[...])
        bb = kk * a
        bonus_s[b] = _exact_right(r * k2 * rk_ref[...], hones) * v
        gsum = _exact_left(ltri_ref[...], lw)
        gend = _exact_left(lones_ref[...], lw)
        e_neg = jnp.exp(-gsum)
        e_end = jnp.exp(gend - gsum)
        cols = {'rh': r * jnp.exp(gsum), 'kkh': kk * jnp.exp(gsum - lw), 'bt': bb * e_neg,
                'kt': k2 * e_neg, 'bg': bb * e_end, 'kg': k2 * e_end, 'v': v, 'gc': jnp.exp(gend)}
        dst = {'rh': rh_s, 'kkh': kkh_s, 'bt': bt_s, 'kt': kt_s, 'bg': bg_s, 'kg': kg_s, 'v': v_s,
               'gc': gc_s}
        for name, val in cols.items():
            for h in range(RW_H):
                dst[name][b * RW_H + h] = val[:, RW_HEAD * h:RW_HEAD * (h + 1)]

    ri = lax.broadcasted_iota(I32, (RC, RC), 0)
    ci = lax.broadcasted_iota(I32, (RC, RC), 1)
    strict = ri > ci
    incl = ri >= ci
    eye = ri == ci
    eye_f = jnp.where(eye, 1.0, 0.0)
    n_double = int(math.log2(RC)) - 1

    def head_group(heads, sub):
        base = pl.multiple_of(sub * RC, RC)
        rows = pl.ds(base, RC)
        mm = lambda a, b: _mm(a, b, passes)
        each = lambda f: [f(i) for i in range(len(heads))]
        rh = [rh_s[hd, rows, :] for hd in heads]
        kkh = [kkh_s[hd, rows, :] for hd in heads]
        vv = [v_s[hd, rows, :] for hd in heads]
        gram = each(lambda i: _mm_nt(
            jnp.concatenate([kkh[i], rh[i]], axis=0),
            jnp.concatenate([bt_s[heads[i], rows, :], kt_s[heads[i], rows, :]], axis=0), passes))
        mb = each(lambda i: jnp.where(strict, gram[i][0:RC, 0:RC], 0.0))
        mkv = each(lambda i: mm(jnp.where(strict, gram[i][0:RC, RC:2 * RC], 0.0), vv[i]))
        nkv = each(lambda i: mm(jnp.where(incl, gram[i][RC:2 * RC, RC:2 * RC], 0.0), vv[i]))
        nb_ = each(lambda i: jnp.where(incl, gram[i][RC:2 * RC, 0:RC], 0.0))
        inv = each(lambda i: eye_f - mb[i])
        q = mb
        for _ in range(n_double):
            q = each(lambda i: mm(q[i], q[i]))
            inv = each(lambda i: inv[i] + mm(inv[i], q[i]))
        u = each(lambda i: -mm(inv[i], jnp.concatenate([kkh[i], mkv[i]], axis=1)))
        yy = each(lambda i: mm(nb_[i], u[i]))
        zz = each(lambda i: mm(u[i].T, bg_s[heads[i], rows, :]))
        vk = each(lambda i: mm(vv[i].T, kg_s[heads[i], rows, :]))
        s = [st_scr[hd] for hd in heads]
        ys = each(lambda i: _mm_nt(rh[i] + yy[i][:, 0:RC], s[i], passes))
        z1 = each(lambda i: jnp.where(eye, gc_s[heads[i], pl.ds(base, 1), :], 0.0)
                  + zz[i][0:RW_HEAD])
        sn = each(lambda i: mm(s[i], z1[i]))
        for i, hd in enumerate(heads):
            y_s[hd, rows, :] = ys[i] + yy[i][:, RC:2 * RC] + nkv[i]
            st_scr[hd] = sn[i] + zz[i][RW_HEAD:2 * RW_HEAD] + vk[i]

    def chunk_step(sub, carry):
        for g0 in range(0, nh, RC_GROUP):
            head_group(list(range(g0, min(g0 + RC_GROUP, nh))), sub)
        return carry

    lax.fori_loop(0, tc // RC, chunk_step, 0)

    for b in range(nb):
        for h in range(RW_H):
            sl = slice(RW_HEAD * h, RW_HEAD * (h + 1))
            y = y_s[b * RW_H + h]
            mean = jnp.mean(y, axis=-1, keepdims=True)
            var = jnp.mean(jnp.square(y - mean), axis=-1, keepdims=True)
            yn = (y - mean) * lax.rsqrt(var + RW_EPS) * lnw_ref[h:h + 1, :] + lnb_ref[h:h + 1, :]
            y_ref[b, :, sl] = ((yn + bonus_s[b, :, sl]) * g_s[b, :, sl]).astype(BF16)
            s1_ref[b, h] = st_scr[b * RW_H + h]
    sh1_ref[...] = carry_scr[...]


def _rwkv_chunk(prw, shift0, s0, P, nb, tc, passes):
    b, t, cols = prw.shape
    tok = np.arange(tc)
    same = (tok[:, None] // RC) == (tok[None, :] // RC)
    ltri = jnp.asarray(same & (tok[:, None] >= tok[None, :]), BF16)
    lones = jnp.asarray(same, BF16)
    col = np.arange(RW_W)
    hones = jnp.asarray((col[:, None] // RW_HEAD) == (col[None, :] // RW_HEAD), BF16)
    full = lambda a: pl.BlockSpec(a.shape, lambda i, j: (0,) * a.ndim)
    params = [P['mu'], P['w0'], P['w2'], P['a0'], P['a2'], P['g2'], P['kk'], P['ka'],
              P['rk'].reshape(1, RW_W), P['lnw'], P['lnb'], ltri, lones, hones]
    nh = nb * RW_H
    hs = lambda: pltpu.VMEM((nh, tc, RW_HEAD), F32)
    return pl.pallas_call(
        functools.partial(_rwkv_chunk_kernel, passes),
        grid=(b // nb, t // tc),
        in_specs=[pl.BlockSpec((nb, tc, cols), lambda i, j: (i, j, 0)),
                  pl.BlockSpec((nb, 1, cols), lambda i, j: (i, 0, 0)),
                  pl.BlockSpec((nb, RW_H, RW_HEAD, RW_HEAD), lambda i, j: (i, 0, 0, 0))]
                 + [full(a) for a in params],
        out_specs=[pl.BlockSpec((nb, tc, RW_W), lambda i, j: (i, j, 0)),
                   pl.BlockSpec((nb, RW_H, RW_HEAD, RW_HEAD), lambda i, j: (i, 0, 0, 0)),
                   pl.BlockSpec((nb, 1, cols), lambda i, j: (i, 0, 0))],
        out_shape=[jax.ShapeDtypeStruct((b, t, RW_W), BF16),
                   jax.ShapeDtypeStruct((b, RW_H, RW_HEAD, RW_HEAD), F32),
                   jax.ShapeDtypeStruct((b, 1, cols), F32)],
        scratch_shapes=[pltpu.VMEM((nh, RW_HEAD, RW_HEAD), F32),
                        pltpu.VMEM((nb, 1, cols), F32),
                        hs(), hs(), hs(), hs(), hs(), hs(), hs(), hs(), hs(),
                        pltpu.VMEM((nb, tc, RW_W), F32), pltpu.VMEM((nb, tc, RW_W), F32)],
        compiler_params=_cparams("arbitrary", "arbitrary"),
        name="rwkv_chunk",
    )(prw, shift0, s0, *params)


def _pattern_to_f32(u):
    key = u ^ INT_MIN
    return lax.bitcast_convert_type(jnp.where(key >= 0, key, key ^ 0x7FFFFFFF), F32)


def _topk_mask(sc_ref, madd_ref, p_scr, rows, nch, k_sel, lim, idx_bits):
    ntile = CW // LANES
    lane = lax.broadcasted_iota(I32, (rows, LANES), 1)

    def count(pred):
        def body(c, acc):
            off = pl.multiple_of(c * CW, CW)
            x = sc_ref[:, pl.ds(off, CW)]
            for t in range(ntile):
                acc = acc + pred(x[:, LANES * t:LANES * (t + 1)], off + LANES * t)
            return acc
        acc = lax.fori_loop(0, nch, body, jnp.zeros((rows, LANES), I32))
        return jnp.sum(acc, axis=1, keepdims=True)

    def bcast(col):
        return jnp.broadcast_to(col, (rows, LANES))

    def searching(state):
        it, alive, _, _ = state
        return (it < 32) & (alive > 0)

    def value_bit(state):
        it, _, tau, cnt_tau = state
        cand = tau | jnp.left_shift(jnp.int32(1), 31 - it)
        cb = bcast(_pattern_to_f32(cand))
        cnt = count(lambda x, off: jnp.where(x >= cb, 1, 0))
        take = cnt >= k_sel
        tau = jnp.where(take, cand, tau)
        cnt_tau = jnp.where(take, cnt, cnt_tau)
        settled = jnp.where(cnt_tau == k_sel, 1, jnp.where(lim < 0, 1, 0))
        return it + 1, 1 - jnp.min(settled), tau, cnt_tau

    _, _, tau, _ = lax.while_loop(
        searching, value_bit,
        (jnp.int32(0), jnp.int32(1), jnp.zeros((rows, 1), I32),
         jnp.zeros((rows, 1), I32) + nch * CW))
    thr = _pattern_to_f32(tau)
    thr_b = bcast(thr)
    cnt_gt = count(lambda x, off: jnp.where(x > thr_b, 1, 0))
    cnt_ge = count(lambda x, off: jnp.where(x >= thr_b, 1, 0))
    need = k_sel - cnt_gt
    tie = jnp.where(cnt_ge > k_sel, jnp.where(thr > NEG_INF, 1, 0), 0)

    p_scr[...] = jnp.full((rows, LANES), INT_MAX, I32)

    @pl.when(jnp.max(tie) > 0)
    def _():
        def index_bit(it, p):
            cand = p | jnp.left_shift(jnp.int32(1), idx_bits - 1 - it)
            cb = bcast(cand)
            cnt = count(lambda x, off: jnp.where(x == thr_b, jnp.where(off + lane < cb, 1, 0), 0))
            return jnp.where(cnt < need, cand, p)
        p = lax.fori_loop(0, idx_bits, index_bit, jnp.zeros((rows, 1), I32))
        p_scr[...] = bcast(p)

    p_b = p_scr[...]
    lim_b = bcast(lim)

    def fin(c, carry):
        off = pl.multiple_of(c * CW, CW)
        x = sc_ref[:, pl.ds(off, CW)]
        for t in range(ntile):
            xt = x[:, LANES * t:LANES * (t + 1)]
            idx = off + LANES * t + lane
            sel = jnp.where(xt > thr_b, 0.0,
                            jnp.where(xt == thr_b, jnp.where(idx <= p_b, 0.0, NEG_INF), NEG_INF))
            madd_ref[:, pl.ds(pl.multiple_of(off + LANES * t, LANES), LANES)] = (
                jnp.where(idx <= lim_b, sel, NEG_INF))
        return carry

    lax.fori_loop(0, nch, fin, 0)


def _bias_table(rb_ref, h, d):
    bias = jnp.full(d.shape, rb_ref[0, h], F32)
    for j in range(1, NUM_BUCKETS):
        if BUCKET_START[j] is not None:
            bias = jnp.where(d >= BUCKET_START[j], rb_ref[j, h], bias)
    return bias


def _dsa_prompt_kernel(k_sel, idx_bits, rb_ref, q_ref, qi_ref, kiw_ref, kT_ref, vb_ref, kiT_ref,
                       o_ref, sc_scr, madd_scr, p_scr, btab_scr):
    i = pl.program_id(1)
    nch = ((i + 1) * QB + CW - 1) // CW

    @pl.when((pl.program_id(0) == 0) & (i == 0))
    def _():
        d = (lax.broadcasted_iota(I32, (QB, 2 * QB), 0)
             - lax.broadcasted_iota(I32, (QB, 2 * QB), 1) + QB)
        for h in range(DSA_H):
            btab_scr[h] = _bias_table(rb_ref, h, d) - rb_ref[NUM_BUCKETS - 1, h]

    tq = i * QB + lax.broadcasted_iota(I32, (QB, 1), 0)
    qi = qi_ref[0]
    wi = kiw_ref[0][:, IDX_D:IDX_D + IDX_H]
    qi_all = jnp.concatenate([qi[:, IDX_D * h:IDX_D * (h + 1)] for h in range(IDX_H)], axis=0)
    wi_h = [wi[:, h:h + 1] for h in range(IDX_H)]

    def score_chunk(c, carry):
        off = pl.multiple_of(c * CW, CW)
        s = _dot(qi_all, kiT_ref[0, :, pl.ds(off, CW)])
        acc = jnp.zeros((QB, CW), F32)
        for h in range(IDX_H):
            acc = acc + jnp.maximum(s[QB * h:QB * (h + 1)], 0.0) * wi_h[h]
        spos = off + lax.broadcasted_iota(I32, (1, CW), 1)
        sc_scr[:, pl.ds(off, CW)] = jnp.where(spos <= tq, acc * IDX_SCALE + 0.0, NEG_INF)
        return carry

    lax.fori_loop(0, nch, score_chunk, 0)

    _topk_mask(sc_scr, madd_scr, p_scr, QB, nch, k_sel, tq, idx_bits)

    q = q_ref[0] * (DSA_HEAD ** -0.5)
    tiles = CW // QB
    hsl = [slice(DSA_HEAD * h, DSA_HEAD * (h + 1)) for h in range(DSA_H)]
    qh = [q[:, sl] for sl in hsl]

    def attend(heads, near, c, carry):
        n = len(heads)
        off = pl.multiple_of(c * CW, CW)
        madd = madd_scr[:, pl.ds(off, CW)]
        s = [_dot(qh[h], kT_ref[0, hsl[h], pl.ds(off, CW)]) + madd for h in heads]
        if near:
            for j, h in enumerate(heads):
                parts = []
                for t in range(tiles):
                    delta = i - (c * tiles + t)
                    parts.append(jnp.where(delta == 0, btab_scr[h, :, QB:2 * QB],
                                           jnp.where(delta == 1, btab_scr[h, :, 0:QB], 0.0)))
                s[j] = s[j] + jnp.concatenate(parts, axis=1)
        m_new = [jnp.maximum(carry[j][0], jnp.max(s[j], axis=1, keepdims=True)) for j in range(n)]
        p = [jnp.exp(s[j] - m_new[j]) for j in range(n)]
        pv = [_dot(p[j].astype(BF16), vb_ref[0, pl.ds(off, CW), hsl[h]])
              for j, h in enumerate(heads)]
        out = []
        for j in range(n):
            m, l, acc = carry[j]
            alpha = jnp.exp(m - m_new[j])
            out.append((m_new[j], l * alpha + jnp.sum(p[j], axis=1, keepdims=True),
                        acc * alpha + pv[j]))
        return tuple(out)

    n_far = jnp.maximum(i - 1, 0) // tiles
    for g0 in range(0, DSA_H, ATTN_GROUP):
        heads = list(range(g0, g0 + ATTN_GROUP))
        init = tuple((jnp.full((QB, 1), NEG_INF, F32), jnp.zeros((QB, 1), F32),
                      jnp.zeros((QB, DSA_HEAD), F32)) for _ in heads)
        carry = lax.fori_loop(0, n_far, functools.partial(attend, heads, False), init)
        carry = lax.fori_loop(n_far, nch, functools.partial(attend, heads, True), carry)
        for j, h in enumerate(heads):
            _, l, acc = carry[j]
            o_ref[0, :, hsl[h]] = (acc / l).astype(BF16)


def _dsa_prompt(rel_bias, q, qi, kiw, kT, vb, kiT):
    b, t, _ = q.shape
    k_sel = min(TOPK_MAX, t // 4)
    idx_bits = max(1, int(math.ceil(math.log2(t))))
    blk = lambda w: pl.BlockSpec((1, QB, w), lambda bi, i: (bi, i, 0))
    return pl.pallas_call(
        functools.partial(_dsa_prompt_kernel, k_sel, idx_bits),
        grid=(b, t // QB),
        in_specs=[pl.BlockSpec(memory_space=pltpu.SMEM),
                  blk(DSA_W), blk(IDX_H * IDX_D), blk(LANES),
                  pl.BlockSpec((1, DSA_W, t), lambda bi, i: (bi, 0, 0)),
                  pl.BlockSpec((1, t, DSA_W), lambda bi, i: (bi, 0, 0)),
                  pl.BlockSpec((1, IDX_D, t), lambda bi, i: (bi, 0, 0))],
        out_specs=blk(DSA_W),
        out_shape=jax.ShapeDtypeStruct((b, t, DSA_W), BF16),
        scratch_shapes=[pltpu.VMEM((QB, t), F32), pltpu.VMEM((QB, t), F32),
                        pltpu.VMEM((QB, LANES), I32),
                        pltpu.VMEM((DSA_H, QB, 2 * QB), F32)],
        compiler_params=_cparams("arbitrary", "arbitrary"),
        name="dsa_prompt",
    )(rel_bias, q, qi, kiw, kT, vb, kiT)


SROWS = 8
SCHUNK = PAGES_PER_STEP * PAGE_SIZE


def _dsa_s_scores_kernel(n_tok, n_past_chunks, pt_ref, qi_ref, wcol_ref, kinew_ref, *rest):
    pages, out_ref = rest[:PAGES_PER_STEP], rest[PAGES_PER_STEP]
    c = pl.program_id(1)
    qi = qi_ref[0]
    wcol = wcol_ref[0]

    def head_sum(s):
        s = jnp.maximum(s, 0.0) * wcol
        return jnp.sum(s.reshape(n_tok, IDX_H, s.shape[-1]), axis=1) * IDX_SCALE + 0.0

    @pl.when(c < n_past_chunks)
    def _():
        for p in range(PAGES_PER_STEP):
            s = _dot_nt(qi, pages[p][0].astype(BF16))
            out_ref[0, 0:n_tok, PAGE_SIZE * p:PAGE_SIZE * (p + 1)] = head_sum(s)
        out_ref[0, n_tok:SROWS, :] = jnp.full((SROWS - n_tok, SCHUNK), NEG_INF, F32)

    @pl.when(c == n_past_chunks)
    def _():
        s = head_sum(_dot_nt(qi, kinew_ref[0]))
        tr = lax.broadcasted_iota(I32, (n_tok, SROWS), 0)
        tc = lax.broadcasted_iota(I32, (n_tok, SROWS), 1)
        out_ref[...] = jnp.full((1, SROWS, SCHUNK), NEG_INF, F32)
        out_ref[0, 0:n_tok, 0:SROWS] = jnp.where(tc <= tr, s, NEG_INF)


def _dsa_s_scores(page_table, cache_ki, qi_rows, wcol, ki_new, n_tok):
    b, n_pages = page_table.shape
    npc = n_pages // PAGES_PER_STEP
    page_specs = [
        pl.BlockSpec((1, PAGE_SIZE, IDX_D),
                     lambda bi, c, pt, j=j: (pt[bi, jnp.minimum(c, npc - 1) * PAGES_PER_STEP + j], 0, 0))
        for j in range(PAGES_PER_STEP)]
    rows = qi_rows.shape[1]
    grid_spec = pltpu.PrefetchScalarGridSpec(
        num_scalar_prefetch=1,
        grid=(b, npc + 1),
        in_specs=[pl.BlockSpec((1, rows, IDX_D), lambda bi, c, pt: (bi, 0, 0)),
                  pl.BlockSpec((1, rows, 1), lambda bi, c, pt: (bi, 0, 0)),
                  pl.BlockSpec((1, SROWS, IDX_D), lambda bi, c, pt: (bi, 0, 0))] + page_specs,
        out_specs=pl.BlockSpec((1, SROWS, SCHUNK), lambda bi, c, pt: (bi, 0, c)),
    )
    return pl.pallas_call(
        functools.partial(_dsa_s_scores_kernel, n_tok, npc),
        grid_spec=grid_spec,
        out_shape=jax.ShapeDtypeStruct((b, SROWS, (npc + 1) * SCHUNK), F32),
        compiler_params=_cparams("arbitrary", "arbitrary"),
        name="dsa_s_scores",
    )(page_table, qi_rows, wcol, ki_new, *([cache_ki] * PAGES_PER_STEP))


def _topk_rows_kernel(k_sel, idx_bits, sc_ref, lim_ref, madd_ref, p_scr):
    rows, width = sc_ref.shape
    _topk_mask(sc_ref, madd_ref, p_scr, rows, width // CW, k_sel, lim_ref[...], idx_bits)


def _topk_rows(scores, lim, k_sel, rows_per_step):
    n, width = scores.shape
    idx_bits = max(1, int(math.ceil(math.log2(width))))
    return pl.pallas_call(
        functools.partial(_topk_rows_kernel, k_sel, idx_bits),
        grid=(n // rows_per_step,),
        in_specs=[pl.BlockSpec((rows_per_step, width), lambda i: (i, 0)),
                  pl.BlockSpec((rows_per_step, 1), lambda i: (i, 0))],
        out_specs=pl.BlockSpec((rows_per_step, width), lambda i: (i, 0)),
        out_shape=jax.ShapeDtypeStruct((n, width), F32),
        scratch_shapes=[pltpu.VMEM((rows_per_step, LANES), I32)],
        compiler_params=_cparams("arbitrary"),
        name="topk_rows",
    )(scores, lim)


def _dsa_s_attn_kernel(n_past_chunks, past, pt_ref, rb_ref, q_ref, madd_ref, knew_ref, vnew_ref,
                       *rest):
    kp = rest[:PAGES_PER_STEP]
    vp = rest[PAGES_PER_STEP:2 * PAGES_PER_STEP]
    o_ref, m_scr, l_scr, acc_scr = rest[2 * PAGES_PER_STEP:]
    c = pl.program_id(1)
    q = q_ref[0] * (DSA_HEAD ** -0.5)
    hsl = [slice(DSA_HEAD * h, DSA_HEAD * (h + 1)) for h in range(DSA_H)]
    heads = range(DSA_H)

    @pl.when(c == 0)
    def _():
        m_scr[...] = jnp.full(m_scr.shape, NEG_INF, F32)
        l_scr[...] = jnp.zeros(l_scr.shape, F32)
        acc_scr[...] = jnp.zeros(acc_scr.shape, F32)

    def near_bias(h, width, key0):
        tok = lax.broadcasted_iota(I32, (SROWS, width), 0)
        d = past + tok - (key0 + lax.broadcasted_iota(I32, (SROWS, width), 1))
        return _bias_table(rb_ref, h, d) - rb_ref[NUM_BUCKETS - 1, h]

    def update(s, values):
        m_new = [jnp.maximum(m_scr[h], jnp.max(s[h], axis=1, keepdims=True)) for h in heads]
        p = [jnp.exp(s[h] - m_new[h]) for h in heads]
        pv = [_dot(p[h].astype(BF16), values[h]) for h in heads]
        for h in heads:
            alpha = jnp.exp(m_scr[h] - m_new[h])
            l_scr[h] = l_scr[h] * alpha + jnp.sum(p[h], axis=1, keepdims=True)
            acc_scr[h] = acc_scr[h] * alpha + pv[h]
            m_scr[h] = m_new[h]

    def head_rows(pages, h):
        return jnp.concatenate(
            [pg[0, pl.ds(h, PAGE_SIZE, stride=DSA_H), :] for pg in pages], axis=0).astype(BF16)

    @pl.when(c < n_past_chunks)
    def _():
        madd = madd_ref[0]
        s = [_dot_nt(q[:, hsl[h]], head_rows(kp, h)) + madd for h in heads]
        is_last = c == n_past_chunks - 1
        tail = SCHUNK - PAGE_SIZE
        for h in heads:
            nb_ = jnp.where(is_last, near_bias(h, PAGE_SIZE, past - PAGE_SIZE), 0.0)
            s[h] = jnp.concatenate([s[h][:, 0:tail], s[h][:, tail:SCHUNK] + nb_], axis=1)
        update(s, [head_rows(vp, h) for h in heads])

    @pl.when(c == n_past_chunks)
    def _():
        knew, vnew = knew_ref[0], vnew_ref[0]
        madd = madd_ref[0, :, 0:SROWS]
        s = [_dot_nt(q[:, hsl[h]], knew[:, hsl[h]]) + madd + near_bias(h, SROWS, past)
             for h in heads]
        update(s, [vnew[:, hsl[h]] for h in heads])
        for h in heads:
            o_ref[0, :, hsl[h]] = (acc_scr[h] / l_scr[h]).astype(o_ref.dtype)


def _dsa_s_attn(page_table, rel_bias, cache_k, cache_v, q, madd, k_new, v_new):
    b, n_pages = page_table.shape
    npc = n_pages // PAGES_PER_STEP
    assert BUCKET_START[NUM_BUCKETS - 1] <= PAGE_SIZE

    def page_spec(j):
        return pl.BlockSpec(
            (1, PAGE_SIZE * DSA_H, DSA_HEAD),
            lambda bi, c, pt, j=j: (pt[bi, jnp.minimum(c, npc - 1) * PAGES_PER_STEP + j], 0, 0))

    page_specs = [page_spec(j) for j in range(PAGES_PER_STEP)]
    grid_spec = pltpu.PrefetchScalarGridSpec(
        num_scalar_prefetch=1,
        grid=(b, npc + 1),
        in_specs=[pl.BlockSpec(memory_space=pltpu.SMEM),
                  pl.BlockSpec((1, SROWS, DSA_W), lambda bi, c, pt: (bi, 0, 0)),
                  pl.BlockSpec((1, SROWS, SCHUNK), lambda bi, c, pt: (bi, 0, c)),
                  pl.BlockSpec((1, SROWS, DSA_W), lambda bi, c, pt: (bi, 0, 0)),
                  pl.BlockSpec((1, SROWS, DSA_W), lambda bi, c, pt: (bi, 0, 0))]
                 + page_specs + page_specs,
        out_specs=pl.BlockSpec((1, SROWS, DSA_W), lambda bi, c, pt: (bi, 0, 0)),
        scratch_shapes=[pltpu.VMEM((DSA_H, SROWS, 1), F32), pltpu.VMEM((DSA_H, SROWS, 1), F32),
                        pltpu.VMEM((DSA_H, SROWS, DSA_HEAD), F32)],
    )
    return pl.pallas_call(
        functools.partial(_dsa_s_attn_kernel, npc, n_pages * PAGE_SIZE),
        grid_spec=grid_spec,
        out_shape=jax.ShapeDtypeStruct((b, SROWS, DSA_W), BF16),
        compiler_params=_cparams("arbitrary", "arbitrary"),
        name="dsa_s_attn",
    )(page_table, rel_bias, q, madd, k_new, v_new,
      *([cache_k] * PAGES_PER_STEP), *([cache_v] * PAGES_PER_STEP))


def _mem_kv_kernel(x_ref, g_ref, w_ref, k_ref, v_ref):
    h = _rms(x_ref[...], g_ref[...]).astype(BF16)
    width = k_ref.shape[-1]
    k_ref[...] = _dot(h, w_ref[:, 0:width])
    v_ref[...] = _dot(h, w_ref[:, width:2 * width])


def _mem_kv(mem, g, wkv, tm):
    n, d = mem.shape
    width = wkv.shape[1] // 2
    return pl.pallas_call(
        _mem_kv_kernel,
        grid=(n // tm,),
        in_specs=[pl.BlockSpec((tm, d), lambda i: (i, 0)),
                  pl.BlockSpec((1, d), lambda i: (0, 0)),
                  pl.BlockSpec(wkv.shape, lambda i: (0, 0))],
        out_specs=[pl.BlockSpec((tm, width), lambda i: (i, 0))] * 2,
        out_shape=[jax.ShapeDtypeStruct((n, width), F32)] * 2,
        compiler_params=_cparams("arbitrary"),
        name="mem_kv",
    )(mem, g, wkv)


def _split_bf16(x):
    hi = x.astype(BF16)
    return hi, (x - hi.astype(F32)).astype(BF16)


def _mix_kernel(x_ref, yrw_ref, ydsa_ref, mk_ref, mv_ref, woa_ref, wob_ref, gx_ref, wq_ref, wo_ref,
                gf_ref, rwh_ref, rwl_ref, rb_ref, x2_ref, h3_ref, comb_ref):
    x1 = x_ref[0] + _dot(yrw_ref[0], woa_ref[...]) + _dot(ydsa_ref[0], wob_ref[...])
    h2 = _rms(x1, gx_ref[...]).astype(BF16)
    qx = _dot(h2, wq_ref[...])
    mk = mk_ref[0].astype(BF16)
    mv = mv_ref[0].astype(BF16)
    heads = []
    for h in range(XA_H):
        sl = slice(XA_HEAD * h, XA_HEAD * (h + 1))
        lg = _dot_nt(qx[:, sl].astype(BF16), mk[:, sl]) * (XA_HEAD ** -0.5)
        p = jnp.exp(lg - jnp.max(lg, axis=1, keepdims=True))
        l = jnp.sum(p, axis=1, keepdims=True)
        heads.append(_dot(p.astype(BF16), mv[:, sl]) / l)
    o = jnp.concatenate(heads, axis=1).astype(BF16)
    x2 = x1 + _dot(o, wo_ref[...])
    x2_ref[0] = x2
    h3 = _rms(x2, gf_ref[...])
    h3_ref[0] = h3.astype(BF16)

    hh, hl = _split_bf16(h3)
    logits = (_dot(hh, rwh_ref[...]) + _dot(hh, rwl_ref[...]) + _dot(hl, rwh_ref[...])
              + rb_ref[...])
    lane = lax.broadcasted_iota(I32, logits.shape, 1)
    work = logits
    vals, idxs = [], []
    for _ in range(TOP_K):
        mx = jnp.max(work, axis=1, keepdims=True)
        ix = jnp.min(jnp.where(work == mx, lane, N_EXPERTS), axis=1, keepdims=True)
        vals.append(mx)
        idxs.append(ix)
        work = jnp.where(lane == ix, -jnp.inf, work)
    es = [jnp.exp(v - vals[0]) for v in vals]
    den = es[0] + es[1] + es[2] + es[3]
    comb = jnp.zeros(logits.shape, F32)
    for e, ix in zip(es, idxs):
        comb = comb + jnp.where(lane == ix, e / den, 0.0)
    comb_ref[0] = comb


def _mix(x, yrw, ydsa, mk, mv, W, tm):
    b, t, d = x.shape
    mem_len, xa_w = mk.shape[1], mk.shape[2]
    full = lambda a: pl.BlockSpec(a.shape, lambda i, j: (0,) * a.ndim)
    rspec = lambda w: pl.BlockSpec((1, tm, w), lambda i, j: (i, j, 0))
    mspec = pl.BlockSpec((1, mem_len, xa_w), lambda i, j: (i, 0, 0))
    params = [W['woa'], W['wob'], W['gx'], W['wq'], W['wo'], W['gf'], W['rwh'], W['rwl'], W['rb']]
    return pl.pallas_call(
        _mix_kernel,
        grid=(b, t // tm),
        in_specs=[rspec(d), rspec(RW_W), rspec(DSA_W), mspec, mspec] + [full(a) for a in params],
        out_specs=[rspec(d), rspec(d), rspec(N_EXPERTS)],
        out_shape=[jax.ShapeDtypeStruct((b, t, d), F32), jax.ShapeDtypeStruct((b, t, d), BF16),
                   jax.ShapeDtypeStruct((b, t, N_EXPERTS), F32)],
        compiler_params=_cparams("arbitrary", "arbitrary"),
        name="mix",
    )(x, yrw, ydsa, mk, mv, *params)


def _moe_kernel(h_ref, comb_ref, x_ref, w1_ref, b1_ref, w2_ref, b2_ref, gfin_ref, o_ref, acc_scr):
    e = pl.program_id(1)
    d_ff = w2_ref.shape[1]

    @pl.when(e == 0)
    def _():
        acc_scr[...] = jnp.zeros(acc_scr.shape, F32)

    h = h_ref[...]
    gate = jnp.minimum(_dot(h, w1_ref[0, :, 0:d_ff]) + b1_ref[0, :, 0:d_ff], SWIGLU_LIMIT)
    up = jnp.clip(_dot(h, w1_ref[0, :, d_ff:2 * d_ff]) + b1_ref[0, :, d_ff:2 * d_ff],
                  -SWIGLU_LIMIT, SWIGLU_LIMIT)
    glu = gate * _sigmoid(SWIGLU_ALPHA * gate)
    out = _dot(((up + 1.0) * glu).astype(BF16), w2_ref[0]) + b2_ref[0]
    comb = comb_ref[...]
    lane = lax.broadcasted_iota(I32, comb.shape, 1)
    wcol = jnp.sum(jnp.where(lane == e, comb, 0.0), axis=1, keepdims=True)
    acc_scr[...] += wcol * out

    @pl.when(e == pl.num_programs(1) - 1)
    def _():
        o_ref[...] = _rms(x_ref[...] + acc_scr[...], gfin_ref[...])


def _moe(h3, comb, x2, w1, b1, w2, b2, gfin, tm):
    n, d = h3.shape
    n_exp, _, two_ff = w1.shape
    d_ff = two_ff // 2
    return pl.pallas_call(
        _moe_kernel,
        grid=(n // tm, n_exp),
        in_specs=[pl.BlockSpec((tm, d), lambda i, e: (i, 0)),
                  pl.BlockSpec((tm, n_exp), lambda i, e: (i, 0)),
                  pl.BlockSpec((tm, d), lambda i, e: (i, 0)),
                  pl.BlockSpec((1, d, two_ff), lambda i, e: (e, 0, 0)),
                  pl.BlockSpec((1, 1, two_ff), lambda i, e: (e, 0, 0)),
                  pl.BlockSpec((1, d_ff, d), lambda i, e: (e, 0, 0)),
                  pl.BlockSpec((1, 1, d), lambda i, e: (e, 0, 0)),
                  pl.BlockSpec((1, d), lambda i, e: (0, 0))],
        out_specs=pl.BlockSpec((tm, d), lambda i, e: (i, 0)),
        out_shape=jax.ShapeDtypeStruct((n, d), F32),
        scratch_shapes=[pltpu.VMEM((tm, d), F32)],
        compiler_params=_cparams("arbitrary", "arbitrary"),
        name="moe",
    )(h3, comb, x2, w1, b1, w2, b2, gfin)


def _pick_tile(n, pref):
    t = min(pref, n)
    while n % t:
        t //= 2
    return t


def kernel(x_prompt, mem_prompt, x_sample, cache_k, cache_v, cache_idx_k, cache_mem_k, cache_mem_v, state_rwkv, state_shift, page_table, rel_bias, norm_final, norm_mix, w_in, mu_shift, rw_w0, rw_w2, rw_a0, rw_a2, rw_g2, rw_k_k, rw_k_a, rw_r_k, rw_ln_w, rw_ln_b, w_out, norm_xattn, norm_mem, xa_wq, xa_wk, xa_wv, xa_wo, norm_ffn, router_w, router_b, moe_w1, moe_b1, moe_w2, moe_b2):
    depth = w_in.shape[0]
    assert depth == 1, "single-layer step"
    l = 0
    bp, tp, d = x_prompt.shape
    bs, ts, _ = x_sample.shape
    n_pool = cache_k.shape[1]
    n_pages = page_table.shape[1]
    past = n_pages * PAGE_SIZE
    mem_len = mem_prompt.shape[1]
    xa_w = XA_H * XA_HEAD
    rw_cols = state_shift.shape[-1]
    assert tp % CW == 0 and n_pages % PAGES_PER_STEP == 0 and ts <= SROWS

    row = lambda a: a.reshape(1, -1)
    bf = lambda a: a.astype(BF16)

    w_pad = bf(jnp.pad(w_in[l], ((0, 0), (0, C_END - w_in.shape[-1]))))
    lora = rw_w2.shape[1]
    P = {
        'mu': row(mu_shift[l]), 'w0': row(rw_w0[l]), 'a0': row(rw_a0[l]),
        'w2': bf(jnp.pad(rw_w2[l], ((0, LANES - lora), (0, 0)))),
        'a2': bf(jnp.pad(rw_a2[l], ((lora, LANES - lora - rw_a2.shape[1]), (0, 0)))),
        'g2': bf(rw_g2[l]), 'kk': row(rw_k_k[l]), 'ka': row(rw_k_a[l]),
        'rk': rw_r_k[l].reshape(RW_H, RW_HEAD),
        'lnw': rw_ln_w[l].reshape(RW_H, RW_HEAD), 'lnb': rw_ln_b[l].reshape(RW_H, RW_HEAD),
    }
    rwh = bf(router_w[l])
    W = {
        'woa': bf(w_out[l][:RW_W]), 'wob': bf(w_out[l][RW_W:]), 'gx': row(norm_xattn[l]),
        'wq': bf(xa_wq[l]), 'wo': bf(xa_wo[l]), 'gf': row(norm_ffn[l]),
        'rwh': rwh, 'rwl': bf(router_w[l] - rwh.astype(F32)), 'rb': row(router_b[l]),
    }
    w1 = bf(moe_w1[l])
    w2 = bf(moe_w2[l])
    b1 = moe_b1[l][:, None, :]
    b2 = moe_b2[l][:, None, :]
    gfin = row(norm_final)
    gmix = row(norm_mix[l])

    prw, q, k, v, qi, kiw, kT, vb, kiT = _in_proj(x_prompt, gmix, w_pad, _pick_tile(tp, 256))
    y_rw, p_s1, p_sh1 = _rwkv_chunk(prw, jnp.zeros((bp, 1, rw_cols), F32),
                                    jnp.zeros((bp, RW_H, RW_HEAD, RW_HEAD), F32), P,
                                    nb=bp, tc=_pick_tile(tp, 256), passes=1)
    y_dsa = _dsa_prompt(rel_bias, q, qi, kiw, kT, vb, kiT)
    wkv = bf(jnp.concatenate([xa_wk[l], xa_wv[l]], axis=1))
    mk_p, mv_p = _mem_kv(mem_prompt.reshape(bp * mem_len, d), row(norm_mem[l]), wkv,
                         _pick_tile(bp * mem_len, 256))
    mk_p = mk_p.reshape(bp, mem_len, xa_w)
    mv_p = mv_p.reshape(bp, mem_len, xa_w)
    x2, h3, comb = _mix(x_prompt, y_rw, y_dsa, mk_p, mv_p, W, _pick_tile(tp, 256))
    n_p = bp * tp
    y_prompt = _moe(h3.reshape(n_p, d), comb.reshape(n_p, N_EXPERTS), x2.reshape(n_p, d),
                    w1, b1, w2, b2, gfin, _pick_tile(n_p, 512)).reshape(bp, tp, d)

    xs = jnp.pad(x_sample, ((0, 0), (0, SROWS - ts), (0, 0)))
    n_s = bs * SROWS
    sprw, sq, sk, sv, sqi, skiw, _, svb, _ = _in_proj(xs.reshape(1, n_s, d), gmix, w_pad,
                                                      _pick_tile(n_s, 256))
    seq = lambda a: a.reshape(bs, SROWS, a.shape[-1])
    sprw, sq, sk, sv, sqi, skiw, svb = map(seq, (sprw, sq, sk, sv, sqi, skiw, svb))
    sy_rw, s_s1, s_sh1 = _rwkv(sprw, state_shift[l][:, None, :], state_rwkv[l], P,
                               nb=_pick_tile(bs, 4), tc=SROWS, n_valid=ts)

    qi_rows = sqi[:, :ts].reshape(bs, ts * IDX_H, IDX_D)
    wcol = skiw[:, :ts, IDX_D:IDX_D + IDX_H].reshape(bs, ts * IDX_H, 1)
    ki_new = bf(skiw[:, :, :IDX_D])
    scores = _dsa_s_scores(page_table, cache_idx_k[l], qi_rows, wcol, ki_new, ts)
    width = scores.shape[-1]
    tok = jnp.arange(SROWS, dtype=I32)
    lim = jnp.tile(jnp.where(tok < ts, past + tok, -1), bs).reshape(n_s, 1)
    k_sel = min(TOPK_MAX, (past + ts) // 4)
    madd = _topk_rows(scores.reshape(n_s, width), lim, k_sel, _pick_tile(n_s, 64))
    madd = madd.reshape(bs, SROWS, width)
    sy_dsa = _dsa_s_attn(page_table, rel_bias,
                         cache_k[l].reshape(n_pool, PAGE_SIZE * DSA_H, DSA_HEAD),
                         cache_v[l].reshape(n_pool, PAGE_SIZE * DSA_H, DSA_HEAD),
                         sq, madd, bf(sk), svb)
    sx2, sh3, scomb = _mix(xs, sy_rw, sy_dsa, cache_mem_k[l].reshape(bs, mem_len, xa_w),
                           cache_mem_v[l].reshape(bs, mem_len, xa_w), W, SROWS)
    y_s = _moe(sh3.reshape(n_s, d), scomb.reshape(n_s, N_EXPERTS), sx2.reshape(n_s, d),
               w1, b1, w2, b2, gfin, _pick_tile(n_s, 256)).reshape(bs, SROWS, d)

    heads = lambda a, b_, t_: a.reshape(1, b_, t_, DSA_H, DSA_HEAD)
    return (y_prompt, y_s[:, :ts],
            p_s1[None], p_sh1.reshape(1, bp, rw_cols),
            heads(k, bp, tp), heads(v, bp, tp), kiw[:, :, :IDX_D][None],
            mk_p.reshape(1, bp, mem_len, XA_H, XA_HEAD), mv_p.reshape(1, bp, mem_len, XA_H, XA_HEAD),
            s_s1[None], s_sh1.reshape(1, bs, rw_cols),
            heads(sk[:, :ts], bs, ts), heads(sv[:, :ts], bs, ts), skiw[:, :ts, :IDX_D][None])
```

```python
import functools
import math

import numpy as np
import jax
import jax.numpy as jnp
from jax import lax
from jax.experimental import pallas as pl
from jax.experimental.pallas import tpu as pltpu

F32 = jnp.float32
BF16 = jnp.bfloat16
I32 = jnp.int32

LANES = 128
VMEM_LIMIT = 56 * 1024 * 1024

NORM_EPS = 1e-5
RW_EPS = 64e-5
NEG_INF = -1e30
RW_H = 8
RW_HEAD = 64
RW_W = RW_H * RW_HEAD
DSA_H = 8
DSA_HEAD = 64
DSA_W = DSA_H * DSA_HEAD
IDX_H = 8
IDX_D = 64
TOPK_MAX = 256
XA_H = 4
XA_HEAD = 128
N_EXPERTS = 32
TOP_K = 4
SWIGLU_LIMIT = 7.0
SWIGLU_ALPHA = 1.702
NUM_BUCKETS = 32
MAX_DISTANCE = 128
PAGE_SIZE = 128
IDX_SCALE = IDX_H ** -0.5 * IDX_D ** -0.5

QB = 128
CW = 512
ATTN_GROUP = 8
PAGES_PER_STEP = 16

INT_MIN = -2 ** 31
INT_MAX = 2 ** 31 - 1


def _bucket_starts():
    max_exact = NUM_BUCKETS // 2
    d = np.arange(0, 4 * MAX_DISTANCE)
    large = max_exact + (np.log(np.maximum(d, 1).astype(np.float32) / np.float32(max_exact))
                         / np.float32(math.log(MAX_DISTANCE / max_exact))
                         * np.float32(NUM_BUCKETS - max_exact)).astype(np.int32)
    b = np.where(d < max_exact, d, np.minimum(large, NUM_BUCKETS - 1))
    starts = []
    for j in range(NUM_BUCKETS):
        hit = np.nonzero(b == j)[0]
        starts.append(int(hit[0]) if hit.size else None)
    return starts


BUCKET_START = _bucket_starts()


def _cparams(*sem):
    return pltpu.CompilerParams(dimension_semantics=sem, vmem_limit_bytes=VMEM_LIMIT)


def _rms(x, g):
    return x * lax.rsqrt(jnp.mean(x * x, axis=-1, keepdims=True) + NORM_EPS) * g


def _dot(a, b):
    return jnp.dot(a, b, preferred_element_type=F32)


def _dot_nt(a, b):
    return lax.dot_general(a, b, (((1,), (1,)), ((), ())), preferred_element_type=F32)


def _softplus(x):
    return jnp.maximum(x, 0.0) + jnp.log1p(jnp.exp(-jnp.abs(x)))


def _sigmoid(x):
    return 1.0 / (1.0 + jnp.exp(-x))


C_RW = 0
C_Q = 1792
C_K = C_Q + DSA_W
C_V = C_K + DSA_W
C_QI = C_V + DSA_W
C_KIW = C_QI + IDX_H * IDX_D
C_END = C_KIW + LANES


def _in_proj_kernel(x_ref, g_ref, w_ref, prw_ref, q_ref, k_ref, v_ref, qi_ref, kiw_ref,
                    kT_ref, vb_ref, kiT_ref):
    h = _rms(x_ref[0], g_ref[...]).astype(BF16)

    def mm(lo, hi):
        return _dot(h, w_ref[:, lo:hi])

    prw_ref[0] = mm(C_RW, C_Q)
    q_ref[0] = mm(C_Q, C_K).astype(BF16)
    k = mm(C_K, C_V)
    k_ref[0] = k
    kT_ref[0] = k.T.astype(BF16)
    v = mm(C_V, C_QI)
    v_ref[0] = v
    vb_ref[0] = v.astype(BF16)
    qi_ref[0] = mm(C_QI, C_KIW).astype(BF16)
    kiw = mm(C_KIW, C_END)
    kiw_ref[0] = kiw
    kiT_ref[0] = kiw.T[0:IDX_D, :].astype(BF16)


def _in_proj(x, g, w_pad, tm):
    b, t, d = x.shape
    grid = (b, t // tm)
    row = lambda w, dt: jax.ShapeDtypeStruct((b, t, w), dt)
    rspec = lambda w: pl.BlockSpec((1, tm, w), lambda i, j: (i, j, 0))
    tspec = lambda w: pl.BlockSpec((1, w, tm), lambda i, j: (i, 0, j))
    return pl.pallas_call(
        _in_proj_kernel,
        grid=grid,
        in_specs=[rspec(d),
                  pl.BlockSpec((1, d), lambda i, j: (0, 0)),
                  pl.BlockSpec((d, C_END), lambda i, j: (0, 0))],
        out_specs=[rspec(C_Q), rspec(DSA_W), rspec(DSA_W), rspec(DSA_W), rspec(IDX_H * IDX_D),
                   rspec(LANES), tspec(DSA_W), rspec(DSA_W), tspec(IDX_D)],
        out_shape=[row(C_Q, F32), row(DSA_W, BF16), row(DSA_W, F32), row(DSA_W, F32),
                   row(IDX_H * IDX_D, BF16), row(LANES, F32),
                   jax.ShapeDtypeStruct((b, DSA_W, t), BF16), row(DSA_W, BF16),
                   jax.ShapeDtypeStruct((b, IDX_D, t), BF16)],
        compiler_params=_cparams("arbitrary", "arbitrary"),
        name="in_proj",
    )(x, g, w_pad)


def _rwkv_kernel(n_steps, prw_ref, sh0_ref, s0_ref, mu_ref, w0_ref, w2_ref, a0_ref, a2_ref, g2_ref,
                 kkw_ref, kaw_ref, rk_ref, lnw_ref, lnb_ref,
                 y_ref, s1_ref, sh1_ref,
                 st_scr, carry_scr, r_s, w_s, k_s, v_s, kk_s, b_s, y_s, g_s):
    nb, tc = prw_ref.shape[0], prw_ref.shape[1]
    c = pl.program_id(1)

    @pl.when(c == 0)
    def _():
        st_scr[...] = s0_ref[...]
        carry_scr[...] = sh0_ref[...]

    row = lax.broadcasted_iota(I32, (tc, 1), 0)
    lane = lax.broadcasted_iota(I32, (1, LANES), 1)
    for b in range(nb):
        x = prw_ref[b]
        prev = jnp.where(row == 0, carry_scr[b], pltpu.roll(x, 1, 0))
        carry_scr[b] = x[n_steps - 1:n_steps, :]
        ps = x + (prev - x) * mu_ref[...]
        r = ps[:, 0:RW_W]
        k = ps[:, RW_W:2 * RW_W]
        v = ps[:, 2 * RW_W:3 * RW_W]
        la = ps[:, 3 * RW_W:3 * RW_W + LANES]
        gd = ps[:, 3 * RW_W + LANES:3 * RW_W + 2 * LANES]
        z = jnp.where(lane < 64, jnp.tanh(la), la).astype(BF16)
        w_log = -_softplus(-(w0_ref[...] + _dot(z, w2_ref[...]))) - 0.5
        decay = jnp.exp(-jnp.exp(w_log))
        a = _sigmoid(a0_ref[...] + _dot(z, a2_ref[...]))
        g_s[b] = _dot(_sigmoid(gd).astype(BF16), g2_ref[...])
        kk = k * kkw_ref[...]
        k2 = k * (1.0 + (a - 1.0) * kaw_ref[...])
        for h in range(RW_H):
            sl = slice(RW_HEAD * h, RW_HEAD * (h + 1))
            kkh = kk[:, sl]
            nrm = jnp.sqrt(jnp.sum(kkh * kkh, axis=-1, keepdims=True))
            kkh = kkh / jnp.maximum(nrm, 1e-12)
            r_s[b, h] = r[:, sl]
            w_s[b, h] = decay[:, sl]
            k_s[b, h] = k2[:, sl]
            v_s[b, h] = v[:, sl]
            kk_s[b, h] = kkh
            b_s[b, h] = kkh * a[:, sl]

    eye = (lax.broadcasted_iota(I32, (RW_HEAD, RW_HEAD), 0)
           == lax.broadcasted_iota(I32, (RW_HEAD, RW_HEAD), 1))

    def step(t, carry):
        for b in range(nb):
            for h in range(RW_H):
                s = st_scr[b, h]
                tt = pl.ds(t, 1)
                sa = -jnp.sum(s * kk_s[b, h, tt, :], axis=1, keepdims=True)
                vcol = jnp.sum(jnp.where(eye, v_s[b, h, tt, :], 0.0), axis=1, keepdims=True)
                s = s * w_s[b, h, tt, :] + sa * b_s[b, h, tt, :] + vcol * k_s[b, h, tt, :]
                st_scr[b, h] = s
                ycol = jnp.sum(s * r_s[b, h, tt, :], axis=1, keepdims=True)
                y_s[b, h, tt, :] = jnp.sum(jnp.where(eye, ycol, 0.0), axis=0, keepdims=True)
        return carry

    lax.fori_loop(0, n_steps, step, 0)

    for b in range(nb):
        for h in range(RW_H):
            sl = slice(RW_HEAD * h, RW_HEAD * (h + 1))
            y = y_s[b, h]
            mean = jnp.mean(y, axis=-1, keepdims=True)
            var = jnp.mean(jnp.square(y - mean), axis=-1, keepdims=True)
            yn = (y - mean) * lax.rsqrt(var + RW_EPS) * lnw_ref[h:h + 1, :] + lnb_ref[h:h + 1, :]
            rr, kk2, vv = r_s[b, h], k_s[b, h], v_s[b, h]
            bonus = jnp.sum(rr * kk2 * rk_ref[h:h + 1, :], axis=-1, keepdims=True) * vv
            y_ref[b, :, sl] = ((yn + bonus) * g_s[b][:, sl]).astype(BF16)
    s1_ref[...] = st_scr[...]
    sh1_ref[...] = carry_scr[...]


def _rwkv(prw, shift0, s0, P, nb, tc, n_valid):
    b, t, cols = prw.shape
    n_steps = min(tc, n_valid)
    grid = (b // nb, t // tc)
    full = lambda a: pl.BlockSpec(a.shape, lambda i, j: (0,) * a.ndim)
    params = [P['mu'], P['w0'], P['w2'], P['a0'], P['a2'], P['g2'], P['kk'], P['ka'], P['rk'],
              P['lnw'], P['lnb']]
    hs = lambda: pltpu.VMEM((nb, RW_H, tc, RW_HEAD), F32)
    return pl.pallas_call(
        functools.partial(_rwkv_kernel, n_steps),
        grid=grid,
        in_specs=[pl.BlockSpec((nb, tc, cols), lambda i, j: (i, j, 0)),
                  pl.BlockSpec((nb, 1, cols), lambda i, j: (i, 0, 0)),
                  pl.BlockSpec((nb, RW_H, RW_HEAD, RW_HEAD), lambda i, j: (i, 0, 0, 0))]
                 + [full(a) for a in params],
        out_specs=[pl.BlockSpec((nb, tc, RW_W), lambda i, j: (i, j, 0)),
                   pl.BlockSpec((nb, RW_H, RW_HEAD, RW_HEAD), lambda i, j: (i, 0, 0, 0)),
                   pl.BlockSpec((nb, 1, cols), lambda i, j: (i, 0, 0))],
        out_shape=[jax.ShapeDtypeStruct((b, t, RW_W), BF16),
                   jax.ShapeDtypeStruct((b, RW_H, RW_HEAD, RW_HEAD), F32),
                   jax.ShapeDtypeStruct((b, 1, cols), F32)],
        scratch_shapes=[pltpu.VMEM((nb, RW_H, RW_HEAD, RW_HEAD), F32),
                        pltpu.VMEM((nb, 1, cols), F32),
                        hs(), hs(), hs(), hs(), hs(), hs(), hs(),
                        pltpu.VMEM((nb, tc, RW_W), F32)],
        compiler_params=_cparams("arbitrary", "arbitrary"),
        name="rwkv",
    )(prw, shift0, s0, *params)


RC = 64
RC_GROUP = 8


def _split3(x):
    hi = x.astype(BF16)
    r1 = x - hi.astype(F32)
    mid = r1.astype(BF16)
    lo = (r1 - mid.astype(F32)).astype(BF16)
    return hi, mid, lo


def _mm(a, b, passes):
    if passes == 1:
        return _dot(a.astype(BF16), b.astype(BF16))
    ah, al = _split_bf16(a)
    bh, bl = _split_bf16(b)
    return _dot(ah, bh) + _dot(ah, bl) + _dot(al, bh)


def _mm_nt(a, b, passes):
    if passes == 1:
        return _dot_nt(a.astype(BF16), b.astype(BF16))
    ah, al = _split_bf16(a)
    bh, bl = _split_bf16(b)
    return _dot_nt(ah, bh) + _dot_nt(ah, bl) + _dot_nt(al, bh)


def _exact_left(m_bf16, x):
    hi, mid, lo = _split3(x)
    return _dot(m_bf16, hi) + _dot(m_bf16, mid) + _dot(m_bf16, lo)


def _exact_right(x, m_bf16):
    hi, mid, lo = _split3(x)
    return _dot(hi, m_bf16) + _dot(mid, m_bf16) + _dot(lo, m_bf16)


def _rwkv_chunk_kernel(passes, prw_ref, sh0_ref, s0_ref, mu_ref, w0_ref, w2_ref, a0_ref, a2_ref,
                       g2_ref, kkw_ref, kaw_ref, rk_ref, lnw_ref, lnb_ref, ltri_ref, lones_ref,
                       hones_ref, y_ref, s1_ref, sh1_ref,
                       st_scr, carry_scr, rh_s, kkh_s, bt_s, kt_s, bg_s, kg_s, v_s, gc_s, y_s,
                       bonus_s, g_s):
    nb, tc = prw_ref.shape[0], prw_ref.shape[1]
    nh = nb * RW_H
    c = pl.program_id(1)

    @pl.when(c == 0)
    def _():
        for b in range(nb):
            for h in range(RW_H):
                st_scr[b * RW_H + h] = s0_ref[b, h]
        carry_scr[...] = sh0_ref[...]

    row = lax.broadcasted_iota(I32, (tc, 1), 0)
    lane = lax.broadcasted_iota(I32, (1, LANES), 1)
    hones = hones_ref[...]
    for b in range(nb):
        x = prw_ref[b]
        prev = jnp.where(row == 0, carry_scr[b], pltpu.roll(x, 1, 0))
        carry_scr[b] = x[tc - 1:tc, :]
        ps = x + (prev - x) * mu_ref[...]
        r = ps[:, 0:RW_W]
        k = ps[:, RW_W:2 * RW_W]
        v = ps[:, 2 * RW_W:3 * RW_W]
        la = ps[:, 3 * RW_W:3 * RW_W + LANES]
        gd = ps[:, 3 * RW_W + LANES:3 * RW_W + 2 * LANES]
        z = jnp.where(lane < 64, jnp.tanh(la), la).astype(BF16)
        w_log = -_softplus(-(w0_ref[...] + _dot(z, w2_ref[...]))) - 0.5
        lw = -jnp.exp(w_log)
        a = _sigmoid(a0_ref[...] + _dot(z, a2_ref[...]))
        g_s[b] = _dot(_sigmoid(gd).astype(BF16), g2_ref[...])
        kk = k * kkw_ref[...]
        kk = kk / jnp.maximum(jnp.sqrt(_exact_right(kk * kk, hones)), 1e-12)
        k2 = k * (1.0 + (a - 1.0) * kaw_ref[...])
        bb = kk * a
        bonus_s[b] = _exact_right(r * k2 * rk_ref[...], hones) * v
        gsum = _exact_left(ltri_ref[...], lw)
        gend = _exact_left(lones_ref[...], lw)
        e_neg = jnp.exp(-gsum)
        e_end = jnp.exp(gend - gsum)
        cols = {'rh': r * jnp.exp(gsum), 'kkh': kk * jnp.exp(gsum - lw), 'bt': bb * e_neg,
                'kt': k2 * e_neg, 'bg': bb * e_end, 'kg': k2 * e_end, 'v': v, 'gc': jnp.exp(gend)}
        dst = {'rh': rh_s, 'kkh': kkh_s, 'bt': bt_s, 'kt': kt_s, 'bg': bg_s, 'kg': kg_s, 'v': v_s,
               'gc': gc_s}
        for name, val in cols.items():
            for h in range(RW_H):
                dst[name][b * RW_H + h] = val[:, RW_HEAD * h:RW_HEAD * (h + 1)]

    ri = lax.broadcasted_iota(I32, (RC, RC), 0)
    ci = lax.broadcasted_iota(I32, (RC, RC), 1)
    strict = ri > ci
    incl = ri >= ci
    eye = ri == ci
    eye_f = jnp.where(eye, 1.0, 0.0)
    n_double = int(math.log2(RC)) - 1

    def head_group(heads, sub):
        base = pl.multiple_of(sub * RC, RC)
        rows = pl.ds(base, RC)
        mm = lambda a, b: _mm(a, b, passes)
        each = lambda f: [f(i) for i in range(len(heads))]
        rh = [rh_s[hd, rows, :] for hd in heads]
        kkh = [kkh_s[hd, rows, :] for hd in heads]
        vv = [v_s[hd, rows, :] for hd in heads]
        gram = each(lambda i: _mm_nt(
            jnp.concatenate([kkh[i], rh[i]], axis=0),
            jnp.concatenate([bt_s[heads[i], rows, :], kt_s[heads[i], rows, :]], axis=0), passes))
        mb = each(lambda i: jnp.where(strict, gram[i][0:RC, 0:RC], 0.0))
        mkv = each(lambda i: mm(jnp.where(strict, gram[i][0:RC, RC:2 * RC], 0.0), vv[i]))
        nkv = each(lambda i: mm(jnp.where(incl, gram[i][RC:2 * RC, RC:2 * RC], 0.0), vv[i]))
        nb_ = each(lambda i: jnp.where(incl, gram[i][RC:2 * RC, 0:RC], 0.0))
        inv = each(lambda i: eye_f - mb[i])
        q = mb
        for _ in range(n_double):
            q = each(lambda i: mm(q[i], q[i]))
            inv = each(lambda i: inv[i] + mm(inv[i], q[i]))
        u = each(lambda i: -mm(inv[i], jnp.concatenate([kkh[i], mkv[i]], axis=1)))
        yy = each(lambda i: mm(nb_[i], u[i]))
        zz = each(lambda i: mm(u[i].T, bg_s[heads[i], rows, :]))
        vk = each(lambda i: mm(vv[i].T, kg_s[heads[i], rows, :]))
        s = [st_scr[hd] for hd in heads]
        ys = each(lambda i: _mm_nt(rh[i] + yy[i][:, 0:RC], s[i], passes))
        z1 = each(lambda i: jnp.where(eye, gc_s[heads[i], pl.ds(base, 1), :], 0.0)
                  + zz[i][0:RW_HEAD])
        sn = each(lambda i: mm(s[i], z1[i]))
        for i, hd in enumerate(heads):
            y_s[hd, rows, :] = ys[i] + yy[i][:, RC:2 * RC] + nkv[i]
            st_scr[hd] = sn[i] + zz[i][RW_HEAD:2 * RW_HEAD] + vk[i]

    def chunk_step(sub, carry):
        for g0 in range(0, nh, RC_GROUP):
            head_group(list(range(g0, min(g0 + RC_GROUP, nh))), sub)
        return carry

    lax.fori_loop(0, tc // RC, chunk_step, 0)

    for b in range(nb):
        for h in range(RW_H):
            sl = slice(RW_HEAD * h, RW_HEAD * (h + 1))
            y = y_s[b * RW_H + h]
            mean = jnp.mean(y, axis=-1, keepdims=True)
            var = jnp.mean(jnp.square(y - mean), axis=-1, keepdims=True)
            yn = (y - mean) * lax.rsqrt(var + RW_EPS) * lnw_ref[h:h + 1, :] + lnb_ref[h:h + 1, :]
            y_ref[b, :, sl] = ((yn + bonus_s[b, :, sl]) * g_s[b, :, sl]).astype(BF16)
            s1_ref[b, h] = st_scr[b * RW_H + h]
    sh1_ref[...] = carry_scr[...]


def _rwkv_chunk(prw, shift0, s0, P, nb, tc, passes):
    b, t, cols = prw.shape
    tok = np.arange(tc)
    same = (tok[:, None] // RC) == (tok[None, :] // RC)
    ltri = jnp.asarray(same & (tok[:, None] >= tok[None, :]), BF16)
    lones = jnp.asarray(same, BF16)
    col = np.arange(RW_W)
    hones = jnp.asarray((col[:, None] // RW_HEAD) == (col[None, :] // RW_HEAD), BF16)
    full = lambda a: pl.BlockSpec(a.shape, lambda i, j: (0,) * a.ndim)
    params = [P['mu'], P['w0'], P['w2'], P['a0'], P['a2'], P['g2'], P['kk'], P['ka'],
              P['rk'].reshape(1, RW_W), P['lnw'], P['lnb'], ltri, lones, hones]
    nh = nb * RW_H
    hs = lambda: pltpu.VMEM((nh, tc, RW_HEAD), F32)
    return pl.pallas_call(
        functools.partial(_rwkv_chunk_kernel, passes),
        grid=(b // nb, t // tc),
        in_specs=[pl.BlockSpec((nb, tc, cols), lambda i, j: (i, j, 0)),
                  pl.BlockSpec((nb, 1, cols), lambda i, j: (i, 0, 0)),
                  pl.BlockSpec((nb, RW_H, RW_HEAD, RW_HEAD), lambda i, j: (i, 0, 0, 0))]
                 + [full(a) for a in params],
        out_specs=[pl.BlockSpec((nb, tc, RW_W), lambda i, j: (i, j, 0)),
                   pl.BlockSpec((nb, RW_H, RW_HEAD, RW_HEAD), lambda i, j: (i, 0, 0, 0)),
                   pl.BlockSpec((nb, 1, cols), lambda i, j: (i, 0, 0))],
        out_shape=[jax.ShapeDtypeStruct((b, t, RW_W), BF16),
                   jax.ShapeDtypeStruct((b, RW_H, RW_HEAD, RW_HEAD), F32),
                   jax.ShapeDtypeStruct((b, 1, cols), F32)],
        scratch_shapes=[pltpu.VMEM((nh, RW_HEAD, RW_HEAD), F32),
                        pltpu.VMEM((nb, 1, cols), F32),
                        hs(), hs(), hs(), hs(), hs(), hs(), hs(), hs(), hs(),
                        pltpu.VMEM((nb, tc, RW_W), F32), pltpu.VMEM((nb, tc, RW_W), F32)],
        compiler_params=_cparams("arbitrary", "arbitrary"),
        name="rwkv_chunk",
    )(prw, shift0, s0, *params)


def _pattern_to_f32(u):
    key = u ^ INT_MIN
    return lax.bitcast_convert_type(jnp.where(key >= 0, key, key ^ 0x7FFFFFFF), F32)


def _topk_mask(sc_ref, madd_ref, p_scr, rows, nch, k_sel, lim, idx_bits):
    ntile = CW // LANES
    lane = lax.broadcasted_iota(I32, (rows, LANES), 1)

    def count(pred):
        def body(c, acc):
            off = pl.multiple_of(c * CW, CW)
            x = sc_ref[:, pl.ds(off, CW)]
            for t in range(ntile):
                acc = acc + pred(x[:, LANES * t:LANES * (t + 1)], off + LANES * t)
            return acc
        acc = lax.fori_loop(0, nch, body, jnp.zeros((rows, LANES), I32))
        return jnp.sum(acc, axis=1, keepdims=True)

    def bcast(col):
        return jnp.broadcast_to(col, (rows, LANES))

    def searching(state):
        it, alive, _, _ = state
        return (it < 32) & (alive > 0)

    def value_bit(state):
        it, _, tau, cnt_tau = state
        cand = tau | jnp.left_shift(jnp.int32(1), 31 - it)
        cb = bcast(_pattern_to_f32(cand))
        cnt = count(lambda x, off: jnp.where(x >= cb, 1, 0))
        take = cnt >= k_sel
        tau = jnp.where(take, cand, tau)
        cnt_tau = jnp.where(take, cnt, cnt_tau)
        settled = jnp.where(cnt_tau == k_sel, 1, jnp.where(lim < 0, 1, 0))
        return it + 1, 1 - jnp.min(settled), tau, cnt_tau

    _, _, tau, _ = lax.while_loop(
        searching, value_bit,
        (jnp.int32(0), jnp.int32(1), jnp.zeros((rows, 1), I32),
         jnp.zeros((rows, 1), I32) + nch * CW))
    thr = _pattern_to_f32(tau)
    thr_b = bcast(thr)
    cnt_gt = count(lambda x, off: jnp.where(x > thr_b, 1, 0))
    cnt_ge = count(lambda x, off: jnp.where(x >= thr_b, 1, 0))
    need = k_sel - cnt_gt
    tie = jnp.where(cnt_ge > k_sel, jnp.where(thr > NEG_INF, 1, 0), 0)

    p_scr[...] = jnp.full((rows, LANES), INT_MAX, I32)

    @pl.when(jnp.max(tie) > 0)
    def _():
        def index_bit(it, p):
            cand = p | jnp.left_shift(jnp.int32(1), idx_bits - 1 - it)
            cb = bcast(cand)
            cnt = count(lambda x, off: jnp.where(x == thr_b, jnp.where(off + lane < cb, 1, 0), 0))
            return jnp.where(cnt < need, cand, p)
        p = lax.fori_loop(0, idx_bits, index_bit, jnp.zeros((rows, 1), I32))
        p_scr[...] = bcast(p)

    p_b = p_scr[...]
    lim_b = bcast(lim)

    def fin(c, carry):
        off = pl.multiple_of(c * CW, CW)
        x = sc_ref[:, pl.ds(off, CW)]
        for t in range(ntile):
            xt = x[:, LANES * t:LANES * (t + 1)]
            idx = off + LANES * t + lane
            sel = jnp.where(xt > thr_b, 0.0,
                            jnp.where(xt == thr_b, jnp.where(idx <= p_b, 0.0, NEG_INF), NEG_INF))
            madd_ref[:, pl.ds(pl.multiple_of(off + LANES * t, LANES), LANES)] = (
                jnp.where(idx <= lim_b, sel, NEG_INF))
        return carry

    lax.fori_loop(0, nch, fin, 0)


def _bias_table(rb_ref, h, d):
    bias = jnp.full(d.shape, rb_ref[0, h], F32)
    for j in range(1, NUM_BUCKETS):
        if BUCKET_START[j] is not None:
            bias = jnp.where(d >= BUCKET_START[j], rb_ref[j, h], bias)
    return bias


def _dsa_prompt_kernel(k_sel, idx_bits, rb_ref, q_ref, qi_ref, kiw_ref, kT_ref, vb_ref, kiT_ref,
                       o_ref, sc_scr, madd_scr, p_scr, btab_scr):
    i = pl.program_id(1)
    nch = ((i + 1) * QB + CW - 1) // CW

    @pl.when((pl.program_id(0) == 0) & (i == 0))
    def _():
        d = (lax.broadcasted_iota(I32, (QB, 2 * QB), 0)
             - lax.broadcasted_iota(I32, (QB, 2 * QB), 1) + QB)
        for h in range(DSA_H):
            btab_scr[h] = _bias_table(rb_ref, h, d) - rb_ref[NUM_BUCKETS - 1, h]

    tq = i * QB + lax.broadcasted_iota(I32, (QB, 1), 0)
    qi = qi_ref[0]
    wi = kiw_ref[0][:, IDX_D:IDX_D + IDX_H]
    qi_all = jnp.concatenate([qi[:, IDX_D * h:IDX_D * (h + 1)] for h in range(IDX_H)], axis=0)
    wi_h = [wi[:, h:h + 1] for h in range(IDX_H)]

    def score_chunk(c, carry):
        off = pl.multiple_of(c * CW, CW)
        s = _dot(qi_all, kiT_ref[0, :, pl.ds(off, CW)])
        acc = jnp.zeros((QB, CW), F32)
        for h in range(IDX_H):
            acc = acc + jnp.maximum(s[QB * h:QB * (h + 1)], 0.0) * wi_h[h]
        spos = off + lax.broadcasted_iota(I32, (1, CW), 1)
        sc_scr[:, pl.ds(off, CW)] = jnp.where(spos <= tq, acc * IDX_SCALE + 0.0, NEG_INF)
        return carry

    lax.fori_loop(0, nch, score_chunk, 0)

    _topk_mask(sc_scr, madd_scr, p_scr, QB, nch, k_sel, tq, idx_bits)

    q = q_ref[0] * (DSA_HEAD ** -0.5)
    tiles = CW // QB
    hsl = [slice(DSA_HEAD * h, DSA_HEAD * (h + 1)) for h in range(DSA_H)]
    qh = [q[:, sl] for sl in hsl]

    def attend(heads, near, c, carry):
        n = len(heads)
        off = pl.multiple_of(c * CW, CW)
        madd = madd_scr[:, pl.ds(off, CW)]
        s = [_dot(qh[h], kT_ref[0, hsl[h], pl.ds(off, CW)]) + madd for h in heads]
        if near:
            for j, h in enumerate(heads):
                parts = []
                for t in range(tiles):
                    delta = i - (c * tiles + t)
                    parts.append(jnp.where(delta == 0, btab_scr[h, :, QB:2 * QB],
                                           jnp.where(delta == 1, btab_scr[h, :, 0:QB], 0.0)))
                s[j] = s[j] + jnp.concatenate(parts, axis=1)
        m_new = [jnp.maximum(carry[j][0], jnp.max(s[j], axis=1, keepdims=True)) for j in range(n)]
        p = [jnp.exp(s[j] - m_new[j]) for j in range(n)]
        pv = [_dot(p[j].astype(BF16), vb_ref[0, pl.ds(off, CW), hsl[h]])
              for j, h in enumerate(heads)]
        out = []
        for j in range(n):
            m, l, acc = carry[j]
            alpha = jnp.exp(m - m_new[j])
            out.append((m_new[j], l * alpha + jnp.sum(p[j], axis=1, keepdims=True),
                        acc * alpha + pv[j]))
        return tuple(out)

    n_far = jnp.maximum(i - 1, 0) // tiles
    for g0 in range(0, DSA_H, ATTN_GROUP):
        heads = list(range(g0, g0 + ATTN_GROUP))
        init = tuple((jnp.full((QB, 1), NEG_INF, F32), jnp.zeros((QB, 1), F32),
                      jnp.zeros((QB, DSA_HEAD), F32)) for _ in heads)
        carry = lax.fori_loop(0, n_far, functools.partial(attend, heads, False), init)
        carry = lax.fori_loop(n_far, nch, functools.partial(attend, heads, True), carry)
        for j, h in enumerate(heads):
            _, l, acc = carry[j]
            o_ref[0, :, hsl[h]] = (acc / l).astype(BF16)


def _dsa_prompt(rel_bias, q, qi, kiw, kT, vb, kiT):
    b, t, _ = q.shape
    k_sel = min(TOPK_MAX, t // 4)
    idx_bits = max(1, int(math.ceil(math.log2(t))))
    blk = lambda w: pl.BlockSpec((1, QB, w), lambda bi, i: (bi, i, 0))
    whole = lambda shape: pl.BlockSpec(shape, lambda bi, i: (bi, 0, 0))
    return pl.pallas_call(
        functools.partial(_dsa_prompt_kernel, k_sel, idx_bits),
        grid=(b, t // QB),
        in_specs=[pl.BlockSpec(memory_space=pltpu.SMEM),
                  blk(DSA_W), blk(IDX_H * IDX_D), blk(LANES),
                  whole((1, DSA_W, t)), whole((1, t, DSA_W)), whole((1, IDX_D, t))],
        out_specs=blk(DSA_W),
        out_shape=jax.ShapeDtypeStruct((b, t, DSA_W), BF16),
        scratch_shapes=[pltpu.VMEM((QB, t), F32), pltpu.VMEM((QB, t), F32),
                        pltpu.VMEM((QB, LANES), I32),
                        pltpu.VMEM((DSA_H, QB, 2 * QB), F32)],
        compiler_params=_cparams("arbitrary", "arbitrary"),
        name="dsa_prompt",
    )(rel_bias, q, qi, kiw, kT, vb, kiT)


SROWS = 8
SCHUNK = PAGES_PER_STEP * PAGE_SIZE


def _dsa_s_scores_kernel(n_tok, n_past_chunks, pt_ref, qi_ref, wcol_ref, kinew_ref, *rest):
    pages, out_ref = rest[:PAGES_PER_STEP], rest[PAGES_PER_STEP]
    c = pl.program_id(1)
    qi = qi_ref[0]
    wcol = wcol_ref[0]

    def head_sum(s):
        s = jnp.maximum(s, 0.0) * wcol
        return jnp.sum(s.reshape(n_tok, IDX_H, s.shape[-1]), axis=1) * IDX_SCALE + 0.0

    @pl.when(c < n_past_chunks)
    def _():
        kit = jnp.concatenate([pg[0] for pg in pages], axis=1).astype(BF16)
        out_ref[0, 0:n_tok, :] = head_sum(_dot(qi, kit))
        out_ref[0, n_tok:SROWS, :] = jnp.full((SROWS - n_tok, SCHUNK), NEG_INF, F32)

    @pl.when(c == n_past_chunks)
    def _():
        s = head_sum(_dot_nt(qi, kinew_ref[0]))
        tr = lax.broadcasted_iota(I32, (n_tok, SROWS), 0)
        tc = lax.broadcasted_iota(I32, (n_tok, SROWS), 1)
        out_ref[...] = jnp.full((1, SROWS, SCHUNK), NEG_INF, F32)
        out_ref[0, 0:n_tok, 0:SROWS] = jnp.where(tc <= tr, s, NEG_INF)


def _dsa_s_scores(page_table, cache_ki, qi_rows, wcol, ki_new, n_tok):
    b, n_pages = page_table.shape
    npc = n_pages // PAGES_PER_STEP
    page_specs = [
        pl.BlockSpec((1, IDX_D, PAGE_SIZE),
                     lambda bi, c, pt, j=j: (pt[bi, jnp.minimum(c, npc - 1) * PAGES_PER_STEP + j], 0, 0))
        for j in range(PAGES_PER_STEP)]
    rows = qi_rows.shape[1]
    grid_spec = pltpu.PrefetchScalarGridSpec(
        num_scalar_prefetch=1,
        grid=(b, npc + 1),
        in_specs=[pl.BlockSpec((1, rows, IDX_D), lambda bi, c, pt: (bi, 0, 0)),
                  pl.BlockSpec((1, rows, 1), lambda bi, c, pt: (bi, 0, 0)),
                  pl.BlockSpec((1, SROWS, IDX_D), lambda bi, c, pt: (bi, 0, 0))] + page_specs,
        out_specs=pl.BlockSpec((1, SROWS, SCHUNK), lambda bi, c, pt: (bi, 0, c)),
    )
    return pl.pallas_call(
        functools.partial(_dsa_s_scores_kernel, n_tok, npc),
        grid_spec=grid_spec,
        out_shape=jax.ShapeDtypeStruct((b, SROWS, (npc + 1) * SCHUNK), F32),
        compiler_params=_cparams("arbitrary", "arbitrary"),
        name="dsa_s_scores",
    )(page_table, qi_rows, wcol, ki_new, *([cache_ki] * PAGES_PER_STEP))


def _topk_rows_kernel(k_sel, idx_bits, sc_ref, lim_ref, madd_ref, p_scr):
    rows, width = sc_ref.shape
    _topk_mask(sc_ref, madd_ref, p_scr, rows, width // CW, k_sel, lim_ref[...], idx_bits)


def _topk_rows(scores, lim, k_sel, rows_per_step):
    n, width = scores.shape
    idx_bits = max(1, int(math.ceil(math.log2(width))))
    return pl.pallas_call(
        functools.partial(_topk_rows_kernel, k_sel, idx_bits),
        grid=(n // rows_per_step,),
        in_specs=[pl.BlockSpec((rows_per_step, width), lambda i: (i, 0)),
                  pl.BlockSpec((rows_per_step, 1), lambda i: (i, 0))],
        out_specs=pl.BlockSpec((rows_per_step, width), lambda i: (i, 0)),
        out_shape=jax.ShapeDtypeStruct((n, width), F32),
        scratch_shapes=[pltpu.VMEM((rows_per_step, LANES), I32)],
        compiler_params=_cparams("arbitrary"),
        name="topk_rows",
    )(scores, lim)


def _dsa_s_attn_kernel(n_past_chunks, past, pt_ref, rb_ref, q_ref, madd_ref, knew_ref, vnew_ref,
                       *rest):
    kp = rest[:PAGES_PER_STEP]
    vp = rest[PAGES_PER_STEP:2 * PAGES_PER_STEP]
    o_ref, m_scr, l_scr, acc_scr = rest[2 * PAGES_PER_STEP:]
    c = pl.program_id(1)
    rows = DSA_H * SROWS

    @pl.when(c == 0)
    def _():
        m_scr[...] = jnp.full(m_scr.shape, NEG_INF, F32)
        l_scr[...] = jnp.zeros(l_scr.shape, F32)
        acc_scr[...] = jnp.zeros(acc_scr.shape, F32)

    q = q_ref[0] * (DSA_HEAD ** -0.5)
    head_of_row = lax.broadcasted_iota(I32, (rows, DSA_W), 0) // SROWS
    head_of_col = lax.broadcasted_iota(I32, (rows, DSA_W), 1) // DSA_HEAD
    qblk = jnp.where(head_of_row == head_of_col, jnp.concatenate([q] * DSA_H, axis=0),
                     jnp.zeros((), BF16))

    def near_bias(width, key0):
        tok = lax.broadcasted_iota(I32, (SROWS, width), 0)
        d = past + tok - (key0 + lax.broadcasted_iota(I32, (SROWS, width), 1))
        return jnp.concatenate(
            [_bias_table(rb_ref, h, d) - rb_ref[NUM_BUCKETS - 1, h] for h in range(DSA_H)], axis=0)

    def update(s, pv_fn):
        m = m_scr[...]
        m_new = jnp.maximum(m, jnp.max(s, axis=1, keepdims=True))
        alpha = jnp.exp(m - m_new)
        p = jnp.exp(s - m_new)
        l_scr[...] = l_scr[...] * alpha + jnp.sum(p, axis=1, keepdims=True)
        acc_scr[...] = acc_scr[...] * alpha + pv_fn(p.astype(BF16))
        m_scr[...] = m_new

    def chunk_t(pages):
        return jnp.concatenate([pg[0].reshape(DSA_W, PAGE_SIZE) for pg in pages],
                               axis=1).astype(BF16)

    @pl.when(c < n_past_chunks)
    def _():
        s = _dot(qblk, chunk_t(kp)) + jnp.concatenate([madd_ref[0]] * DSA_H, axis=0)
        is_last = c == n_past_chunks - 1
        tail = SCHUNK - PAGE_SIZE
        nb_ = jnp.where(is_last, near_bias(PAGE_SIZE, past - PAGE_SIZE), 0.0)
        s = jnp.concatenate([s[:, 0:tail], s[:, tail:SCHUNK] + nb_], axis=1)
        vt = chunk_t(vp)
        update(s, lambda pb: _dot_nt(pb, vt))

    @pl.when(c == n_past_chunks)
    def _():
        s = (_dot_nt(qblk, knew_ref[0]) + jnp.concatenate([madd_ref[0, :, 0:SROWS]] * DSA_H, axis=0)
             + near_bias(SROWS, past))
        update(s, lambda pb: _dot(pb, vnew_ref[0]))
        for h in range(DSA_H):
            r = slice(SROWS * h, SROWS * (h + 1))
            cs = slice(DSA_HEAD * h, DSA_HEAD * (h + 1))
            o_ref[0, :, cs] = (acc_scr[r, cs] / l_scr[r, :]).astype(o_ref.dtype)


def _dsa_s_attn(page_table, rel_bias, cache_k, cache_v, q, madd, k_new, v_new):
    b, n_pages = page_table.shape
    npc = n_pages // PAGES_PER_STEP
    rows = DSA_H * SROWS
    assert BUCKET_START[NUM_BUCKETS - 1] <= PAGE_SIZE

    def page_spec(j):
        return pl.BlockSpec(
            (1, DSA_H, DSA_HEAD, PAGE_SIZE),
            lambda bi, c, pt, j=j: (pt[bi, jnp.minimum(c, npc - 1) * PAGES_PER_STEP + j], 0, 0, 0))

    page_specs = [page_spec(j) for j in range(PAGES_PER_STEP)]
    grid_spec = pltpu.PrefetchScalarGridSpec(
        num_scalar_prefetch=1,
        grid=(b, npc + 1),
        in_specs=[pl.BlockSpec(memory_space=pltpu.SMEM),
                  pl.BlockSpec((1, SROWS, DSA_W), lambda bi, c, pt: (bi, 0, 0)),
                  pl.BlockSpec((1, SROWS, SCHUNK), lambda bi, c, pt: (bi, 0, c)),
                  pl.BlockSpec((1, SROWS, DSA_W), lambda bi, c, pt: (bi, 0, 0)),
                  pl.BlockSpec((1, SROWS, DSA_W), lambda bi, c, pt: (bi, 0, 0))]
                 + page_specs + page_specs,
        out_specs=pl.BlockSpec((1, SROWS, DSA_W), lambda bi, c, pt: (bi, 0, 0)),
        scratch_shapes=[pltpu.VMEM((rows, 1), F32), pltpu.VMEM((rows, 1), F32),
                        pltpu.VMEM((rows, DSA_W), F32)],
    )
    return pl.pallas_call(
        functools.partial(_dsa_s_attn_kernel, npc, n_pages * PAGE_SIZE),
        grid_spec=grid_spec,
        out_shape=jax.ShapeDtypeStruct((b, SROWS, DSA_W), BF16),
        compiler_params=_cparams("arbitrary", "arbitrary"),
        name="dsa_s_attn",
    )(page_table, rel_bias, q, madd, k_new, v_new,
      *([cache_k] * PAGES_PER_STEP), *([cache_v] * PAGES_PER_STEP))


def _mem_kv_kernel(x_ref, g_ref, w_ref, k_ref, v_ref):
    h = _rms(x_ref[...], g_ref[...]).astype(BF16)
    width = k_ref.shape[-1]
    k_ref[...] = _dot(h, w_ref[:, 0:width])
    v_ref[...] = _dot(h, w_ref[:, width:2 * width])


def _mem_kv(mem, g, wkv, tm):
    n, d = mem.shape
    width = wkv.shape[1] // 2
    return pl.pallas_call(
        _mem_kv_kernel,
        grid=(n // tm,),
        in_specs=[pl.BlockSpec((tm, d), lambda i: (i, 0)),
                  pl.BlockSpec((1, d), lambda i: (0, 0)),
                  pl.BlockSpec(wkv.shape, lambda i: (0, 0))],
        out_specs=[pl.BlockSpec((tm, width), lambda i: (i, 0))] * 2,
        out_shape=[jax.ShapeDtypeStruct((n, width), F32)] * 2,
        compiler_params=_cparams("arbitrary"),
        name="mem_kv",
    )(mem, g, wkv)


def _split_bf16(x):
    hi = x.astype(BF16)
    return hi, (x - hi.astype(F32)).astype(BF16)


def _mix_kernel(x_ref, yrw_ref, ydsa_ref, mk_ref, mv_ref, woa_ref, wob_ref, gx_ref, wq_ref, wo_ref,
                gf_ref, rwh_ref, rwl_ref, rb_ref, x2_ref, h3_ref, comb_ref):
    x1 = x_ref[0] + _dot(yrw_ref[0], woa_ref[...]) + _dot(ydsa_ref[0], wob_ref[...])
    h2 = _rms(x1, gx_ref[...]).astype(BF16)
    qx = _dot(h2, wq_ref[...])
    mk = mk_ref[0].astype(BF16)
    mv = mv_ref[0].astype(BF16)
    heads = []
    for h in range(XA_H):
        sl = slice(XA_HEAD * h, XA_HEAD * (h + 1))
        lg = _dot_nt(qx[:, sl].astype(BF16), mk[:, sl]) * (XA_HEAD ** -0.5)
        p = jnp.exp(lg - jnp.max(lg, axis=1, keepdims=True))
        l = jnp.sum(p, axis=1, keepdims=True)
        heads.append(_dot(p.astype(BF16), mv[:, sl]) / l)
    o = jnp.concatenate(heads, axis=1).astype(BF16)
    x2 = x1 + _dot(o, wo_ref[...])
    x2_ref[0] = x2
    h3 = _rms(x2, gf_ref[...])
    h3_ref[0] = h3.astype(BF16)

    hh, hl = _split_bf16(h3)
    logits = (_dot(hh, rwh_ref[...]) + _dot(hh, rwl_ref[...]) + _dot(hl, rwh_ref[...])
              + rb_ref[...])
    lane = lax.broadcasted_iota(I32, logits.shape, 1)
    work = logits
    vals, idxs = [], []
    for _ in range(TOP_K):
        mx = jnp.max(work, axis=1, keepdims=True)
        ix = jnp.min(jnp.where(work == mx, lane, N_EXPERTS), axis=1, keepdims=True)
        vals.append(mx)
        idxs.append(ix)
        work = jnp.where(lane == ix, -jnp.inf, work)
    es = [jnp.exp(v - vals[0]) for v in vals]
    den = es[0] + es[1] + es[2] + es[3]
    comb = jnp.zeros(logits.shape, F32)
    for e, ix in zip(es, idxs):
        comb = comb + jnp.where(lane == ix, e / den, 0.0)
    comb_ref[0] = comb


def _mix(x, yrw, ydsa, mk, mv, W, tm):
    b, t, d = x.shape
    mem_len, xa_w = mk.shape[1], mk.shape[2]
    full = lambda a: pl.BlockSpec(a.shape, lambda i, j: (0,) * a.ndim)
    rspec = lambda w: pl.BlockSpec((1, tm, w), lambda i, j: (i, j, 0))
    mspec = pl.BlockSpec((1, mem_len, xa_w), lambda i, j: (i, 0, 0))
    params = [W['woa'], W['wob'], W['gx'], W['wq'], W['wo'], W['gf'], W['rwh'], W['rwl'], W['rb']]
    return pl.pallas_call(
        _mix_kernel,
        grid=(b, t // tm),
        in_specs=[rspec(d), rspec(RW_W), rspec(DSA_W), mspec, mspec] + [full(a) for a in params],
        out_specs=[rspec(d), rspec(d), rspec(N_EXPERTS)],
        out_shape=[jax.ShapeDtypeStruct((b, t, d), F32), jax.ShapeDtypeStruct((b, t, d), BF16),
                   jax.ShapeDtypeStruct((b, t, N_EXPERTS), F32)],
        compiler_params=_cparams("arbitrary", "arbitrary"),
        name="mix",
    )(x, yrw, ydsa, mk, mv, *params)


def _moe_kernel(h_ref, comb_ref, x_ref, w1_ref, b1_ref, w2_ref, b2_ref, gfin_ref, o_ref, acc_scr):
    e = pl.program_id(1)
    d_ff = w2_ref.shape[1]

    @pl.when(e == 0)
    def _():
        acc_scr[...] = jnp.zeros(acc_scr.shape, F32)

    h = h_ref[...]
    gate = jnp.minimum(_dot(h, w1_ref[0, :, 0:d_ff]) + b1_ref[0, :, 0:d_ff], SWIGLU_LIMIT)
    up = jnp.clip(_dot(h, w1_ref[0, :, d_ff:2 * d_ff]) + b1_ref[0, :, d_ff:2 * d_ff],
                  -SWIGLU_LIMIT, SWIGLU_LIMIT)
    glu = gate * _sigmoid(SWIGLU_ALPHA * gate)
    out = _dot(((up + 1.0) * glu).astype(BF16), w2_ref[0]) + b2_ref[0]
    comb = comb_ref[...]
    lane = lax.broadcasted_iota(I32, comb.shape, 1)
    wcol = jnp.sum(jnp.where(lane == e, comb, 0.0), axis=1, keepdims=True)
    acc_scr[...] += wcol * out

    @pl.when(e == pl.num_programs(1) - 1)
    def _():
        o_ref[...] = _rms(x_ref[...] + acc_scr[...], gfin_ref[...])


def _moe(h3, comb, x2, w1, b1, w2, b2, gfin, tm):
    n, d = h3.shape
    n_exp, _, two_ff = w1.shape
    d_ff = two_ff // 2
    return pl.pallas_call(
        _moe_kernel,
        grid=(n // tm, n_exp),
        in_specs=[pl.BlockSpec((tm, d), lambda i, e: (i, 0)),
                  pl.BlockSpec((tm, n_exp), lambda i, e: (i, 0)),
                  pl.BlockSpec((tm, d), lambda i, e: (i, 0)),
                  pl.BlockSpec((1, d, two_ff), lambda i, e: (e, 0, 0)),
                  pl.BlockSpec((1, 1, two_ff), lambda i, e: (e, 0, 0)),
                  pl.BlockSpec((1, d_ff, d), lambda i, e: (e, 0, 0)),
                  pl.BlockSpec((1, 1, d), lambda i, e: (e, 0, 0)),
                  pl.BlockSpec((1, d), lambda i, e: (0, 0))],
        out_specs=pl.BlockSpec((tm, d), lambda i, e: (i, 0)),
        out_shape=jax.ShapeDtypeStruct((n, d), F32),
        scratch_shapes=[pltpu.VMEM((tm, d), F32)],
        compiler_params=_cparams("arbitrary", "arbitrary"),
        name="moe",
    )(h3, comb, x2, w1, b1, w2, b2, gfin)


def _pick_tile(n, pref):
    t = min(pref, n)
    while n % t:
        t //= 2
    return t


def kernel(x_prompt, mem_prompt, x_sample, cache_k, cache_v, cache_idx_k, cache_mem_k, cache_mem_v, state_rwkv, state_shift, page_table, rel_bias, norm_final, norm_mix, w_in, mu_shift, rw_w0, rw_w2, rw_a0, rw_a2, rw_g2, rw_k_k, rw_k_a, rw_r_k, rw_ln_w, rw_ln_b, w_out, norm_xattn, norm_mem, xa_wq, xa_wk, xa_wv, xa_wo, norm_ffn, router_w, router_b, moe_w1, moe_b1, moe_w2, moe_b2):
    depth = w_in.shape[0]
    assert depth == 1, "single-layer step"
    l = 0
    bp, tp, d = x_prompt.shape
    bs, ts, _ = x_sample.shape
    n_pool = cache_k.shape[1]
    n_pages = page_table.shape[1]
    past = n_pages * PAGE_SIZE
    mem_len = mem_prompt.shape[1]
    xa_w = XA_H * XA_HEAD
    rw_cols = state_shift.shape[-1]
    assert tp % CW == 0 and n_pages % PAGES_PER_STEP == 0 and ts <= SROWS

    row = lambda a: a.reshape(1, -1)
    bf = lambda a: a.astype(BF16)

    w_pad = bf(jnp.pad(w_in[l], ((0, 0), (0, C_END - w_in.shape[-1]))))
    lora = rw_w2.shape[1]
    P = {
        'mu': row(mu_shift[l]), 'w0': row(rw_w0[l]), 'a0': row(rw_a0[l]),
        'w2': bf(jnp.pad(rw_w2[l], ((0, LANES - lora), (0, 0)))),
        'a2': bf(jnp.pad(rw_a2[l], ((lora, LANES - lora - rw_a2.shape[1]), (0, 0)))),
        'g2': bf(rw_g2[l]), 'kk': row(rw_k_k[l]), 'ka': row(rw_k_a[l]),
        'rk': rw_r_k[l].reshape(RW_H, RW_HEAD),
        'lnw': rw_ln_w[l].reshape(RW_H, RW_HEAD), 'lnb': rw_ln_b[l].reshape(RW_H, RW_HEAD),
    }
    rwh = bf(router_w[l])
    W = {
        'woa': bf(w_out[l][:RW_W]), 'wob': bf(w_out[l][RW_W:]), 'gx': row(norm_xattn[l]),
        'wq': bf(xa_wq[l]), 'wo': bf(xa_wo[l]), 'gf': row(norm_ffn[l]),
        'rwh': rwh, 'rwl': bf(router_w[l] - rwh.astype(F32)), 'rb': row(router_b[l]),
    }
    w1 = bf(moe_w1[l])
    w2 = bf(moe_w2[l])
    b1 = moe_b1[l][:, None, :]
    b2 = moe_b2[l][:, None, :]
    gfin = row(norm_final)
    gmix = row(norm_mix[l])

    prw, q, k, v, qi, kiw, kT, vb, kiT = _in_proj(x_prompt, gmix, w_pad, _pick_tile(tp, 256))
    y_rw, p_s1, p_sh1 = _rwkv_chunk(prw, jnp.zeros((bp, 1, rw_cols), F32),
                                    jnp.zeros((bp, RW_H, RW_HEAD, RW_HEAD), F32), P,
                                    nb=bp, tc=_pick_tile(tp, 256), passes=1)
    y_dsa = _dsa_prompt(rel_bias, q, qi, kiw, kT, vb, kiT)
    wkv = bf(jnp.concatenate([xa_wk[l], xa_wv[l]], axis=1))
    mk_p, mv_p = _mem_kv(mem_prompt.reshape(bp * mem_len, d), row(norm_mem[l]), wkv,
                         _pick_tile(bp * mem_len, 256))
    mk_p = mk_p.reshape(bp, mem_len, xa_w)
    mv_p = mv_p.reshape(bp, mem_len, xa_w)
    x2, h3, comb = _mix(x_prompt, y_rw, y_dsa, mk_p, mv_p, W, _pick_tile(tp, 256))
    n_p = bp * tp
    y_prompt = _moe(h3.reshape(n_p, d), comb.reshape(n_p, N_EXPERTS), x2.reshape(n_p, d),
                    w1, b1, w2, b2, gfin, _pick_tile(n_p, 512)).reshape(bp, tp, d)

    xs = jnp.pad(x_sample, ((0, 0), (0, SROWS - ts), (0, 0)))
    n_s = bs * SROWS
    sprw, sq, sk, sv, sqi, skiw, _, svb, _ = _in_proj(xs.reshape(1, n_s, d), gmix, w_pad,
                                                      _pick_tile(n_s, 256))
    seq = lambda a: a.reshape(bs, SROWS, a.shape[-1])
    sprw, sq, sk, sv, sqi, skiw, svb = map(seq, (sprw, sq, sk, sv, sqi, skiw, svb))
    sy_rw, s_s1, s_sh1 = _rwkv(sprw, state_shift[l][:, None, :], state_rwkv[l], P,
                               nb=_pick_tile(bs, 4), tc=SROWS, n_valid=ts)

    qi_rows = sqi[:, :ts].reshape(bs, ts * IDX_H, IDX_D)
    wcol = skiw[:, :ts, IDX_D:IDX_D + IDX_H].reshape(bs, ts * IDX_H, 1)
    ki_new = bf(skiw[:, :, :IDX_D])
    scores = _dsa_s_scores(page_table, jnp.transpose(cache_idx_k[l], (0, 2, 1)), qi_rows, wcol,
                           ki_new, ts)
    width = scores.shape[-1]
    tok = jnp.arange(SROWS, dtype=I32)
    lim = jnp.tile(jnp.where(tok < ts, past + tok, -1), bs).reshape(n_s, 1)
    k_sel = min(TOPK_MAX, (past + ts) // 4)
    madd = _topk_rows(scores.reshape(n_s, width), lim, k_sel, _pick_tile(n_s, 64))
    madd = madd.reshape(bs, SROWS, width)
    sy_dsa = _dsa_s_attn(page_table, rel_bias,
                         jnp.transpose(cache_k[l], (0, 2, 3, 1)),
                         jnp.transpose(cache_v[l], (0, 2, 3, 1)),
                         sq, madd, bf(sk), svb)
    sx2, sh3, scomb = _mix(xs, sy_rw, sy_dsa, cache_mem_k[l].reshape(bs, mem_len, xa_w),
                           cache_mem_v[l].reshape(bs, mem_len, xa_w), W, SROWS)
    y_s = _moe(sh3.reshape(n_s, d), scomb.reshape(n_s, N_EXPERTS), sx2.reshape(n_s, d),
               w1, b1, w2, b2, gfin, _pick_tile(n_s, 256)).reshape(bs, SROWS, d)

    heads = lambda a, b_, t_: a.reshape(1, b_, t_, DSA_H, DSA_HEAD)
    return (y_prompt, y_s[:, :ts],
            p_s1[None], p_sh1.reshape(1, bp, rw_cols),
            heads(k, bp, tp), heads(v, bp, tp), kiw[:, :, :IDX_D][None],
            mk_p.reshape(1, bp, mem_len, XA_H, XA_HEAD), mv_p.reshape(1, bp, mem_len, XA_H, XA_HEAD),
            s_s1[None], s_sh1.reshape(1, bs, rw_cols),
            heads(sk[:, :ts], bs, ts), heads(sv[:, :ts], bs, ts), skiw[:, :ts, :IDX_D][None])
```

```python
import functools
import math

import numpy as np
import jax
import jax.numpy as jnp
from jax import lax
from jax.experimental import pallas as pl
from jax.experimental.pallas import tpu as pltpu

F32 = jnp.float32
BF16 = jnp.bfloat16
I32 = jnp.int32

LANES = 128
VMEM_LIMIT = 56 * 1024 * 1024

NORM_EPS = 1e-5
RW_EPS = 64e-5
NEG_INF = -1e30
RW_H = 8
RW_HEAD = 64
RW_W = RW_H * RW_HEAD
DSA_H = 8
DSA_HEAD = 64
DSA_W = DSA_H * DSA_HEAD
IDX_H = 8
IDX_D = 64
TOPK_MAX = 256
XA_H = 4
XA_HEAD = 128
N_EXPERTS = 32
TOP_K = 4
SWIGLU_LIMIT = 7.0
SWIGLU_ALPHA = 1.702
NUM_BUCKETS = 32
MAX_DISTANCE = 128
PAGE_SIZE = 128
IDX_SCALE = IDX_H ** -0.5 * IDX_D ** -0.5

QB = 128
CW = 512
ATTN_GROUP = 8
PAGES_PER_STEP = 16

INT_MIN = -2 ** 31
INT_MAX = 2 ** 31 - 1


def _bucket_starts():
    max_exact = NUM_BUCKETS // 2
    d = np.arange(0, 4 * MAX_DISTANCE)
    large = max_exact + (np.log(np.maximum(d, 1).astype(np.float32) / np.float32(max_exact))
                         / np.float32(math.log(MAX_DISTANCE / max_exact))
                         * np.float32(NUM_BUCKETS - max_exact)).astype(np.int32)
    b = np.where(d < max_exact, d, np.minimum(large, NUM_BUCKETS - 1))
    starts = []
    for j in range(NUM_BUCKETS):
        hit = np.nonzero(b == j)[0]
        starts.append(int(hit[0]) if hit.size else None)
    return starts


BUCKET_START = _bucket_starts()


def _cparams(*sem):
    return pltpu.CompilerParams(dimension_semantics=sem, vmem_limit_bytes=VMEM_LIMIT)


def _rms(x, g):
    return x * lax.rsqrt(jnp.mean(x * x, axis=-1, keepdims=True) + NORM_EPS) * g


def _dot(a, b):
    return jnp.dot(a, b, preferred_element_type=F32)


def _dot_nt(a, b):
    return lax.dot_general(a, b, (((1,), (1,)), ((), ())), preferred_element_type=F32)


def _softplus(x):
    return jnp.maximum(x, 0.0) + jnp.log1p(jnp.exp(-jnp.abs(x)))


def _sigmoid(x):
    return 1.0 / (1.0 + jnp.exp(-x))


C_RW = 0
C_Q = 1792
C_K = C_Q + DSA_W
C_V = C_K + DSA_W
C_QI = C_V + DSA_W
C_KIW = C_QI + IDX_H * IDX_D
C_END = C_KIW + LANES


def _in_proj_kernel(x_ref, g_ref, w_ref, prw_ref, q_ref, k_ref, v_ref, qi_ref, kiw_ref,
                    kT_ref, vb_ref, kiT_ref):
    h = _rms(x_ref[0], g_ref[...]).astype(BF16)

    def mm(lo, hi):
        return _dot(h, w_ref[:, lo:hi])

    prw_ref[0] = mm(C_RW, C_Q)
    q_ref[0] = mm(C_Q, C_K).astype(BF16)
    k = mm(C_K, C_V)
    k_ref[0] = k
    kT_ref[0] = k.T.astype(BF16)
    v = mm(C_V, C_QI)
    v_ref[0] = v
    vb_ref[0] = v.astype(BF16)
    qi_ref[0] = mm(C_QI, C_KIW).astype(BF16)
    kiw = mm(C_KIW, C_END)
    kiw_ref[0] = kiw
    kiT_ref[0] = kiw.T[0:IDX_D, :].astype(BF16)


def _in_proj(x, g, w_pad, tm):
    b, t, d = x.shape
    grid = (b, t // tm)
    row = lambda w, dt: jax.ShapeDtypeStruct((b, t, w), dt)
    rspec = lambda w: pl.BlockSpec((1, tm, w), lambda i, j: (i, j, 0))
    tspec = lambda w: pl.BlockSpec((1, w, tm), lambda i, j: (i, 0, j))
    return pl.pallas_call(
        _in_proj_kernel,
        grid=grid,
        in_specs=[rspec(d),
                  pl.BlockSpec((1, d), lambda i, j: (0, 0)),
                  pl.BlockSpec((d, C_END), lambda i, j: (0, 0))],
        out_specs=[rspec(C_Q), rspec(DSA_W), rspec(DSA_W), rspec(DSA_W), rspec(IDX_H * IDX_D),
                   rspec(LANES), tspec(DSA_W), rspec(DSA_W), tspec(IDX_D)],
        out_shape=[row(C_Q, F32), row(DSA_W, BF16), row(DSA_W, F32), row(DSA_W, F32),
                   row(IDX_H * IDX_D, BF16), row(LANES, F32),
                   jax.ShapeDtypeStruct((b, DSA_W, t), BF16), row(DSA_W, BF16),
                   jax.ShapeDtypeStruct((b, IDX_D, t), BF16)],
        compiler_params=_cparams("arbitrary", "arbitrary"),
        name="in_proj",
    )(x, g, w_pad)


def _rwkv_kernel(n_steps, prw_ref, sh0_ref, s0_ref, mu_ref, w0_ref, w2_ref, a0_ref, a2_ref, g2_ref,
                 kkw_ref, kaw_ref, rk_ref, lnw_ref, lnb_ref,
                 y_ref, s1_ref, sh1_ref,
                 st_scr, carry_scr, r_s, w_s, k_s, v_s, kk_s, b_s, y_s, g_s):
    nb, tc = prw_ref.shape[0], prw_ref.shape[1]
    c = pl.program_id(1)

    @pl.when(c == 0)
    def _():
        st_scr[...] = s0_ref[...]
        carry_scr[...] = sh0_ref[...]

    row = lax.broadcasted_iota(I32, (tc, 1), 0)
    lane = lax.broadcasted_iota(I32, (1, LANES), 1)
    for b in range(nb):
        x = prw_ref[b]
        prev = jnp.where(row == 0, carry_scr[b], pltpu.roll(x, 1, 0))
        carry_scr[b] = x[n_steps - 1:n_steps, :]
        ps = x + (prev - x) * mu_ref[...]
        r = ps[:, 0:RW_W]
        k = ps[:, RW_W:2 * RW_W]
        v = ps[:, 2 * RW_W:3 * RW_W]
        la = ps[:, 3 * RW_W:3 * RW_W + LANES]
        gd = ps[:, 3 * RW_W + LANES:3 * RW_W + 2 * LANES]
        z = jnp.where(lane < 64, jnp.tanh(la), la).astype(BF16)
        w_log = -_softplus(-(w0_ref[...] + _dot(z, w2_ref[...]))) - 0.5
        decay = jnp.exp(-jnp.exp(w_log))
        a = _sigmoid(a0_ref[...] + _dot(z, a2_ref[...]))
        g_s[b] = _dot(_sigmoid(gd).astype(BF16), g2_ref[...])
        kk = k * kkw_ref[...]
        k2 = k * (1.0 + (a - 1.0) * kaw_ref[...])
        for h in range(RW_H):
            sl = slice(RW_HEAD * h, RW_HEAD * (h + 1))
            kkh = kk[:, sl]
            nrm = jnp.sqrt(jnp.sum(kkh * kkh, axis=-1, keepdims=True))
            kkh = kkh / jnp.maximum(nrm, 1e-12)
            r_s[b, h] = r[:, sl]
            w_s[b, h] = decay[:, sl]
            k_s[b, h] = k2[:, sl]
            v_s[b, h] = v[:, sl]
            kk_s[b, h] = kkh
            b_s[b, h] = kkh * a[:, sl]

    eye = (lax.broadcasted_iota(I32, (RW_HEAD, RW_HEAD), 0)
           == lax.broadcasted_iota(I32, (RW_HEAD, RW_HEAD), 1))

    def step(t, carry):
        for b in range(nb):
            for h in range(RW_H):
                s = st_scr[b, h]
                tt = pl.ds(t, 1)
                sa = -jnp.sum(s * kk_s[b, h, tt, :], axis=1, keepdims=True)
                vcol = jnp.sum(jnp.where(eye, v_s[b, h, tt, :], 0.0), axis=1, keepdims=True)
                s = s * w_s[b, h, tt, :] + sa * b_s[b, h, tt, :] + vcol * k_s[b, h, tt, :]
                st_scr[b, h] = s
                ycol = jnp.sum(s * r_s[b, h, tt, :], axis=1, keepdims=True)
                y_s[b, h, tt, :] = jnp.sum(jnp.where(eye, ycol, 0.0), axis=0, keepdims=True)
        return carry

    lax.fori_loop(0, n_steps, step, 0)

    for b in range(nb):
        for h in range(RW_H):
            sl = slice(RW_HEAD * h, RW_HEAD * (h + 1))
            y = y_s[b, h]
            mean = jnp.mean(y, axis=-1, keepdims=True)
            var = jnp.mean(jnp.square(y - mean), axis=-1, keepdims=True)
            yn = (y - mean) * lax.rsqrt(var + RW_EPS) * lnw_ref[h:h + 1, :] + lnb_ref[h:h + 1, :]
            rr, kk2, vv = r_s[b, h], k_s[b, h], v_s[b, h]
            bonus = jnp.sum(rr * kk2 * rk_ref[h:h + 1, :], axis=-1, keepdims=True) * vv
            y_ref[b, :, sl] = ((yn + bonus) * g_s[b][:, sl]).astype(BF16)
    s1_ref[...] = st_scr[...]
    sh1_ref[...] = carry_scr[...]


def _rwkv(prw, shift0, s0, P, nb, tc, n_valid):
    b, t, cols = prw.shape
    n_steps = min(tc, n_valid)
    grid = (b // nb, t // tc)
    full = lambda a: pl.BlockSpec(a.shape, lambda i, j: (0,) * a.ndim)
    params = [P['mu'], P['w0'], P['w2'], P['a0'], P['a2'], P['g2'], P['kk'], P['ka'], P['rk'],
              P['lnw'], P['lnb']]
    hs = lambda: pltpu.VMEM((nb, RW_H, tc, RW_HEAD), F32)
    return pl.pallas_call(
        functools.partial(_rwkv_kernel, n_steps),
        grid=grid,
        in_specs=[pl.BlockSpec((nb, tc, cols), lambda i, j: (i, j, 0)),
                  pl.BlockSpec((nb, 1, cols), lambda i, j: (i, 0, 0)),
                  pl.BlockSpec((nb, RW_H, RW_HEAD, RW_HEAD), lambda i, j: (i, 0, 0, 0))]
                 + [full(a) for a in params],
        out_specs=[pl.BlockSpec((nb, tc, RW_W), lambda i, j: (i, j, 0)),
                   pl.BlockSpec((nb, RW_H, RW_HEAD, RW_HEAD), lambda i, j: (i, 0, 0, 0)),
                   pl.BlockSpec((nb, 1, cols), lambda i, j: (i, 0, 0))],
        out_shape=[jax.ShapeDtypeStruct((b, t, RW_W), BF16),
                   jax.ShapeDtypeStruct((b, RW_H, RW_HEAD, RW_HEAD), F32),
                   jax.ShapeDtypeStruct((b, 1, cols), F32)],
        scratch_shapes=[pltpu.VMEM((nb, RW_H, RW_HEAD, RW_HEAD), F32),
                        pltpu.VMEM((nb, 1, cols), F32),
                        hs(), hs(), hs(), hs(), hs(), hs(), hs(),
                        pltpu.VMEM((nb, tc, RW_W), F32)],
        compiler_params=_cparams("arbitrary", "arbitrary"),
        name="rwkv",
    )(prw, shift0, s0, *params)


RC = 64
RC_GROUP = 16


def _split3(x):
    hi = x.astype(BF16)
    r1 = x - hi.astype(F32)
    mid = r1.astype(BF16)
    lo = (r1 - mid.astype(F32)).astype(BF16)
    return hi, mid, lo


def _mm(a, b, passes):
    if passes == 1:
        return _dot(a.astype(BF16), b.astype(BF16))
    ah, al = _split_bf16(a)
    bh, bl = _split_bf16(b)
    return _dot(ah, bh) + _dot(ah, bl) + _dot(al, bh)


def _mm_nt(a, b, passes):
    if passes == 1:
        return _dot_nt(a.astype(BF16), b.astype(BF16))
    ah, al = _split_bf16(a)
    bh, bl = _split_bf16(b)
    return _dot_nt(ah, bh) + _dot_nt(ah, bl) + _dot_nt(al, bh)


def _exact_left(m_bf16, x):
    hi, mid, lo = _split3(x)
    return _dot(m_bf16, hi) + _dot(m_bf16, mid) + _dot(m_bf16, lo)


def _exact_right(x, m_bf16):
    hi, mid, lo = _split3(x)
    return _dot(hi, m_bf16) + _dot(mid, m_bf16) + _dot(lo, m_bf16)


def _rwkv_chunk_kernel(passes, prw_ref, sh0_ref, s0_ref, mu_ref, w0_ref, w2_ref, a0_ref, a2_ref,
                       g2_ref, kkw_ref, kaw_ref, rk_ref, lnw_ref, lnb_ref, ltri_ref, lones_ref,
                       hones_ref, y_ref, s1_ref, sh1_ref,
                       st_scr, carry_scr, rh_s, kkh_s, bt_s, kt_s, bg_s, kg_s, v_s, gc_s, y_s,
                       bonus_s, g_s):
    nb, tc = prw_ref.shape[0], prw_ref.shape[1]
    nh = nb * RW_H
    c = pl.program_id(1)

    @pl.when(c == 0)
    def _():
        for b in range(nb):
            for h in range(RW_H):
                st_scr[b * RW_H + h] = s0_ref[b, h]
        carry_scr[...] = sh0_ref[...]

    row = lax.broadcasted_iota(I32, (tc, 1), 0)
    lane = lax.broadcasted_iota(I32, (1, LANES), 1)
    hones = hones_ref[...]
    for b in range(nb):
        x = prw_ref[b]
        prev = jnp.where(row == 0, carry_scr[b], pltpu.roll(x, 1, 0))
        carry_scr[b] = x[tc - 1:tc, :]
        ps = x + (prev - x) * mu_ref[...]
        r = ps[:, 0:RW_W]
        k = ps[:, RW_W:2 * RW_W]
        v = ps[:, 2 * RW_W:3 * RW_W]
        la = ps[:, 3 * RW_W:3 * RW_W + LANES]
        gd = ps[:, 3 * RW_W + LANES:3 * RW_W + 2 * LANES]
        z = jnp.where(lane < 64, jnp.tanh(la), la).astype(BF16)
        w_log = -_softplus(-(w0_ref[...] + _dot(z, w2_ref[...]))) - 0.5
        lw = -jnp.exp(w_log)
        a = _sigmoid(a0_ref[...] + _dot(z, a2_ref[...]))
        g_s[b] = _dot(_sigmoid(gd).astype(BF16), g2_ref[...])
        kk = k * kkw_ref[...]
        kk = kk / jnp.maximum(jnp.sqrt(_exact_right(kk * kk, hones)), 1e-12)
        k2 = k * (1.0 + (a - 1.0) * kaw_ref[...])
        bb = kk * a
        bonus_s[b] = _exact_right(r * k2 * rk_ref[...], hones) * v
        gsum = _exact_left(ltri_ref[...], lw)
        gend = _exact_left(lones_ref[...], lw)
        e_neg = jnp.exp(-gsum)
        e_end = jnp.exp(gend - gsum)
        cols = {'rh': r * jnp.exp(gsum), 'kkh': kk * jnp.exp(gsum - lw), 'bt': bb * e_neg,
                'kt': k2 * e_neg, 'bg': bb * e_end, 'kg': k2 * e_end, 'v': v, 'gc': jnp.exp(gend)}
        dst = {'rh': rh_s, 'kkh': kkh_s, 'bt': bt_s, 'kt': kt_s, 'bg': bg_s, 'kg': kg_s, 'v': v_s,
               'gc': gc_s}
        for name, val in cols.items():
            for h in range(RW_H):
                dst[name][b * RW_H + h] = val[:, RW_HEAD * h:RW_HEAD * (h + 1)]

    ri = lax.broadcasted_iota(I32, (RC, RC), 0)
    ci = lax.broadcasted_iota(I32, (RC, RC), 1)
    strict = ri > ci
    incl = ri >= ci
    eye = ri == ci
    eye_f = jnp.where(eye, 1.0, 0.0)
    n_double = int(math.log2(RC)) - 1

    def head_group(heads, sub):
        base = pl.multiple_of(sub * RC, RC)
        rows = pl.ds(base, RC)
        mm = lambda a, b: _mm(a, b, passes)
        each = lambda f: [f(i) for i in range(len(heads))]
        rh = [rh_s[hd, rows, :] for hd in heads]
        kkh = [kkh_s[hd, rows, :] for hd in heads]
        vv = [v_s[hd, rows, :] for hd in heads]
        gram = each(lambda i: _mm_nt(
            jnp.concatenate([kkh[i], rh[i]], axis=0),
            jnp.concatenate([bt_s[heads[i], rows, :], kt_s[heads[i], rows, :]], axis=0), passes))
        mb = each(lambda i: jnp.where(strict, gram[i][0:RC, 0:RC], 0.0))
        mkv = each(lambda i: mm(jnp.where(strict, gram[i][0:RC, RC:2 * RC], 0.0), vv[i]))
        nkv = each(lambda i: mm(jnp.where(incl, gram[i][RC:2 * RC, RC:2 * RC], 0.0), vv[i]))
        nb_ = each(lambda i: jnp.where(incl, gram[i][RC:2 * RC, 0:RC], 0.0))
        inv = each(lambda i: eye_f - mb[i])
        q = mb
        for _ in range(n_double):
            q = each(lambda i: mm(q[i], q[i]))
            inv = each(lambda i: inv[i] + mm(inv[i], q[i]))
        u = each(lambda i: -mm(inv[i], jnp.concatenate([kkh[i], mkv[i]], axis=1)))
        yy = each(lambda i: mm(nb_[i], u[i]))
        zz = each(lambda i: mm(u[i].T, bg_s[heads[i], rows, :]))
        vk = each(lambda i: mm(vv[i].T, kg_s[heads[i], rows, :]))
        s = [st_scr[hd] for hd in heads]
        ys = each(lambda i: _mm_nt(rh[i] + yy[i][:, 0:RC], s[i], passes))
        z1 = each(lambda i: jnp.where(eye, gc_s[heads[i], pl.ds(base, 1), :], 0.0)
                  + zz[i][0:RW_HEAD])
        sn = each(lambda i: mm(s[i], z1[i]))
        for i, hd in enumerate(heads):
            y_s[hd, rows, :] = ys[i] + yy[i][:, RC:2 * RC] + nkv[i]
            st_scr[hd] = sn[i] + zz[i][RW_HEAD:2 * RW_HEAD] + vk[i]

    def chunk_step(sub, carry):
        for g0 in range(0, nh, RC_GROUP):
            head_group(list(range(g0, min(g0 + RC_GROUP, nh))), sub)
        return carry

    lax.fori_loop(0, tc // RC, chunk_step, 0)

    for b in range(nb):
        for h in range(RW_H):
            sl = slice(RW_HEAD * h, RW_HEAD * (h + 1))
            y = y_s[b * RW_H + h]
            mean = jnp.mean(y, axis=-1, keepdims=True)
            var = jnp.mean(jnp.square(y - mean), axis=-1, keepdims=True)
            yn = (y - mean) * lax.rsqrt(var + RW_EPS) * lnw_ref[h:h + 1, :] + lnb_ref[h:h + 1, :]
            y_ref[b, :, sl] = ((yn + bonus_s[b, :, sl]) * g_s[b, :, sl]).astype(BF16)
            s1_ref[b, h] = st_scr[b * RW_H + h]
    sh1_ref[...] = carry_scr[...]


def _rwkv_chunk(prw, shift0, s0, P, nb, tc, passes):
    b, t, cols = prw.shape
    tok = np.arange(tc)
    same = (tok[:, None] // RC) == (tok[None, :] // RC)
    ltri = jnp.asarray(same & (tok[:, None] >= tok[None, :]), BF16)
    lones = jnp.asarray(same, BF16)
    col = np.arange(RW_W)
    hones = jnp.asarray((col[:, None] // RW_HEAD) == (col[None, :] // RW_HEAD), BF16)
    full = lambda a: pl.BlockSpec(a.shape, lambda i, j: (0,) * a.ndim)
    params = [P['mu'], P['w0'], P['w2'], P['a0'], P['a2'], P['g2'], P['kk'], P['ka'],
              P['rk'].reshape(1, RW_W), P['lnw'], P['lnb'], ltri, lones, hones]
    nh = nb * RW_H
    hs = lambda: pltpu.VMEM((nh, tc, RW_HEAD), F32)
    return pl.pallas_call(
        functools.partial(_rwkv_chunk_kernel, passes),
        grid=(b // nb, t // tc),
        in_specs=[pl.BlockSpec((nb, tc, cols), lambda i, j: (i, j, 0)),
                  pl.BlockSpec((nb, 1, cols), lambda i, j: (i, 0, 0)),
                  pl.BlockSpec((nb, RW_H, RW_HEAD, RW_HEAD), lambda i, j: (i, 0, 0, 0))]
                 + [full(a) for a in params],
        out_specs=[pl.BlockSpec((nb, tc, RW_W), lambda i, j: (i, j, 0)),
                   pl.BlockSpec((nb, RW_H, RW_HEAD, RW_HEAD), lambda i, j: (i, 0, 0, 0)),
                   pl.BlockSpec((nb, 1, cols), lambda i, j: (i, 0, 0))],
        out_shape=[jax.ShapeDtypeStruct((b, t, RW_W), BF16),
                   jax.ShapeDtypeStruct((b, RW_H, RW_HEAD, RW_HEAD), F32),
                   jax.ShapeDtypeStruct((b, 1, cols), F32)],
        scratch_shapes=[pltpu.VMEM((nh, RW_HEAD, RW_HEAD), F32),
                        pltpu.VMEM((nb, 1, cols), F32),
                        hs(), hs(), hs(), hs(), hs(), hs(), hs(), hs(), hs(),
                        pltpu.VMEM((nb, tc, RW_W), F32), pltpu.VMEM((nb, tc, RW_W), F32)],
        compiler_params=_cparams("arbitrary", "arbitrary"),
        name="rwkv_chunk",
    )(prw, shift0, s0, *params)


def _pattern_to_f32(u):
    key = u ^ INT_MIN
    return lax.bitcast_convert_type(jnp.where(key >= 0, key, key ^ 0x7FFFFFFF), F32)


def _topk_mask(sc_ref, madd_ref, p_scr, rows, nch, k_sel, lim, idx_bits):
    ntile = CW // LANES
    lane = lax.broadcasted_iota(I32, (rows, LANES), 1)

    def count(pred):
        def body(c, acc):
            off = pl.multiple_of(c * CW, CW)
            x = sc_ref[:, pl.ds(off, CW)]
            for t in range(ntile):
                acc = acc + pred(x[:, LANES * t:LANES * (t + 1)], off + LANES * t)
            return acc
        acc = lax.fori_loop(0, nch, body, jnp.zeros((rows, LANES), F32))
        return jnp.sum(acc, axis=1, keepdims=True)

    def bcast(col):
        return jnp.broadcast_to(col, (rows, LANES))

    def searching(state):
        it, alive, _, _ = state
        return (it < 32) & (alive > 0)

    def value_bit(state):
        it, _, tau, cnt_tau = state
        cand = tau | jnp.left_shift(jnp.int32(1), 31 - it)
        cb = bcast(_pattern_to_f32(cand))
        cnt = count(lambda x, off: jnp.where(x >= cb, 1.0, 0.0))
        take = cnt >= k_sel
        tau = jnp.where(take, cand, tau)
        cnt_tau = jnp.where(take, cnt, cnt_tau)
        settled = jnp.where(cnt_tau == k_sel, 1, jnp.where(lim < 0, 1, 0))
        return it + 1, 1 - jnp.min(settled), tau, cnt_tau

    _, _, tau, _ = lax.while_loop(
        searching, value_bit,
        (jnp.int32(0), jnp.int32(1), jnp.zeros((rows, 1), I32),
         jnp.zeros((rows, 1), F32) + jnp.asarray(nch * CW, F32)))
    thr = _pattern_to_f32(tau)
    thr_b = bcast(thr)
    cnt_gt = count(lambda x, off: jnp.where(x > thr_b, 1.0, 0.0))
    cnt_ge = count(lambda x, off: jnp.where(x >= thr_b, 1.0, 0.0))
    need = k_sel - cnt_gt
    tie = jnp.where(cnt_ge > k_sel, jnp.where(thr > NEG_INF, 1, 0), 0)

    p_scr[...] = jnp.full((rows, LANES), INT_MAX, I32)

    @pl.when(jnp.max(tie) > 0)
    def _():
        def index_bit(it, p):
            cand = p | jnp.left_shift(jnp.int32(1), idx_bits - 1 - it)
            cb = bcast(cand)
            cnt = count(lambda x, off: jnp.where(x == thr_b,
                                                 jnp.where(off + lane < cb, 1.0, 0.0), 0.0))
            return jnp.where(cnt < need, cand, p)
        p = lax.fori_loop(0, idx_bits, index_bit, jnp.zeros((rows, 1), I32))
        p_scr[...] = bcast(p)

    p_b = p_scr[...]
    lim_b = bcast(lim)

    def fin(c, carry):
        off = pl.multiple_of(c * CW, CW)
        x = sc_ref[:, pl.ds(off, CW)]
        for t in range(ntile):
            xt = x[:, LANES * t:LANES * (t + 1)]
            idx = off + LANES * t + lane
            sel = jnp.where(xt > thr_b, 0.0,
                            jnp.where(xt == thr_b, jnp.where(idx <= p_b, 0.0, NEG_INF), NEG_INF))
            madd_ref[:, pl.ds(pl.multiple_of(off + LANES * t, LANES), LANES)] = (
                jnp.where(idx <= lim_b, sel, NEG_INF))
        return carry

    lax.fori_loop(0, nch, fin, 0)


def _bias_table(rb_ref, h, d):
    bias = jnp.full(d.shape, rb_ref[0, h], F32)
    for j in range(1, NUM_BUCKETS):
        if BUCKET_START[j] is not None:
            bias = jnp.where(d >= BUCKET_START[j], rb_ref[j, h], bias)
    return bias


def _dsa_prompt_kernel(k_sel, idx_bits, rb_ref, q_ref, qi_ref, kiw_ref, kT_ref, vb_ref, kiT_ref,
                       o_ref, sc_scr, madd_scr, p_scr, btab_scr):
    i = pl.program_id(1)
    nch = ((i + 1) * QB + CW - 1) // CW

    @pl.when((pl.program_id(0) == 0) & (i == 0))
    def _():
        d = (lax.broadcasted_iota(I32, (QB, 2 * QB), 0)
             - lax.broadcasted_iota(I32, (QB, 2 * QB), 1) + QB)
        for h in range(DSA_H):
            btab_scr[h] = _bias_table(rb_ref, h, d) - rb_ref[NUM_BUCKETS - 1, h]

    tq = i * QB + lax.broadcasted_iota(I32, (QB, 1), 0)
    qi = qi_ref[0]
    wi = kiw_ref[0][:, IDX_D:IDX_D + IDX_H]
    qi_all = jnp.concatenate([qi[:, IDX_D * h:IDX_D * (h + 1)] for h in range(IDX_H)], axis=0)
    wi_h = [wi[:, h:h + 1] for h in range(IDX_H)]

    def score_chunk(c, carry):
        off = pl.multiple_of(c * CW, CW)
        s = _dot(qi_all, kiT_ref[0, :, pl.ds(off, CW)])
        acc = jnp.zeros((QB, CW), F32)
        for h in range(IDX_H):
            acc = acc + jnp.maximum(s[QB * h:QB * (h + 1)], 0.0) * wi_h[h]
        spos = off + lax.broadcasted_iota(I32, (1, CW), 1)
        sc_scr[:, pl.ds(off, CW)] = jnp.where(spos <= tq, acc * IDX_SCALE + 0.0, NEG_INF)
        return carry

    lax.fori_loop(0, nch, score_chunk, 0)

    _topk_mask(sc_scr, madd_scr, p_scr, QB, nch, k_sel, tq, idx_bits)

    q = q_ref[0] * (DSA_HEAD ** -0.5)
    tiles = CW // QB
    hsl = [slice(DSA_HEAD * h, DSA_HEAD * (h + 1)) for h in range(DSA_H)]
    qh = [q[:, sl] for sl in hsl]

    def attend(heads, near, c, carry):
        n = len(heads)
        off = pl.multiple_of(c * CW, CW)
        madd = madd_scr[:, pl.ds(off, CW)]
        s = [_dot(qh[h], kT_ref[0, hsl[h], pl.ds(off, CW)]) + madd for h in heads]
        if near:
            for j, h in enumerate(heads):
                parts = []
                for t in range(tiles):
                    delta = i - (c * tiles + t)
                    parts.append(jnp.where(delta == 0, btab_scr[h, :, QB:2 * QB],
                                           jnp.where(delta == 1, btab_scr[h, :, 0:QB], 0.0)))
                s[j] = s[j] + jnp.concatenate(parts, axis=1)
        m_new = [jnp.maximum(carry[j][0], jnp.max(s[j], axis=1, keepdims=True)) for j in range(n)]
        p = [jnp.exp(s[j] - m_new[j]) for j in range(n)]
        pv = [_dot(p[j].astype(BF16), vb_ref[0, pl.ds(off, CW), hsl[h]])
              for j, h in enumerate(heads)]
        out = []
        for j in range(n):
            m, l, acc = carry[j]
            alpha = jnp.exp(m - m_new[j])
            out.append((m_new[j], l * alpha + jnp.sum(p[j], axis=1, keepdims=True),
                        acc * alpha + pv[j]))
        return tuple(out)

    n_far = jnp.maximum(i - 1, 0) // tiles
    for g0 in range(0, DSA_H, ATTN_GROUP):
        heads = list(range(g0, g0 + ATTN_GROUP))
        init = tuple((jnp.full((QB, 1), NEG_INF, F32), jnp.zeros((QB, 1), F32),
                      jnp.zeros((QB, DSA_HEAD), F32)) for _ in heads)
        carry = lax.fori_loop(0, n_far, functools.partial(attend, heads, False), init)
        carry = lax.fori_loop(n_far, nch, functools.partial(attend, heads, True), carry)
        for j, h in enumerate(heads):
            _, l, acc = carry[j]
            o_ref[0, :, hsl[h]] = (acc / l).astype(BF16)


def _dsa_prompt(rel_bias, q, qi, kiw, kT, vb, kiT):
    b, t, _ = q.shape
    k_sel = min(TOPK_MAX, t // 4)
    idx_bits = max(1, int(math.ceil(math.log2(t))))
    blk = lambda w: pl.BlockSpec((1, QB, w), lambda bi, i: (bi, i, 0))
    whole = lambda shape: pl.BlockSpec(shape, lambda bi, i: (bi, 0, 0))
    return pl.pallas_call(
        functools.partial(_dsa_prompt_kernel, k_sel, idx_bits),
        grid=(b, t // QB),
        in_specs=[pl.BlockSpec(memory_space=pltpu.SMEM),
                  blk(DSA_W), blk(IDX_H * IDX_D), blk(LANES),
                  whole((1, DSA_W, t)), whole((1, t, DSA_W)), whole((1, IDX_D, t))],
        out_specs=blk(DSA_W),
        out_shape=jax.ShapeDtypeStruct((b, t, DSA_W), BF16),
        scratch_shapes=[pltpu.VMEM((QB, t), F32), pltpu.VMEM((QB, t), F32),
                        pltpu.VMEM((QB, LANES), I32),
                        pltpu.VMEM((DSA_H, QB, 2 * QB), F32)],
        compiler_params=_cparams("arbitrary", "arbitrary"),
        name="dsa_prompt",
    )(rel_bias, q, qi, kiw, kT, vb, kiT)


SROWS = 8
SCHUNK = PAGES_PER_STEP * PAGE_SIZE


def _dsa_s_scores_kernel(n_tok, n_past_chunks, pt_ref, qi_ref, wcol_ref, kinew_ref, *rest):
    pages, out_ref = rest[:PAGES_PER_STEP], rest[PAGES_PER_STEP]
    c = pl.program_id(1)
    qi = qi_ref[0]
    wcol = wcol_ref[0]

    def head_sum(s):
        s = jnp.maximum(s, 0.0) * wcol
        return jnp.sum(s.reshape(n_tok, IDX_H, s.shape[-1]), axis=1) * IDX_SCALE + 0.0

    @pl.when(c < n_past_chunks)
    def _():
        kit = jnp.concatenate([pg[0] for pg in pages], axis=1).astype(BF16)
        out_ref[0, 0:n_tok, :] = head_sum(_dot(qi, kit))
        out_ref[0, n_tok:SROWS, :] = jnp.full((SROWS - n_tok, SCHUNK), NEG_INF, F32)

    @pl.when(c == n_past_chunks)
    def _():
        s = head_sum(_dot_nt(qi, kinew_ref[0]))
        tr = lax.broadcasted_iota(I32, (n_tok, SROWS), 0)
        tc = lax.broadcasted_iota(I32, (n_tok, SROWS), 1)
        out_ref[...] = jnp.full((1, SROWS, SCHUNK), NEG_INF, F32)
        out_ref[0, 0:n_tok, 0:SROWS] = jnp.where(tc <= tr, s, NEG_INF)


def _dsa_s_scores(page_table, cache_ki, qi_rows, wcol, ki_new, n_tok):
    b, n_pages = page_table.shape
    npc = n_pages // PAGES_PER_STEP
    page_specs = [
        pl.BlockSpec((1, IDX_D, PAGE_SIZE),
                     lambda bi, c, pt, j=j: (pt[bi, jnp.minimum(c, npc - 1) * PAGES_PER_STEP + j], 0, 0))
        for j in range(PAGES_PER_STEP)]
    rows = qi_rows.shape[1]
    grid_spec = pltpu.PrefetchScalarGridSpec(
        num_scalar_prefetch=1,
        grid=(b, npc + 1),
        in_specs=[pl.BlockSpec((1, rows, IDX_D), lambda bi, c, pt: (bi, 0, 0)),
                  pl.BlockSpec((1, rows, 1), lambda bi, c, pt: (bi, 0, 0)),
                  pl.BlockSpec((1, SROWS, IDX_D), lambda bi, c, pt: (bi, 0, 0))] + page_specs,
        out_specs=pl.BlockSpec((1, SROWS, SCHUNK), lambda bi, c, pt: (bi, 0, c)),
    )
    return pl.pallas_call(
        functools.partial(_dsa_s_scores_kernel, n_tok, npc),
        grid_spec=grid_spec,
        out_shape=jax.ShapeDtypeStruct((b, SROWS, (npc + 1) * SCHUNK), F32),
        compiler_params=_cparams("arbitrary", "arbitrary"),
        name="dsa_s_scores",
    )(page_table, qi_rows, wcol, ki_new, *([cache_ki] * PAGES_PER_STEP))


def _topk_rows_kernel(k_sel, idx_bits, sc_ref, lim_ref, madd_ref, p_scr):
    rows, width = sc_ref.shape
    _topk_mask(sc_ref, madd_ref, p_scr, rows, width // CW, k_sel, lim_ref[...], idx_bits)


def _topk_rows(scores, lim, k_sel, rows_per_step):
    n, width = scores.shape
    idx_bits = max(1, int(math.ceil(math.log2(width))))
    return pl.pallas_call(
        functools.partial(_topk_rows_kernel, k_sel, idx_bits),
        grid=(n // rows_per_step,),
        in_specs=[pl.BlockSpec((rows_per_step, width), lambda i: (i, 0)),
                  pl.BlockSpec((rows_per_step, 1), lambda i: (i, 0))],
        out_specs=pl.BlockSpec((rows_per_step, width), lambda i: (i, 0)),
        out_shape=jax.ShapeDtypeStruct((n, width), F32),
        scratch_shapes=[pltpu.VMEM((rows_per_step, LANES), I32)],
        compiler_params=_cparams("arbitrary"),
        name="topk_rows",
    )(scores, lim)


def _dsa_s_attn_kernel(n_past_chunks, past, pt_ref, rb_ref, q_ref, madd_ref, knew_ref, vnew_ref,
                       *rest):
    kp = rest[:PAGES_PER_STEP]
    vp = rest[PAGES_PER_STEP:2 * PAGES_PER_STEP]
    o_ref, m_scr, l_scr, acc_scr = rest[2 * PAGES_PER_STEP:]
    c = pl.program_id(1)
    rows = DSA_H * SROWS

    @pl.when(c == 0)
    def _():
        m_scr[...] = jnp.full(m_scr.shape, NEG_INF, F32)
        l_scr[...] = jnp.zeros(l_scr.shape, F32)
        acc_scr[...] = jnp.zeros(acc_scr.shape, F32)

    q = q_ref[0] * (DSA_HEAD ** -0.5)
    head_of_row = lax.broadcasted_iota(I32, (rows, DSA_W), 0) // SROWS
    head_of_col = lax.broadcasted_iota(I32, (rows, DSA_W), 1) // DSA_HEAD
    qblk = jnp.where(head_of_row == head_of_col, jnp.concatenate([q] * DSA_H, axis=0),
                     jnp.zeros((), BF16))

    def near_bias(width, key0):
        tok = lax.broadcasted_iota(I32, (SROWS, width), 0)
        d = past + tok - (key0 + lax.broadcasted_iota(I32, (SROWS, width), 1))
        return jnp.concatenate(
            [_bias_table(rb_ref, h, d) - rb_ref[NUM_BUCKETS - 1, h] for h in range(DSA_H)], axis=0)

    def update(s, pv_fn):
        m = m_scr[...]
        m_new = jnp.maximum(m, jnp.max(s, axis=1, keepdims=True))
        alpha = jnp.exp(m - m_new)
        p = jnp.exp(s - m_new)
        l_scr[...] = l_scr[...] * alpha + jnp.sum(p, axis=1, keepdims=True)
        acc_scr[...] = acc_scr[...] * alpha + pv_fn(p.astype(BF16))
        m_scr[...] = m_new

    def chunk_t(pages):
        return jnp.concatenate([pg[0].reshape(DSA_W, PAGE_SIZE) for pg in pages],
                               axis=1).astype(BF16)

    @pl.when(c < n_past_chunks)
    def _():
        s = _dot(qblk, chunk_t(kp)) + jnp.concatenate([madd_ref[0]] * DSA_H, axis=0)
        is_last = c == n_past_chunks - 1
        tail = SCHUNK - PAGE_SIZE
        nb_ = jnp.where(is_last, near_bias(PAGE_SIZE, past - PAGE_SIZE), 0.0)
        s = jnp.concatenate([s[:, 0:tail], s[:, tail:SCHUNK] + nb_], axis=1)
        vt = chunk_t(vp)
        update(s, lambda pb: _dot_nt(pb, vt))

    @pl.when(c == n_past_chunks)
    def _():
        s = (_dot_nt(qblk, knew_ref[0]) + jnp.concatenate([madd_ref[0, :, 0:SROWS]] * DSA_H, axis=0)
             + near_bias(SROWS, past))
        update(s, lambda pb: _dot(pb, vnew_ref[0]))
        for h in range(DSA_H):
            r = slice(SROWS * h, SROWS * (h + 1))
            cs = slice(DSA_HEAD * h, DSA_HEAD * (h + 1))
            o_ref[0, :, cs] = (acc_scr[r, cs] / l_scr[r, :]).astype(o_ref.dtype)


def _dsa_s_attn(page_table, rel_bias, cache_k, cache_v, q, madd, k_new, v_new):
    b, n_pages = page_table.shape
    npc = n_pages // PAGES_PER_STEP
    rows = DSA_H * SROWS
    assert BUCKET_START[NUM_BUCKETS - 1] <= PAGE_SIZE

    def page_spec(j):
        return pl.BlockSpec(
            (1, DSA_H, DSA_HEAD, PAGE_SIZE),
            lambda bi, c, pt, j=j: (pt[bi, jnp.minimum(c, npc - 1) * PAGES_PER_STEP + j], 0, 0, 0))

    page_specs = [page_spec(j) for j in range(PAGES_PER_STEP)]
    grid_spec = pltpu.PrefetchScalarGridSpec(
        num_scalar_prefetch=1,
        grid=(b, npc + 1),
        in_specs=[pl.BlockSpec(memory_space=pltpu.SMEM),
                  pl.BlockSpec((1, SROWS, DSA_W), lambda bi, c, pt: (bi, 0, 0)),
                  pl.BlockSpec((1, SROWS, SCHUNK), lambda bi, c, pt: (bi, 0, c)),
                  pl.BlockSpec((1, SROWS, DSA_W), lambda bi, c, pt: (bi, 0, 0)),
                  pl.BlockSpec((1, SROWS, DSA_W), lambda bi, c, pt: (bi, 0, 0))]
                 + page_specs + page_specs,
        out_specs=pl.BlockSpec((1, SROWS, DSA_W), lambda bi, c, pt: (bi, 0, 0)),
        scratch_shapes=[pltpu.VMEM((rows, 1), F32), pltpu.VMEM((rows, 1), F32),
                        pltpu.VMEM((rows, DSA_W), F32)],
    )
    return pl.pallas_call(
        functools.partial(_dsa_s_attn_kernel, npc, n_pages * PAGE_SIZE),
        grid_spec=grid_spec,
        out_shape=jax.ShapeDtypeStruct((b, SROWS, DSA_W), BF16),
        compiler_params=_cparams("arbitrary", "arbitrary"),
        name="dsa_s_attn",
    )(page_table, rel_bias, q, madd, k_new, v_new,
      *([cache_k] * PAGES_PER_STEP), *([cache_v] * PAGES_PER_STEP))


def _mem_kv_kernel(x_ref, g_ref, w_ref, k_ref, v_ref):
    h = _rms(x_ref[...], g_ref[...]).astype(BF16)
    width = k_ref.shape[-1]
    k_ref[...] = _dot(h, w_ref[:, 0:width])
    v_ref[...] = _dot(h, w_ref[:, width:2 * width])


def _mem_kv(mem, g, wkv, tm):
    n, d = mem.shape
    width = wkv.shape[1] // 2
    return pl.pallas_call(
        _mem_kv_kernel,
        grid=(n // tm,),
        in_specs=[pl.BlockSpec((tm, d), lambda i: (i, 0)),
                  pl.BlockSpec((1, d), lambda i: (0, 0)),
                  pl.BlockSpec(wkv.shape, lambda i: (0, 0))],
        out_specs=[pl.BlockSpec((tm, width), lambda i: (i, 0))] * 2,
        out_shape=[jax.ShapeDtypeStruct((n, width), F32)] * 2,
        compiler_params=_cparams("arbitrary"),
        name="mem_kv",
    )(mem, g, wkv)


def _split_bf16(x):
    hi = x.astype(BF16)
    return hi, (x - hi.astype(F32)).astype(BF16)


def _mix_kernel(x_ref, yrw_ref, ydsa_ref, mk_ref, mv_ref, woa_ref, wob_ref, gx_ref, wq_ref, wo_ref,
                gf_ref, rwh_ref, rwl_ref, rb_ref, x2_ref, h3_ref, comb_ref):
    x1 = x_ref[0] + _dot(yrw_ref[0], woa_ref[...]) + _dot(ydsa_ref[0], wob_ref[...])
    h2 = _rms(x1, gx_ref[...]).astype(BF16)
    qx = _dot(h2, wq_ref[...])
    mk = mk_ref[0].astype(BF16)
    mv = mv_ref[0].astype(BF16)
    heads = []
    for h in range(XA_H):
        sl = slice(XA_HEAD * h, XA_HEAD * (h + 1))
        lg = _dot_nt(qx[:, sl].astype(BF16), mk[:, sl]) * (XA_HEAD ** -0.5)
        p = jnp.exp(lg - jnp.max(lg, axis=1, keepdims=True))
        l = jnp.sum(p, axis=1, keepdims=True)
        heads.append(_dot(p.astype(BF16), mv[:, sl]) / l)
    o = jnp.concatenate(heads, axis=1).astype(BF16)
    x2 = x1 + _dot(o, wo_ref[...])
    x2_ref[0] = x2
    h3 = _rms(x2, gf_ref[...])
    h3_ref[0] = h3.astype(BF16)

    hh, hl = _split_bf16(h3)
    logits = (_dot(hh, rwh_ref[...]) + _dot(hh, rwl_ref[...]) + _dot(hl, rwh_ref[...])
              + rb_ref[...])
    lane = lax.broadcasted_iota(I32, logits.shape, 1)
    work = logits
    vals, idxs = [], []
    for _ in range(TOP_K):
        mx = jnp.max(work, axis=1, keepdims=True)
        ix = jnp.min(jnp.where(work == mx, lane, N_EXPERTS), axis=1, keepdims=True)
        vals.append(mx)
        idxs.append(ix)
        work = jnp.where(lane == ix, -jnp.inf, work)
    es = [jnp.exp(v - vals[0]) for v in vals]
    den = es[0] + es[1] + es[2] + es[3]
    comb = jnp.zeros(logits.shape, F32)
    for e, ix in zip(es, idxs):
        comb = comb + jnp.where(lane == ix, e / den, 0.0)
    comb_ref[0] = comb


def _mix(x, yrw, ydsa, mk, mv, W, tm):
    b, t, d = x.shape
    mem_len, xa_w = mk.shape[1], mk.shape[2]
    full = lambda a: pl.BlockSpec(a.shape, lambda i, j: (0,) * a.ndim)
    rspec = lambda w: pl.BlockSpec((1, tm, w), lambda i, j: (i, j, 0))
    mspec = pl.BlockSpec((1, mem_len, xa_w), lambda i, j: (i, 0, 0))
    params = [W['woa'], W['wob'], W['gx'], W['wq'], W['wo'], W['gf'], W['rwh'], W['rwl'], W['rb']]
    return pl.pallas_call(
        _mix_kernel,
        grid=(b, t // tm),
        in_specs=[rspec(d), rspec(RW_W), rspec(DSA_W), mspec, mspec] + [full(a) for a in params],
        out_specs=[rspec(d), rspec(d), rspec(N_EXPERTS)],
        out_shape=[jax.ShapeDtypeStruct((b, t, d), F32), jax.ShapeDtypeStruct((b, t, d), BF16),
                   jax.ShapeDtypeStruct((b, t, N_EXPERTS), F32)],
        compiler_params=_cparams("arbitrary", "arbitrary"),
        name="mix",
    )(x, yrw, ydsa, mk, mv, *params)


def _moe_kernel(h_ref, comb_ref, x_ref, w1_ref, b1_ref, w2_ref, b2_ref, gfin_ref, o_ref, acc_scr):
    e = pl.program_id(1)
    d_ff = w2_ref.shape[1]

    @pl.when(e == 0)
    def _():
        acc_scr[...] = jnp.zeros(acc_scr.shape, F32)

    h = h_ref[...]
    gate = jnp.minimum(_dot(h, w1_ref[0, :, 0:d_ff]) + b1_ref[0, :, 0:d_ff], SWIGLU_LIMIT)
    up = jnp.clip(_dot(h, w1_ref[0, :, d_ff:2 * d_ff]) + b1_ref[0, :, d_ff:2 * d_ff],
                  -SWIGLU_LIMIT, SWIGLU_LIMIT)
    glu = gate * _sigmoid(SWIGLU_ALPHA * gate)
    out = _dot(((up + 1.0) * glu).astype(BF16), w2_ref[0]) + b2_ref[0]
    comb = comb_ref[...]
    lane = lax.broadcasted_iota(I32, comb.shape, 1)
    wcol = jnp.sum(jnp.where(lane == e, comb, 0.0), axis=1, keepdims=True)
    acc_scr[...] += wcol * out

    @pl.when(e == pl.num_programs(1) - 1)
    def _():
        o_ref[...] = _rms(x_ref[...] + acc_scr[...], gfin_ref[...])


def _moe(h3, comb, x2, w1, b1, w2, b2, gfin, tm):
    n, d = h3.shape
    n_exp, _, two_ff = w1.shape
    d_ff = two_ff // 2
    return pl.pallas_call(
        _moe_kernel,
        grid=(n // tm, n_exp),
        in_specs=[pl.BlockSpec((tm, d), lambda i, e: (i, 0)),
                  pl.BlockSpec((tm, n_exp), lambda i, e: (i, 0)),
                  pl.BlockSpec((tm, d), lambda i, e: (i, 0)),
                  pl.BlockSpec((1, d, two_ff), lambda i, e: (e, 0, 0)),
                  pl.BlockSpec((1, 1, two_ff), lambda i, e: (e, 0, 0)),
                  pl.BlockSpec((1, d_ff, d), lambda i, e: (e, 0, 0)),
                  pl.BlockSpec((1, 1, d), lambda i, e: (e, 0, 0)),
                  pl.BlockSpec((1, d), lambda i, e: (0, 0))],
        out_specs=pl.BlockSpec((tm, d), lambda i, e: (i, 0)),
        out_shape=jax.ShapeDtypeStruct((n, d), F32),
        scratch_shapes=[pltpu.VMEM((tm, d), F32)],
        compiler_params=_cparams("arbitrary", "arbitrary"),
        name="moe",
    )(h3, comb, x2, w1, b1, w2, b2, gfin)


def _pick_tile(n, pref):
    t = min(pref, n)
    while n % t:
        t //= 2
    return t


def kernel(x_prompt, mem_prompt, x_sample, cache_k, cache_v, cache_idx_k, cache_mem_k, cache_mem_v, state_rwkv, state_shift, page_table, rel_bias, norm_final, norm_mix, w_in, mu_shift, rw_w0, rw_w2, rw_a0, rw_a2, rw_g2, rw_k_k, rw_k_a, rw_r_k, rw_ln_w, rw_ln_b, w_out, norm_xattn, norm_mem, xa_wq, xa_wk, xa_wv, xa_wo, norm_ffn, router_w, router_b, moe_w1, moe_b1, moe_w2, moe_b2):
    depth = w_in.shape[0]
    assert depth == 1, "single-layer step"
    l = 0
    bp, tp, d = x_prompt.shape
    bs, ts, _ = x_sample.shape
    n_pool = cache_k.shape[1]
    n_pages = page_table.shape[1]
    past = n_pages * PAGE_SIZE
    mem_len = mem_prompt.shape[1]
    xa_w = XA_H * XA_HEAD
    rw_cols = state_shift.shape[-1]
    assert tp % CW == 0 and n_pages % PAGES_PER_STEP == 0 and ts <= SROWS

    row = lambda a: a.reshape(1, -1)
    bf = lambda a: a.astype(BF16)

    w_pad = bf(jnp.pad(w_in[l], ((0, 0), (0, C_END - w_in.shape[-1]))))
    lora = rw_w2.shape[1]
    P = {
        'mu': row(mu_shift[l]), 'w0': row(rw_w0[l]), 'a0': row(rw_a0[l]),
        'w2': bf(jnp.pad(rw_w2[l], ((0, LANES - lora), (0, 0)))),
        'a2': bf(jnp.pad(rw_a2[l], ((lora, LANES - lora - rw_a2.shape[1]), (0, 0)))),
        'g2': bf(rw_g2[l]), 'kk': row(rw_k_k[l]), 'ka': row(rw_k_a[l]),
        'rk': rw_r_k[l].reshape(RW_H, RW_HEAD),
        'lnw': rw_ln_w[l].reshape(RW_H, RW_HEAD), 'lnb': rw_ln_b[l].reshape(RW_H, RW_HEAD),
    }
    rwh = bf(router_w[l])
    W = {
        'woa': bf(w_out[l][:RW_W]), 'wob': bf(w_out[l][RW_W:]), 'gx': row(norm_xattn[l]),
        'wq': bf(xa_wq[l]), 'wo': bf(xa_wo[l]), 'gf': row(norm_ffn[l]),
        'rwh': rwh, 'rwl': bf(router_w[l] - rwh.astype(F32)), 'rb': row(router_b[l]),
    }
    w1 = bf(moe_w1[l])
    w2 = bf(moe_w2[l])
    b1 = moe_b1[l][:, None, :]
    b2 = moe_b2[l][:, None, :]
    gfin = row(norm_final)
    gmix = row(norm_mix[l])

    prw, q, k, v, qi, kiw, kT, vb, kiT = _in_proj(x_prompt, gmix, w_pad, _pick_tile(tp, 256))
    y_rw, p_s1, p_sh1 = _rwkv_chunk(prw, jnp.zeros((bp, 1, rw_cols), F32),
                                    jnp.zeros((bp, RW_H, RW_HEAD, RW_HEAD), F32), P,
                                    nb=bp, tc=_pick_tile(tp, 256), passes=1)
    y_dsa = _dsa_prompt(rel_bias, q, qi, kiw, kT, vb, kiT)
    wkv = bf(jnp.concatenate([xa_wk[l], xa_wv[l]], axis=1))
    mk_p, mv_p = _mem_kv(mem_prompt.reshape(bp * mem_len, d), row(norm_mem[l]), wkv,
                         _pick_tile(bp * mem_len, 256))
    mk_p = mk_p.reshape(bp, mem_len, xa_w)
    mv_p = mv_p.reshape(bp, mem_len, xa_w)
    x2, h3, comb = _mix(x_prompt, y_rw, y_dsa, mk_p, mv_p, W, _pick_tile(tp, 256))
    n_p = bp * tp
    y_prompt = _moe(h3.reshape(n_p, d), comb.reshape(n_p, N_EXPERTS), x2.reshape(n_p, d),
                    w1, b1, w2, b2, gfin, _pick_tile(n_p, 512)).reshape(bp, tp, d)

    xs = jnp.pad(x_sample, ((0, 0), (0, SROWS - ts), (0, 0)))
    n_s = bs * SROWS
    sprw, sq, sk, sv, sqi, skiw, _, svb, _ = _in_proj(xs.reshape(1, n_s, d), gmix, w_pad,
                                                      _pick_tile(n_s, 256))
    seq = lambda a: a.reshape(bs, SROWS, a.shape[-1])
    sprw, sq, sk, sv, sqi, skiw, svb = map(seq, (sprw, sq, sk, sv, sqi, skiw, svb))
    sy_rw, s_s1, s_sh1 = _rwkv(sprw, state_shift[l][:, None, :], state_rwkv[l], P,
                               nb=_pick_tile(bs, 4), tc=SROWS, n_valid=ts)

    qi_rows = sqi[:, :ts].reshape(bs, ts * IDX_H, IDX_D)
    wcol = skiw[:, :ts, IDX_D:IDX_D + IDX_H].reshape(bs, ts * IDX_H, 1)
    ki_new = bf(skiw[:, :, :IDX_D])
    scores = _dsa_s_scores(page_table, jnp.transpose(cache_idx_k[l], (0, 2, 1)), qi_rows, wcol,
                           ki_new, ts)
    width = scores.shape[-1]
    tok = jnp.arange(SROWS, dtype=I32)
    lim = jnp.tile(jnp.where(tok < ts, past + tok, -1), bs).reshape(n_s, 1)
    k_sel = min(TOPK_MAX, (past + ts) // 4)
    madd = _topk_rows(scores.reshape(n_s, width), lim, k_sel, _pick_tile(n_s, 64))
    madd = madd.reshape(bs, SROWS, width)
    sy_dsa = _dsa_s_attn(page_table, rel_bias,
                         jnp.transpose(cache_k[l], (0, 2, 3, 1)),
                         jnp.transpose(cache_v[l], (0, 2, 3, 1)),
                         sq, madd, bf(sk), svb)
    sx2, sh3, scomb = _mix(xs, sy_rw, sy_dsa, cache_mem_k[l].reshape(bs, mem_len, xa_w),
                           cache_mem_v[l].reshape(bs, mem_len, xa_w), W, SROWS)
    y_s = _moe(sh3.reshape(n_s, d), scomb.reshape(n_s, N_EXPERTS), sx2.reshape(n_s, d),
               w1, b1, w2, b2, gfin, _pick_tile(n_s, 256)).reshape(bs, SROWS, d)

    heads = lambda a, b_, t_: a.reshape(1, b_, t_, DSA_H, DSA_HEAD)
    return (y_prompt, y_s[:, :ts],
            p_s1[None], p_sh1.reshape(1, bp, rw_cols),
            heads(k, bp, tp), heads(v, bp, tp), kiw[:, :, :IDX_D][None],
            mk_p.reshape(1, bp, mem_len, XA_H, XA_HEAD), mv_p.reshape(1, bp, mem_len, XA_H, XA_HEAD),
            s_s1[None], s_sh1.reshape(1, bs, rw_cols),
            heads(sk[:, :ts], bs, ts), heads(sv[:, :ts], bs, ts), skiw[:, :ts, :IDX_D][None])
```

```python
import functools
import math

import numpy as np
import jax
import jax.numpy as jnp
from jax import lax
from jax.experimental import pallas as pl
from jax.experimental.pallas import tpu as pltpu

F32 = jnp.float32
BF16 = jnp.bfloat16
I32 = jnp.int32

LANES = 128
VMEM_LIMIT = 56 * 1024 * 1024

NORM_EPS = 1e-5
RW_EPS = 64e-5
NEG_INF = -1e30
RW_H = 8
RW_HEAD = 64
RW_W = RW_H * RW_HEAD
DSA_H = 8
DSA_HEAD = 64
DSA_W = DSA_H * DSA_HEAD
IDX_H = 8
IDX_D = 64
TOPK_MAX = 256
XA_H = 4
XA_HEAD = 128
N_EXPERTS = 32
TOP_K = 4
SWIGLU_LIMIT = 7.0
SWIGLU_ALPHA = 1.702
NUM_BUCKETS = 32
MAX_DISTANCE = 128
PAGE_SIZE = 128
IDX_SCALE = IDX_H ** -0.5 * IDX_D ** -0.5

QB = 128
CW = 512
ATTN_GROUP = 8
PAGES_PER_STEP = 16

INT_MIN = -2 ** 31
INT_MAX = 2 ** 31 - 1


def _bucket_starts():
    max_exact = NUM_BUCKETS // 2
    d = np.arange(0, 4 * MAX_DISTANCE)
    large = max_exact + (np.log(np.maximum(d, 1).astype(np.float32) / np.float32(max_exact))
                         / np.float32(math.log(MAX_DISTANCE / max_exact))
                         * np.float32(NUM_BUCKETS - max_exact)).astype(np.int32)
    b = np.where(d < max_exact, d, np.minimum(large, NUM_BUCKETS - 1))
    starts = []
    for j in range(NUM_BUCKETS):
        hit = np.nonzero(b == j)[0]
        starts.append(int(hit[0]) if hit.size else None)
    return starts


BUCKET_START = _bucket_starts()


def _cparams(*sem):
    return pltpu.CompilerParams(dimension_semantics=sem, vmem_limit_bytes=VMEM_LIMIT)


def _rms(x, g):
    return x * lax.rsqrt(jnp.mean(x * x, axis=-1, keepdims=True) + NORM_EPS) * g


def _dot(a, b):
    return jnp.dot(a, b, preferred_element_type=F32)


def _dot_nt(a, b):
    return lax.dot_general(a, b, (((1,), (1,)), ((), ())), preferred_element_type=F32)


def _softplus(x):
    return jnp.maximum(x, 0.0) + jnp.log1p(jnp.exp(-jnp.abs(x)))


def _sigmoid(x):
    return 1.0 / (1.0 + jnp.exp(-x))


C_RW = 0
C_Q = 1792
C_K = C_Q + DSA_W
C_V = C_K + DSA_W
C_QI = C_V + DSA_W
C_KIW = C_QI + IDX_H * IDX_D
C_END = C_KIW + LANES


def _in_proj_kernel(x_ref, g_ref, w_ref, prw_ref, q_ref, k_ref, v_ref, qi_ref, kiw_ref,
                    kT_ref, vb_ref, kiT_ref):
    h = _rms(x_ref[0], g_ref[...]).astype(BF16)

    def mm(lo, hi):
        return _dot(h, w_ref[:, lo:hi])

    prw_ref[0] = mm(C_RW, C_Q)
    q_ref[0] = mm(C_Q, C_K).astype(BF16)
    k = mm(C_K, C_V)
    k_ref[0] = k
    kT_ref[0] = k.T.astype(BF16)
    v = mm(C_V, C_QI)
    v_ref[0] = v
    vb_ref[0] = v.astype(BF16)
    qi_ref[0] = mm(C_QI, C_KIW).astype(BF16)
    kiw = mm(C_KIW, C_END)
    kiw_ref[0] = kiw
    kiT_ref[0] = kiw.T[0:IDX_D, :].astype(BF16)


def _in_proj(x, g, w_pad, tm):
    b, t, d = x.shape
    grid = (b, t // tm)
    row = lambda w, dt: jax.ShapeDtypeStruct((b, t, w), dt)
    rspec = lambda w: pl.BlockSpec((1, tm, w), lambda i, j: (i, j, 0))
    tspec = lambda w: pl.BlockSpec((1, w, tm), lambda i, j: (i, 0, j))
    return pl.pallas_call(
        _in_proj_kernel,
        grid=grid,
        in_specs=[rspec(d),
                  pl.BlockSpec((1, d), lambda i, j: (0, 0)),
                  pl.BlockSpec((d, C_END), lambda i, j: (0, 0))],
        out_specs=[rspec(C_Q), rspec(DSA_W), rspec(DSA_W), rspec(DSA_W), rspec(IDX_H * IDX_D),
                   rspec(LANES), tspec(DSA_W), rspec(DSA_W), tspec(IDX_D)],
        out_shape=[row(C_Q, F32), row(DSA_W, BF16), row(DSA_W, F32), row(DSA_W, F32),
                   row(IDX_H * IDX_D, BF16), row(LANES, F32),
                   jax.ShapeDtypeStruct((b, DSA_W, t), BF16), row(DSA_W, BF16),
                   jax.ShapeDtypeStruct((b, IDX_D, t), BF16)],
        compiler_params=_cparams("arbitrary", "arbitrary"),
        name="in_proj",
    )(x, g, w_pad)


def _rwkv_kernel(n_steps, prw_ref, sh0_ref, s0_ref, mu_ref, w0_ref, w2_ref, a0_ref, a2_ref, g2_ref,
                 kkw_ref, kaw_ref, rk_ref, lnw_ref, lnb_ref,
                 y_ref, s1_ref, sh1_ref,
                 st_scr, carry_scr, r_s, w_s, k_s, v_s, kk_s, b_s, y_s, g_s):
    nb, tc = prw_ref.shape[0], prw_ref.shape[1]
    c = pl.program_id(1)

    @pl.when(c == 0)
    def _():
        st_scr[...] = s0_ref[...]
        carry_scr[...] = sh0_ref[...]

    row = lax.broadcasted_iota(I32, (tc, 1), 0)
    lane = lax.broadcasted_iota(I32, (1, LANES), 1)
    for b in range(nb):
        x = prw_ref[b]
        prev = jnp.where(row == 0, carry_scr[b], pltpu.roll(x, 1, 0))
        carry_scr[b] = x[n_steps - 1:n_steps, :]
        ps = x + (prev - x) * mu_ref[...]
        r = ps[:, 0:RW_W]
        k = ps[:, RW_W:2 * RW_W]
        v = ps[:, 2 * RW_W:3 * RW_W]
        la = ps[:, 3 * RW_W:3 * RW_W + LANES]
        gd = ps[:, 3 * RW_W + LANES:3 * RW_W + 2 * LANES]
        z = jnp.where(lane < 64, jnp.tanh(la), la).astype(BF16)
        w_log = -_softplus(-(w0_ref[...] + _dot(z, w2_ref[...]))) - 0.5
        decay = jnp.exp(-jnp.exp(w_log))
        a = _sigmoid(a0_ref[...] + _dot(z, a2_ref[...]))
        g_s[b] = _dot(_sigmoid(gd).astype(BF16), g2_ref[...])
        kk = k * kkw_ref[...]
        k2 = k * (1.0 + (a - 1.0) * kaw_ref[...])
        for h in range(RW_H):
            sl = slice(RW_HEAD * h, RW_HEAD * (h + 1))
            kkh = kk[:, sl]
            nrm = jnp.sqrt(jnp.sum(kkh * kkh, axis=-1, keepdims=True))
            kkh = kkh / jnp.maximum(nrm, 1e-12)
            r_s[b, h] = r[:, sl]
            w_s[b, h] = decay[:, sl]
            k_s[b, h] = k2[:, sl]
            v_s[b, h] = v[:, sl]
            kk_s[b, h] = kkh
            b_s[b, h] = kkh * a[:, sl]

    eye = (lax.broadcasted_iota(I32, (RW_HEAD, RW_HEAD), 0)
           == lax.broadcasted_iota(I32, (RW_HEAD, RW_HEAD), 1))

    def step(t, carry):
        for b in range(nb):
            for h in range(RW_H):
                s = st_scr[b, h]
                tt = pl.ds(t, 1)
                sa = -jnp.sum(s * kk_s[b, h, tt, :], axis=1, keepdims=True)
                vcol = jnp.sum(jnp.where(eye, v_s[b, h, tt, :], 0.0), axis=1, keepdims=True)
                s = s * w_s[b, h, tt, :] + sa * b_s[b, h, tt, :] + vcol * k_s[b, h, tt, :]
                st_scr[b, h] = s
                ycol = jnp.sum(s * r_s[b, h, tt, :], axis=1, keepdims=True)
                y_s[b, h, tt, :] = jnp.sum(jnp.where(eye, ycol, 0.0), axis=0, keepdims=True)
        return carry

    lax.fori_loop(0, n_steps, step, 0)

    for b in range(nb):
        for h in range(RW_H):
            sl = slice(RW_HEAD * h, RW_HEAD * (h + 1))
            y = y_s[b, h]
            mean = jnp.mean(y, axis=-1, keepdims=True)
            var = jnp.mean(jnp.square(y - mean), axis=-1, keepdims=True)
            yn = (y - mean) * lax.rsqrt(var + RW_EPS) * lnw_ref[h:h + 1, :] + lnb_ref[h:h + 1, :]
            rr, kk2, vv = r_s[b, h], k_s[b, h], v_s[b, h]
            bonus = jnp.sum(rr * kk2 * rk_ref[h:h + 1, :], axis=-1, keepdims=True) * vv
            y_ref[b, :, sl] = ((yn + bonus) * g_s[b][:, sl]).astype(BF16)
    s1_ref[...] = st_scr[...]
    sh1_ref[...] = carry_scr[...]


def _rwkv(prw, shift0, s0, P, nb, tc, n_valid):
    b, t, cols = prw.shape
    n_steps = min(tc, n_valid)
    grid = (b // nb, t // tc)
    full = lambda a: pl.BlockSpec(a.shape, lambda i, j: (0,) * a.ndim)
    params = [P['mu'], P['w0'], P['w2'], P['a0'], P['a2'], P['g2'], P['kk'], P['ka'], P['rk'],
              P['lnw'], P['lnb']]
    hs = lambda: pltpu.VMEM((nb, RW_H, tc, RW_HEAD), F32)
    return pl.pallas_call(
        functools.partial(_rwkv_kernel, n_steps),
        grid=grid,
        in_specs=[pl.BlockSpec((nb, tc, cols), lambda i, j: (i, j, 0)),
                  pl.BlockSpec((nb, 1, cols), lambda i, j: (i, 0, 0)),
                  pl.BlockSpec((nb, RW_H, RW_HEAD, RW_HEAD), lambda i, j: (i, 0, 0, 0))]
                 + [full(a) for a in params],
        out_specs=[pl.BlockSpec((nb, tc, RW_W), lambda i, j: (i, j, 0)),
                   pl.BlockSpec((nb, RW_H, RW_HEAD, RW_HEAD), lambda i, j: (i, 0, 0, 0)),
                   pl.BlockSpec((nb, 1, cols), lambda i, j: (i, 0, 0))],
        out_shape=[jax.ShapeDtypeStruct((b, t, RW_W), BF16),
                   jax.ShapeDtypeStruct((b, RW_H, RW_HEAD, RW_HEAD), F32),
                   jax.ShapeDtypeStruct((b, 1, cols), F32)],
        scratch_shapes=[pltpu.VMEM((nb, RW_H, RW_HEAD, RW_HEAD), F32),
                        pltpu.VMEM((nb, 1, cols), F32),
                        hs(), hs(), hs(), hs(), hs(), hs(), hs(),
                        pltpu.VMEM((nb, tc, RW_W), F32)],
        compiler_params=_cparams("arbitrary", "arbitrary"),
        name="rwkv",
    )(prw, shift0, s0, *params)


RC = 64
RC_GROUP = 16


def _split3(x):
    hi = x.astype(BF16)
    r1 = x - hi.astype(F32)
    mid = r1.astype(BF16)
    lo = (r1 - mid.astype(F32)).astype(BF16)
    return hi, mid, lo


def _mm(a, b, passes):
    if passes == 1:
        return _dot(a.astype(BF16), b.astype(BF16))
    ah, al = _split_bf16(a)
    bh, bl = _split_bf16(b)
    return _dot(ah, bh) + _dot(ah, bl) + _dot(al, bh)


def _mm_nt(a, b, passes):
    if passes == 1:
        return _dot_nt(a.astype(BF16), b.astype(BF16))
    ah, al = _split_bf16(a)
    bh, bl = _split_bf16(b)
    return _dot_nt(ah, bh) + _dot_nt(ah, bl) + _dot_nt(al, bh)


def _exact_left(m_bf16, x):
    hi, mid, lo = _split3(x)
    return _dot(m_bf16, hi) + _dot(m_bf16, mid) + _dot(m_bf16, lo)


def _exact_right(x, m_bf16):
    hi, mid, lo = _split3(x)
    return _dot(hi, m_bf16) + _dot(mid, m_bf16) + _dot(lo, m_bf16)


def _rwkv_chunk_kernel(passes, prw_ref, sh0_ref, s0_ref, mu_ref, w0_ref, w2_ref, a0_ref, a2_ref,
                       g2_ref, kkw_ref, kaw_ref, rk_ref, lnw_ref, lnb_ref, ltri_ref, lones_ref,
                       hones_ref, y_ref, s1_ref, sh1_ref,
                       st_scr, carry_scr, rh_s, kkh_s, bt_s, kt_s, bg_s, kg_s, v_s, gc_s, y_s,
                       bonus_s, g_s):
    nb, tc = prw_ref.shape[0], prw_ref.shape[1]
    nh = nb * RW_H
    c = pl.program_id(1)

    @pl.when(c == 0)
    def _():
        for b in range(nb):
            for h in range(RW_H):
                st_scr[b * RW_H + h] = s0_ref[b, h]
        carry_scr[...] = sh0_ref[...]

    row = lax.broadcasted_iota(I32, (tc, 1), 0)
    lane = lax.broadcasted_iota(I32, (1, LANES), 1)
    hones = hones_ref[...]
    for b in range(nb):
        x = prw_ref[b]
        prev = jnp.where(row == 0, carry_scr[b], pltpu.roll(x, 1, 0))
        carry_scr[b] = x[tc - 1:tc, :]
        ps = x + (prev - x) * mu_ref[...]
        r = ps[:, 0:RW_W]
        k = ps[:, RW_W:2 * RW_W]
        v = ps[:, 2 * RW_W:3 * RW_W]
        la = ps[:, 3 * RW_W:3 * RW_W + LANES]
        gd = ps[:, 3 * RW_W + LANES:3 * RW_W + 2 * LANES]
        z = jnp.where(lane < 64, jnp.tanh(la), la).astype(BF16)
        w_log = -_softplus(-(w0_ref[...] + _dot(z, w2_ref[...]))) - 0.5
        lw = -jnp.exp(w_log)
        a = _sigmoid(a0_ref[...] + _dot(z, a2_ref[...]))
        g_s[b] = _dot(_sigmoid(gd).astype(BF16), g2_ref[...])
        kk = k * kkw_ref[...]
        kk = kk / jnp.maximum(jnp.sqrt(_exact_right(kk * kk, hones)), 1e-12)
        k2 = k * (1.0 + (a - 1.0) * kaw_ref[...])
        bb = kk * a
        bonus_s[b] = _exact_right(r * k2 * rk_ref[...], hones) * v
        gsum = _exact_left(ltri_ref[...], lw)
        gend = _exact_left(lones_ref[...], lw)
        e_neg = jnp.exp(-gsum)
        e_end = jnp.exp(gend - gsum)
        cols = {'rh': r * jnp.exp(gsum), 'kkh': kk * jnp.exp(gsum - lw), 'bt': bb * e_neg,
                'kt': k2 * e_neg, 'bg': bb * e_end, 'kg': k2 * e_end, 'v': v, 'gc': jnp.exp(gend)}
        dst = {'rh': rh_s, 'kkh': kkh_s, 'bt': bt_s, 'kt': kt_s, 'bg': bg_s, 'kg': kg_s, 'v': v_s,
               'gc': gc_s}
        for name, val in cols.items():
            for h in range(RW_H):
                dst[name][b * RW_H + h] = val[:, RW_HEAD * h:RW_HEAD * (h + 1)]

    ri = lax.broadcasted_iota(I32, (RC, RC), 0)
    ci = lax.broadcasted_iota(I32, (RC, RC), 1)
    strict = ri > ci
    incl = ri >= ci
    eye = ri == ci
    eye_f = jnp.where(eye, 1.0, 0.0)
    n_double = int(math.log2(RC)) - 1

    def head_group(heads, sub):
        base = pl.multiple_of(sub * RC, RC)
        rows = pl.ds(base, RC)
        mm = lambda a, b: _mm(a, b, passes)
        each = lambda f: [f(i) for i in range(len(heads))]
        rh = [rh_s[hd, rows, :] for hd in heads]
        kkh = [kkh_s[hd, rows, :] for hd in heads]
        vv = [v_s[hd, rows, :] for hd in heads]
        gram = each(lambda i: _mm_nt(
            jnp.concatenate([kkh[i], rh[i]], axis=0),
            jnp.concatenate([bt_s[heads[i], rows, :], kt_s[heads[i], rows, :]], axis=0), passes))
        mb = each(lambda i: jnp.where(strict, gram[i][0:RC, 0:RC], 0.0))
        mkv = each(lambda i: mm(jnp.where(strict, gram[i][0:RC, RC:2 * RC], 0.0), vv[i]))
        nkv = each(lambda i: mm(jnp.where(incl, gram[i][RC:2 * RC, RC:2 * RC], 0.0), vv[i]))
        nb_ = each(lambda i: jnp.where(incl, gram[i][RC:2 * RC, 0:RC], 0.0))
        inv = each(lambda i: eye_f - mb[i])
        q = mb
        for _ in range(n_double):
            q = each(lambda i: mm(q[i], q[i]))
            inv = each(lambda i: inv[i] + mm(inv[i], q[i]))
        u = each(lambda i: -mm(inv[i], jnp.concatenate([kkh[i], mkv[i]], axis=1)))
        yy = each(lambda i: mm(nb_[i], u[i]))
        zz = each(lambda i: mm(u[i].T, bg_s[heads[i], rows, :]))
        vk = each(lambda i: mm(vv[i].T, kg_s[heads[i], rows, :]))
        s = [st_scr[hd] for hd in heads]
        ys = each(lambda i: _mm_nt(rh[i] + yy[i][:, 0:RC], s[i], passes))
        z1 = each(lambda i: jnp.where(eye, gc_s[heads[i], pl.ds(base, 1), :], 0.0)
                  + zz[i][0:RW_HEAD])
        sn = each(lambda i: mm(s[i], z1[i]))
        for i, hd in enumerate(heads):
            y_s[hd, rows, :] = ys[i] + yy[i][:, RC:2 * RC] + nkv[i]
            st_scr[hd] = sn[i] + zz[i][RW_HEAD:2 * RW_HEAD] + vk[i]

    def chunk_step(sub, carry):
        for g0 in range(0, nh, RC_GROUP):
            head_group(list(range(g0, min(g0 + RC_GROUP, nh))), sub)
        return carry

    lax.fori_loop(0, tc // RC, chunk_step, 0)

    for b in range(nb):
        for h in range(RW_H):
            sl = slice(RW_HEAD * h, RW_HEAD * (h + 1))
            y = y_s[b * RW_H + h]
            mean = jnp.mean(y, axis=-1, keepdims=True)
            var = jnp.mean(jnp.square(y - mean), axis=-1, keepdims=True)
            yn = (y - mean) * lax.rsqrt(var + RW_EPS) * lnw_ref[h:h + 1, :] + lnb_ref[h:h + 1, :]
            y_ref[b, :, sl] = ((yn + bonus_s[b, :, sl]) * g_s[b, :, sl]).astype(BF16)
            s1_ref[b, h] = st_scr[b * RW_H + h]
    sh1_ref[...] = carry_scr[...]


def _rwkv_chunk(prw, shift0, s0, P, nb, tc, passes):
    b, t, cols = prw.shape
    tok = np.arange(tc)
    same = (tok[:, None] // RC) == (tok[None, :] // RC)
    ltri = jnp.asarray(same & (tok[:, None] >= tok[None, :]), BF16)
    lones = jnp.asarray(same, BF16)
    col = np.arange(RW_W)
    hones = jnp.asarray((col[:, None] // RW_HEAD) == (col[None, :] // RW_HEAD), BF16)
    full = lambda a: pl.BlockSpec(a.shape, lambda i, j: (0,) * a.ndim)
    params = [P['mu'], P['w0'], P['w2'], P['a0'], P['a2'], P['g2'], P['kk'], P['ka'],
              P['rk'].reshape(1, RW_W), P['lnw'], P['lnb'], ltri, lones, hones]
    nh = nb * RW_H
    hs = lambda: pltpu.VMEM((nh, tc, RW_HEAD), F32)
    return pl.pallas_call(
        functools.partial(_rwkv_chunk_kernel, passes),
        grid=(b // nb, t // tc),
        in_specs=[pl.BlockSpec((nb, tc, cols), lambda i, j: (i, j, 0)),
                  pl.BlockSpec((nb, 1, cols), lambda i, j: (i, 0, 0)),
                  pl.BlockSpec((nb, RW_H, RW_HEAD, RW_HEAD), lambda i, j: (i, 0, 0, 0))]
                 + [full(a) for a in params],
        out_specs=[pl.BlockSpec((nb, tc, RW_W), lambda i, j: (i, j, 0)),
                   pl.BlockSpec((nb, RW_H, RW_HEAD, RW_HEAD), lambda i, j: (i, 0, 0, 0)),
                   pl.BlockSpec((nb, 1, cols), lambda i, j: (i, 0, 0))],
        out_shape=[jax.ShapeDtypeStruct((b, t, RW_W), BF16),
                   jax.ShapeDtypeStruct((b, RW_H, RW_HEAD, RW_HEAD), F32),
                   jax.ShapeDtypeStruct((b, 1, cols), F32)],
        scratch_shapes=[pltpu.VMEM((nh, RW_HEAD, RW_HEAD), F32),
                        pltpu.VMEM((nb, 1, cols), F32),
                        hs(), hs(), hs(), hs(), hs(), hs(), hs(), hs(), hs(),
                        pltpu.VMEM((nb, tc, RW_W), F32), pltpu.VMEM((nb, tc, RW_W), F32)],
        compiler_params=_cparams("arbitrary", "arbitrary"),
        name="rwkv_chunk",
    )(prw, shift0, s0, *params)


def _pattern_to_f32(u):
    key = u ^ INT_MIN
    return lax.bitcast_convert_type(jnp.where(key >= 0, key, key ^ 0x7FFFFFFF), F32)


def _topk_mask(sc_ref, madd_ref, p_scr, rows, nch, k_sel, lim, idx_bits):
    ntile = CW // LANES
    lane = lax.broadcasted_iota(I32, (rows, LANES), 1)

    def count(pred):
        def body(c, acc):
            off = pl.multiple_of(c * CW, CW)
            x = sc_ref[:, pl.ds(off, CW)]
            for t in range(ntile):
                acc = acc + pred(x[:, LANES * t:LANES * (t + 1)], off + LANES * t)
            return acc
        acc = lax.fori_loop(0, nch, body, jnp.zeros((rows, LANES), F32))
        return jnp.sum(acc, axis=1, keepdims=True)

    def bcast(col):
        return jnp.broadcast_to(col, (rows, LANES))

    def searching(state):
        it, alive, _, _ = state
        return (it < 32) & (alive > 0)

    def value_bit(state):
        it, _, tau, cnt_tau = state
        cand = tau | jnp.left_shift(jnp.int32(1), 31 - it)
        cb = bcast(_pattern_to_f32(cand))
        cnt = count(lambda x, off: jnp.where(x >= cb, 1.0, 0.0))
        take = cnt >= k_sel
        tau = jnp.where(take, cand, tau)
        cnt_tau = jnp.where(take, cnt, cnt_tau)
        settled = jnp.where(cnt_tau == k_sel, 1, jnp.where(lim < 0, 1, 0))
        return it + 1, 1 - jnp.min(settled), tau, cnt_tau

    _, _, tau, _ = lax.while_loop(
        searching, value_bit,
        (jnp.int32(0), jnp.int32(1), jnp.zeros((rows, 1), I32),
         jnp.zeros((rows, 1), F32) + jnp.asarray(nch * CW, F32)))
    thr = _pattern_to_f32(tau)
    thr_b = bcast(thr)
    cnt_gt = count(lambda x, off: jnp.where(x > thr_b, 1.0, 0.0))
    cnt_ge = count(lambda x, off: jnp.where(x >= thr_b, 1.0, 0.0))
    need = k_sel - cnt_gt
    tie = jnp.where(cnt_ge > k_sel, jnp.where(thr > NEG_INF, 1, 0), 0)

    p_scr[...] = jnp.full((rows, LANES), INT_MAX, I32)

    @pl.when(jnp.max(tie) > 0)
    def _():
        def index_bit(it, p):
            cand = p | jnp.left_shift(jnp.int32(1), idx_bits - 1 - it)
            cb = bcast(cand)
            cnt = count(lambda x, off: jnp.where(x == thr_b,
                                                 jnp.where(off + lane < cb, 1.0, 0.0), 0.0))
            return jnp.where(cnt < need, cand, p)
        p = lax.fori_loop(0, idx_bits, index_bit, jnp.zeros((rows, 1), I32))
        p_scr[...] = bcast(p)

    p_b = p_scr[...]
    lim_b = bcast(lim)

    def fin(c, carry):
        off = pl.multiple_of(c * CW, CW)
        x = sc_ref[:, pl.ds(off, CW)]
        for t in range(ntile):
            xt = x[:, LANES * t:LANES * (t + 1)]
            idx = off + LANES * t + lane
            sel = jnp.where(xt > thr_b, 0.0,
                            jnp.where(xt == thr_b, jnp.where(idx <= p_b, 0.0, NEG_INF), NEG_INF))
            madd_ref[:, pl.ds(pl.multiple_of(off + LANES * t, LANES), LANES)] = (
                jnp.where(idx <= lim_b, sel, NEG_INF))
        return carry

    lax.fori_loop(0, nch, fin, 0)


def _bias_table(rb_ref, h, d):
    bias = jnp.full(d.shape, rb_ref[0, h], F32)
    for j in range(1, NUM_BUCKETS):
        if BUCKET_START[j] is not None:
            bias = jnp.where(d >= BUCKET_START[j], rb_ref[j, h], bias)
    return bias


def _dsa_prompt_kernel(k_sel, idx_bits, rb_ref, q_ref, qi_ref, kiw_ref, kT_ref, vb_ref, kiT_ref,
                       o_ref, sc_scr, madd_scr, p_scr, btab_scr):
    i = pl.program_id(1)
    nch = ((i + 1) * QB + CW - 1) // CW

    @pl.when((pl.program_id(0) == 0) & (i == 0))
    def _():
        d = (lax.broadcasted_iota(I32, (QB, 2 * QB), 0)
             - lax.broadcasted_iota(I32, (QB, 2 * QB), 1) + QB)
        for h in range(DSA_H):
            btab_scr[h] = _bias_table(rb_ref, h, d) - rb_ref[NUM_BUCKETS - 1, h]

    tq = i * QB + lax.broadcasted_iota(I32, (QB, 1), 0)
    qi = qi_ref[0]
    wi = kiw_ref[0][:, IDX_D:IDX_D + IDX_H]
    qi_all = jnp.concatenate([qi[:, IDX_D * h:IDX_D * (h + 1)] for h in range(IDX_H)], axis=0)
    wi_h = [wi[:, h:h + 1] for h in range(IDX_H)]

    def score_chunk(c, carry):
        off = pl.multiple_of(c * CW, CW)
        s = _dot(qi_all, kiT_ref[0, :, pl.ds(off, CW)])
        acc = jnp.zeros((QB, CW), F32)
        for h in range(IDX_H):
            acc = acc + jnp.maximum(s[QB * h:QB * (h + 1)], 0.0) * wi_h[h]
        spos = off + lax.broadcasted_iota(I32, (1, CW), 1)
        sc_scr[:, pl.ds(off, CW)] = jnp.where(spos <= tq, acc * IDX_SCALE + 0.0, NEG_INF)
        return carry

    lax.fori_loop(0, nch, score_chunk, 0)

    _topk_mask(sc_scr, madd_scr, p_scr, QB, nch, k_sel, tq, idx_bits)

    q = q_ref[0] * (DSA_HEAD ** -0.5)
    tiles = CW // QB
    hsl = [slice(DSA_HEAD * h, DSA_HEAD * (h + 1)) for h in range(DSA_H)]
    qh = [q[:, sl] for sl in hsl]

    def attend(heads, near, c, carry):
        n = len(heads)
        off = pl.multiple_of(c * CW, CW)
        madd = madd_scr[:, pl.ds(off, CW)]
        s = [_dot(qh[h], kT_ref[0, hsl[h], pl.ds(off, CW)]) + madd for h in heads]
        if near:
            for j, h in enumerate(heads):
                parts = []
                for t in range(tiles):
                    delta = i - (c * tiles + t)
                    parts.append(jnp.where(delta == 0, btab_scr[h, :, QB:2 * QB],
                                           jnp.where(delta == 1, btab_scr[h, :, 0:QB], 0.0)))
                s[j] = s[j] + jnp.concatenate(parts, axis=1)
        m_new = [jnp.maximum(carry[j][0], jnp.max(s[j], axis=1, keepdims=True)) for j in range(n)]
        p = [jnp.exp(s[j] - m_new[j]) for j in range(n)]
        pv = [_dot(p[j].astype(BF16), vb_ref[0, pl.ds(off, CW), hsl[h]])
              for j, h in enumerate(heads)]
        out = []
        for j in range(n):
            m, l, acc = carry[j]
            alpha = jnp.exp(m - m_new[j])
            out.append((m_new[j], l * alpha + jnp.sum(p[j], axis=1, keepdims=True),
                        acc * alpha + pv[j]))
        return tuple(out)

    n_far = jnp.maximum(i - 1, 0) // tiles
    for g0 in range(0, DSA_H, ATTN_GROUP):
        heads = list(range(g0, g0 + ATTN_GROUP))
        init = tuple((jnp.full((QB, 1), NEG_INF, F32), jnp.zeros((QB, 1), F32),
                      jnp.zeros((QB, DSA_HEAD), F32)) for _ in heads)
        carry = lax.fori_loop(0, n_far, functools.partial(attend, heads, False), init)
        carry = lax.fori_loop(n_far, nch, functools.partial(attend, heads, True), carry)
        for j, h in enumerate(heads):
            _, l, acc = carry[j]
            o_ref[0, :, hsl[h]] = (acc / l).astype(BF16)


def _dsa_prompt(rel_bias, q, qi, kiw, kT, vb, kiT):
    b, t, _ = q.shape
    k_sel = min(TOPK_MAX, t // 4)
    idx_bits = max(1, int(math.ceil(math.log2(t))))
    blk = lambda w: pl.BlockSpec((1, QB, w), lambda bi, i: (bi, i, 0))
    whole = lambda shape: pl.BlockSpec(shape, lambda bi, i: (bi, 0, 0))
    return pl.pallas_call(
        functools.partial(_dsa_prompt_kernel, k_sel, idx_bits),
        grid=(b, t // QB),
        in_specs=[pl.BlockSpec(memory_space=pltpu.SMEM),
                  blk(DSA_W), blk(IDX_H * IDX_D), blk(LANES),
                  whole((1, DSA_W, t)), whole((1, t, DSA_W)), whole((1, IDX_D, t))],
        out_specs=blk(DSA_W),
        out_shape=jax.ShapeDtypeStruct((b, t, DSA_W), BF16),
        scratch_shapes=[pltpu.VMEM((QB, t), F32), pltpu.VMEM((QB, t), F32),
                        pltpu.VMEM((QB, LANES), I32),
                        pltpu.VMEM((DSA_H, QB, 2 * QB), F32)],
        compiler_params=_cparams("arbitrary", "arbitrary"),
        name="dsa_prompt",
    )(rel_bias, q, qi, kiw, kT, vb, kiT)


SROWS = 8
SCHUNK = PAGES_PER_STEP * PAGE_SIZE


SCORE_PAGES = 2 * PAGES_PER_STEP
SCORE_CHUNK = SCORE_PAGES * PAGE_SIZE


def _dsa_s_scores_kernel(n_tok, n_past_chunks, pt_ref, qi_ref, wcol_ref, kinew_ref, *rest):
    pages, out_ref = rest[:SCORE_PAGES], rest[SCORE_PAGES]
    c = pl.program_id(1)
    qi = qi_ref[0]
    wcol = wcol_ref[0]

    def head_sum(s):
        s = jnp.maximum(s, 0.0) * wcol
        return jnp.sum(s.reshape(n_tok, IDX_H, s.shape[-1]), axis=1) * IDX_SCALE + 0.0

    @pl.when(c < n_past_chunks)
    def _():
        kit = jnp.concatenate([pg[0] for pg in pages], axis=1).astype(BF16)
        out_ref[0, 0:n_tok, :] = head_sum(_dot(qi, kit))
        out_ref[0, n_tok:SROWS, :] = jnp.full((SROWS - n_tok, SCORE_CHUNK), NEG_INF, F32)

    @pl.when(c == n_past_chunks)
    def _():
        s = head_sum(_dot_nt(qi, kinew_ref[0]))
        tr = lax.broadcasted_iota(I32, (n_tok, SROWS), 0)
        tc = lax.broadcasted_iota(I32, (n_tok, SROWS), 1)
        out_ref[...] = jnp.full((1, SROWS, SCORE_CHUNK), NEG_INF, F32)
        out_ref[0, 0:n_tok, 0:SROWS] = jnp.where(tc <= tr, s, NEG_INF)


def _dsa_s_scores(page_table, cache_ki, qi_rows, wcol, ki_new, n_tok):
    b, n_pages = page_table.shape
    npc = n_pages // SCORE_PAGES
    page_specs = [
        pl.BlockSpec((1, IDX_D, PAGE_SIZE),
                     lambda bi, c, pt, j=j: (pt[bi, jnp.minimum(c, npc - 1) * SCORE_PAGES + j], 0, 0))
        for j in range(SCORE_PAGES)]
    rows = qi_rows.shape[1]
    grid_spec = pltpu.PrefetchScalarGridSpec(
        num_scalar_prefetch=1,
        grid=(b, npc + 1),
        in_specs=[pl.BlockSpec((1, rows, IDX_D), lambda bi, c, pt: (bi, 0, 0)),
                  pl.BlockSpec((1, rows, 1), lambda bi, c, pt: (bi, 0, 0)),
                  pl.BlockSpec((1, SROWS, IDX_D), lambda bi, c, pt: (bi, 0, 0))] + page_specs,
        out_specs=pl.BlockSpec((1, SROWS, SCORE_CHUNK), lambda bi, c, pt: (bi, 0, c)),
    )
    return pl.pallas_call(
        functools.partial(_dsa_s_scores_kernel, n_tok, npc),
        grid_spec=grid_spec,
        out_shape=jax.ShapeDtypeStruct((b, SROWS, (npc + 1) * SCORE_CHUNK), F32),
        compiler_params=_cparams("arbitrary", "arbitrary"),
        name="dsa_s_scores",
    )(page_table, qi_rows, wcol, ki_new, *([cache_ki] * SCORE_PAGES))


def _topk_rows_kernel(k_sel, idx_bits, sc_ref, lim_ref, madd_ref, p_scr):
    rows, width = sc_ref.shape
    _topk_mask(sc_ref, madd_ref, p_scr, rows, width // CW, k_sel, lim_ref[...], idx_bits)


def _topk_rows(scores, lim, k_sel, rows_per_step):
    n, width = scores.shape
    idx_bits = max(1, int(math.ceil(math.log2(width))))
    return pl.pallas_call(
        functools.partial(_topk_rows_kernel, k_sel, idx_bits),
        grid=(n // rows_per_step,),
        in_specs=[pl.BlockSpec((rows_per_step, width), lambda i: (i, 0)),
                  pl.BlockSpec((rows_per_step, 1), lambda i: (i, 0))],
        out_specs=pl.BlockSpec((rows_per_step, width), lambda i: (i, 0)),
        out_shape=jax.ShapeDtypeStruct((n, width), F32),
        scratch_shapes=[pltpu.VMEM((rows_per_step, LANES), I32)],
        compiler_params=_cparams("arbitrary"),
        name="topk_rows",
    )(scores, lim)


def _dsa_s_attn_kernel(n_past_chunks, past, pt_ref, rb_ref, q_ref, madd_ref, knew_ref, vnew_ref,
                       *rest):
    kp = rest[:PAGES_PER_STEP]
    vp = rest[PAGES_PER_STEP:2 * PAGES_PER_STEP]
    o_ref, m_scr, l_scr, acc_scr = rest[2 * PAGES_PER_STEP:]
    c = pl.program_id(1)
    rows = DSA_H * SROWS

    @pl.when(c == 0)
    def _():
        m_scr[...] = jnp.full(m_scr.shape, NEG_INF, F32)
        l_scr[...] = jnp.zeros(l_scr.shape, F32)
        acc_scr[...] = jnp.zeros(acc_scr.shape, F32)

    q = q_ref[0] * (DSA_HEAD ** -0.5)
    head_of_row = lax.broadcasted_iota(I32, (rows, DSA_W), 0) // SROWS
    head_of_col = lax.broadcasted_iota(I32, (rows, DSA_W), 1) // DSA_HEAD
    qblk = jnp.where(head_of_row == head_of_col, jnp.concatenate([q] * DSA_H, axis=0),
                     jnp.zeros((), BF16))

    def near_bias(width, key0):
        tok = lax.broadcasted_iota(I32, (SROWS, width), 0)
        d = past + tok - (key0 + lax.broadcasted_iota(I32, (SROWS, width), 1))
        return jnp.concatenate(
            [_bias_table(rb_ref, h, d) - rb_ref[NUM_BUCKETS - 1, h] for h in range(DSA_H)], axis=0)

    def update(s, pv_fn):
        m = m_scr[...]
        m_new = jnp.maximum(m, jnp.max(s, axis=1, keepdims=True))
        alpha = jnp.exp(m - m_new)
        p = jnp.exp(s - m_new)
        l_scr[...] = l_scr[...] * alpha + jnp.sum(p, axis=1, keepdims=True)
        acc_scr[...] = acc_scr[...] * alpha + pv_fn(p.astype(BF16))
        m_scr[...] = m_new

    def chunk_t(pages):
        return jnp.concatenate([pg[0].reshape(DSA_W, PAGE_SIZE) for pg in pages],
                               axis=1).astype(BF16)

    @pl.when(c < n_past_chunks)
    def _():
        s = _dot(qblk, chunk_t(kp)) + jnp.concatenate([madd_ref[0]] * DSA_H, axis=0)
        is_last = c == n_past_chunks - 1
        tail = SCHUNK - PAGE_SIZE
        nb_ = jnp.where(is_last, near_bias(PAGE_SIZE, past - PAGE_SIZE), 0.0)
        s = jnp.concatenate([s[:, 0:tail], s[:, tail:SCHUNK] + nb_], axis=1)
        vt = chunk_t(vp)
        update(s, lambda pb: _dot_nt(pb, vt))

    @pl.when(c == n_past_chunks)
    def _():
        s = (_dot_nt(qblk, knew_ref[0]) + jnp.concatenate([madd_ref[0, :, 0:SROWS]] * DSA_H, axis=0)
             + near_bias(SROWS, past))
        update(s, lambda pb: _dot(pb, vnew_ref[0]))
        for h in range(DSA_H):
            r = slice(SROWS * h, SROWS * (h + 1))
            cs = slice(DSA_HEAD * h, DSA_HEAD * (h + 1))
            o_ref[0, :, cs] = (acc_scr[r, cs] / l_scr[r, :]).astype(o_ref.dtype)


def _dsa_s_attn(page_table, rel_bias, cache_k, cache_v, q, madd, k_new, v_new):
    b, n_pages = page_table.shape
    npc = n_pages // PAGES_PER_STEP
    rows = DSA_H * SROWS
    assert BUCKET_START[NUM_BUCKETS - 1] <= PAGE_SIZE

    def page_spec(j):
        return pl.BlockSpec(
            (1, DSA_H, DSA_HEAD, PAGE_SIZE),
            lambda bi, c, pt, j=j: (pt[bi, jnp.minimum(c, npc - 1) * PAGES_PER_STEP + j], 0, 0, 0))

    page_specs = [page_spec(j) for j in range(PAGES_PER_STEP)]
    grid_spec = pltpu.PrefetchScalarGridSpec(
        num_scalar_prefetch=1,
        grid=(b, npc + 1),
        in_specs=[pl.BlockSpec(memory_space=pltpu.SMEM),
                  pl.BlockSpec((1, SROWS, DSA_W), lambda bi, c, pt: (bi, 0, 0)),
                  pl.BlockSpec((1, SROWS, SCHUNK), lambda bi, c, pt: (bi, 0, c)),
                  pl.BlockSpec((1, SROWS, DSA_W), lambda bi, c, pt: (bi, 0, 0)),
                  pl.BlockSpec((1, SROWS, DSA_W), lambda bi, c, pt: (bi, 0, 0))]
                 + page_specs + page_specs,
        out_specs=pl.BlockSpec((1, SROWS, DSA_W), lambda bi, c, pt: (bi, 0, 0)),
        scratch_shapes=[pltpu.VMEM((rows, 1), F32), pltpu.VMEM((rows, 1), F32),
                        pltpu.VMEM((rows, DSA_W), F32)],
    )
    return pl.pallas_call(
        functools.partial(_dsa_s_attn_kernel, npc, n_pages * PAGE_SIZE),
        grid_spec=grid_spec,
        out_shape=jax.ShapeDtypeStruct((b, SROWS, DSA_W), BF16),
        compiler_params=_cparams("arbitrary", "arbitrary"),
        name="dsa_s_attn",
    )(page_table, rel_bias, q, madd, k_new, v_new,
      *([cache_k] * PAGES_PER_STEP), *([cache_v] * PAGES_PER_STEP))


def _mem_kv_kernel(x_ref, g_ref, w_ref, k_ref, v_ref):
    h = _rms(x_ref[...], g_ref[...]).astype(BF16)
    width = k_ref.shape[-1]
    k_ref[...] = _dot(h, w_ref[:, 0:width])
    v_ref[...] = _dot(h, w_ref[:, width:2 * width])


def _mem_kv(mem, g, wkv, tm):
    n, d = mem.shape
    width = wkv.shape[1] // 2
    return pl.pallas_call(
        _mem_kv_kernel,
        grid=(n // tm,),
        in_specs=[pl.BlockSpec((tm, d), lambda i: (i, 0)),
                  pl.BlockSpec((1, d), lambda i: (0, 0)),
                  pl.BlockSpec(wkv.shape, lambda i: (0, 0))],
        out_specs=[pl.BlockSpec((tm, width), lambda i: (i, 0))] * 2,
        out_shape=[jax.ShapeDtypeStruct((n, width), F32)] * 2,
        compiler_params=_cparams("arbitrary"),
        name="mem_kv",
    )(mem, g, wkv)


def _split_bf16(x):
    hi = x.astype(BF16)
    return hi, (x - hi.astype(F32)).astype(BF16)


def _mix_kernel(x_ref, yrw_ref, ydsa_ref, mk_ref, mv_ref, woa_ref, wob_ref, gx_ref, wq_ref, wo_ref,
                gf_ref, rwh_ref, rwl_ref, rb_ref, x2_ref, h3_ref, comb_ref):
    x1 = x_ref[0] + _dot(yrw_ref[0], woa_ref[...]) + _dot(ydsa_ref[0], wob_ref[...])
    h2 = _rms(x1, gx_ref[...]).astype(BF16)
    qx = _dot(h2, wq_ref[...])
    mk = mk_ref[0].astype(BF16)
    mv = mv_ref[0].astype(BF16)
    heads = []
    for h in range(XA_H):
        sl = slice(XA_HEAD * h, XA_HEAD * (h + 1))
        lg = _dot_nt(qx[:, sl].astype(BF16), mk[:, sl]) * (XA_HEAD ** -0.5)
        p = jnp.exp(lg - jnp.max(lg, axis=1, keepdims=True))
        l = jnp.sum(p, axis=1, keepdims=True)
        heads.append(_dot(p.astype(BF16), mv[:, sl]) / l)
    o = jnp.concatenate(heads, axis=1).astype(BF16)
    x2 = x1 + _dot(o, wo_ref[...])
    x2_ref[0] = x2
    h3 = _rms(x2, gf_ref[...])
    h3_ref[0] = h3.astype(BF16)

    hh, hl = _split_bf16(h3)
    logits = (_dot(hh, rwh_ref[...]) + _dot(hh, rwl_ref[...]) + _dot(hl, rwh_ref[...])
              + rb_ref[...])
    lane = lax.broadcasted_iota(I32, logits.shape, 1)
    work = logits
    vals, idxs = [], []
    for _ in range(TOP_K):
        mx = jnp.max(work, axis=1, keepdims=True)
        ix = jnp.min(jnp.where(work == mx, lane, N_EXPERTS), axis=1, keepdims=True)
        vals.append(mx)
        idxs.append(ix)
        work = jnp.where(lane == ix, -jnp.inf, work)
    es = [jnp.exp(v - vals[0]) for v in vals]
    den = es[0] + es[1] + es[2] + es[3]
    comb = jnp.zeros(logits.shape, F32)
    for e, ix in zip(es, idxs):
        comb = comb + jnp.where(lane == ix, e / den, 0.0)
    comb_ref[0] = comb


def _mix(x, yrw, ydsa, mk, mv, W, tm):
    b, t, d = x.shape
    mem_len, xa_w = mk.shape[1], mk.shape[2]
    full = lambda a: pl.BlockSpec(a.shape, lambda i, j: (0,) * a.ndim)
    rspec = lambda w: pl.BlockSpec((1, tm, w), lambda i, j: (i, j, 0))
    mspec = pl.BlockSpec((1, mem_len, xa_w), lambda i, j: (i, 0, 0))
    params = [W['woa'], W['wob'], W['gx'], W['wq'], W['wo'], W['gf'], W['rwh'], W['rwl'], W['rb']]
    return pl.pallas_call(
        _mix_kernel,
        grid=(b, t // tm),
        in_specs=[rspec(d), rspec(RW_W), rspec(DSA_W), mspec, mspec] + [full(a) for a in params],
        out_specs=[rspec(d), rspec(d), rspec(N_EXPERTS)],
        out_shape=[jax.ShapeDtypeStruct((b, t, d), F32), jax.ShapeDtypeStruct((b, t, d), BF16),
                   jax.ShapeDtypeStruct((b, t, N_EXPERTS), F32)],
        compiler_params=_cparams("arbitrary", "arbitrary"),
        name="mix",
    )(x, yrw, ydsa, mk, mv, *params)


def _moe_kernel(h_ref, comb_ref, x_ref, w1_ref, b1_ref, w2_ref, b2_ref, gfin_ref, o_ref, acc_scr):
    e = pl.program_id(1)
    d_ff = w2_ref.shape[1]

    @pl.when(e == 0)
    def _():
        acc_scr[...] = jnp.zeros(acc_scr.shape, F32)

    h = h_ref[...]
    gate = jnp.minimum(_dot(h, w1_ref[0, :, 0:d_ff]) + b1_ref[0, :, 0:d_ff], SWIGLU_LIMIT)
    up = jnp.clip(_dot(h, w1_ref[0, :, d_ff:2 * d_ff]) + b1_ref[0, :, d_ff:2 * d_ff],
                  -SWIGLU_LIMIT, SWIGLU_LIMIT)
    glu = gate * _sigmoid(SWIGLU_ALPHA * gate)
    out = _dot(((up + 1.0) * glu).astype(BF16), w2_ref[0]) + b2_ref[0]
    comb = comb_ref[...]
    lane = lax.broadcasted_iota(I32, comb.shape, 1)
    wcol = jnp.sum(jnp.where(lane == e, comb, 0.0), axis=1, keepdims=True)
    acc_scr[...] += wcol * out

    @pl.when(e == pl.num_programs(1) - 1)
    def _():
        o_ref[...] = _rms(x_ref[...] + acc_scr[...], gfin_ref[...])


def _moe(h3, comb, x2, w1, b1, w2, b2, gfin, tm):
    n, d = h3.shape
    n_exp, _, two_ff = w1.shape
    d_ff = two_ff // 2
    return pl.pallas_call(
        _moe_kernel,
        grid=(n // tm, n_exp),
        in_specs=[pl.BlockSpec((tm, d), lambda i, e: (i, 0)),
                  pl.BlockSpec((tm, n_exp), lambda i, e: (i, 0)),
                  pl.BlockSpec((tm, d), lambda i, e: (i, 0)),
                  pl.BlockSpec((1, d, two_ff), lambda i, e: (e, 0, 0)),
                  pl.BlockSpec((1, 1, two_ff), lambda i, e: (e, 0, 0)),
                  pl.BlockSpec((1, d_ff, d), lambda i, e: (e, 0, 0)),
                  pl.BlockSpec((1, 1, d), lambda i, e: (e, 0, 0)),
                  pl.BlockSpec((1, d), lambda i, e: (0, 0))],
        out_specs=pl.BlockSpec((tm, d), lambda i, e: (i, 0)),
        out_shape=jax.ShapeDtypeStruct((n, d), F32),
        scratch_shapes=[pltpu.VMEM((tm, d), F32)],
        compiler_params=_cparams("arbitrary", "arbitrary"),
        name="moe",
    )(h3, comb, x2, w1, b1, w2, b2, gfin)


def _pick_tile(n, pref):
    t = min(pref, n)
    while n % t:
        t //= 2
    return t


def kernel(x_prompt, mem_prompt, x_sample, cache_k, cache_v, cache_idx_k, cache_mem_k, cache_mem_v, state_rwkv, state_shift, page_table, rel_bias, norm_final, norm_mix, w_in, mu_shift, rw_w0, rw_w2, rw_a0, rw_a2, rw_g2, rw_k_k, rw_k_a, rw_r_k, rw_ln_w, rw_ln_b, w_out, norm_xattn, norm_mem, xa_wq, xa_wk, xa_wv, xa_wo, norm_ffn, router_w, router_b, moe_w1, moe_b1, moe_w2, moe_b2):
    depth = w_in.shape[0]
    assert depth == 1, "single-layer step"
    l = 0
    bp, tp, d = x_prompt.shape
    bs, ts, _ = x_sample.shape
    n_pool = cache_k.shape[1]
    n_pages = page_table.shape[1]
    past = n_pages * PAGE_SIZE
    mem_len = mem_prompt.shape[1]
    xa_w = XA_H * XA_HEAD
    rw_cols = state_shift.shape[-1]
    assert tp % CW == 0 and n_pages % SCORE_PAGES == 0 and ts <= SROWS

    row = lambda a: a.reshape(1, -1)
    bf = lambda a: a.astype(BF16)

    w_pad = bf(jnp.pad(w_in[l], ((0, 0), (0, C_END - w_in.shape[-1]))))
    lora = rw_w2.shape[1]
    P = {
        'mu': row(mu_shift[l]), 'w0': row(rw_w0[l]), 'a0': row(rw_a0[l]),
        'w2': bf(jnp.pad(rw_w2[l], ((0, LANES - lora), (0, 0)))),
        'a2': bf(jnp.pad(rw_a2[l], ((lora, LANES - lora - rw_a2.shape[1]), (0, 0)))),
        'g2': bf(rw_g2[l]), 'kk': row(rw_k_k[l]), 'ka': row(rw_k_a[l]),
        'rk': rw_r_k[l].reshape(RW_H, RW_HEAD),
        'lnw': rw_ln_w[l].reshape(RW_H, RW_HEAD), 'lnb': rw_ln_b[l].reshape(RW_H, RW_HEAD),
    }
    rwh = bf(router_w[l])
    W = {
        'woa': bf(w_out[l][:RW_W]), 'wob': bf(w_out[l][RW_W:]), 'gx': row(norm_xattn[l]),
        'wq': bf(xa_wq[l]), 'wo': bf(xa_wo[l]), 'gf': row(norm_ffn[l]),
        'rwh': rwh, 'rwl': bf(router_w[l] - rwh.astype(F32)), 'rb': row(router_b[l]),
    }
    w1 = bf(moe_w1[l])
    w2 = bf(moe_w2[l])
    b1 = moe_b1[l][:, None, :]
    b2 = moe_b2[l][:, None, :]
    gfin = row(norm_final)
    gmix = row(norm_mix[l])

    prw, q, k, v, qi, kiw, kT, vb, kiT = _in_proj(x_prompt, gmix, w_pad, _pick_tile(tp, 256))
    y_rw, p_s1, p_sh1 = _rwkv_chunk(prw, jnp.zeros((bp, 1, rw_cols), F32),
                                    jnp.zeros((bp, RW_H, RW_HEAD, RW_HEAD), F32), P,
                                    nb=bp, tc=_pick_tile(tp, 256), passes=1)
    y_dsa = _dsa_prompt(rel_bias, q, qi, kiw, kT, vb, kiT)
    wkv = bf(jnp.concatenate([xa_wk[l], xa_wv[l]], axis=1))
    mk_p, mv_p = _mem_kv(mem_prompt.reshape(bp * mem_len, d), row(norm_mem[l]), wkv,
                         _pick_tile(bp * mem_len, 256))
    mk_p = mk_p.reshape(bp, mem_len, xa_w)
    mv_p = mv_p.reshape(bp, mem_len, xa_w)
    x2, h3, comb = _mix(x_prompt, y_rw, y_dsa, mk_p, mv_p, W, _pick_tile(tp, 256))
    n_p = bp * tp
    y_prompt = _moe(h3.reshape(n_p, d), comb.reshape(n_p, N_EXPERTS), x2.reshape(n_p, d),
                    w1, b1, w2, b2, gfin, _pick_tile(n_p, 512)).reshape(bp, tp, d)

    xs = jnp.pad(x_sample, ((0, 0), (0, SROWS - ts), (0, 0)))
    n_s = bs * SROWS
    sprw, sq, sk, sv, sqi, skiw, _, svb, _ = _in_proj(xs.reshape(1, n_s, d), gmix, w_pad,
                                                      _pick_tile(n_s, 256))
    seq = lambda a: a.reshape(bs, SROWS, a.shape[-1])
    sprw, sq, sk, sv, sqi, skiw, svb = map(seq, (sprw, sq, sk, sv, sqi, skiw, svb))
    sy_rw, s_s1, s_sh1 = _rwkv(sprw, state_shift[l][:, None, :], state_rwkv[l], P,
                               nb=_pick_tile(bs, 4), tc=SROWS, n_valid=ts)

    qi_rows = sqi[:, :ts].reshape(bs, ts * IDX_H, IDX_D)
    wcol = skiw[:, :ts, IDX_D:IDX_D + IDX_H].reshape(bs, ts * IDX_H, 1)
    ki_new = bf(skiw[:, :, :IDX_D])
    scores = _dsa_s_scores(page_table, jnp.transpose(cache_idx_k[l], (0, 2, 1)), qi_rows, wcol,
                           ki_new, ts)
    width = scores.shape[-1]
    tok = jnp.arange(SROWS, dtype=I32)
    lim = jnp.tile(jnp.where(tok < ts, past + tok, -1), bs).reshape(n_s, 1)
    k_sel = min(TOPK_MAX, (past + ts) // 4)
    madd = _topk_rows(scores.reshape(n_s, width), lim, k_sel, _pick_tile(n_s, 64))
    madd = madd.reshape(bs, SROWS, width)
    sy_dsa = _dsa_s_attn(page_table, rel_bias,
                         jnp.transpose(cache_k[l], (0, 2, 3, 1)),
                         jnp.transpose(cache_v[l], (0, 2, 3, 1)),
                         sq, madd, bf(sk), svb)
    sx2, sh3, scomb = _mix(xs, sy_rw, sy_dsa, cache_mem_k[l].reshape(bs, mem_len, xa_w),
                           cache_mem_v[l].reshape(bs, mem_len, xa_w), W, SROWS)
    y_s = _moe(sh3.reshape(n_s, d), scomb.reshape(n_s, N_EXPERTS), sx2.reshape(n_s, d),
               w1, b1, w2, b2, gfin, _pick_tile(n_s, 256)).reshape(bs, SROWS, d)

    heads = lambda a, b_, t_: a.reshape(1, b_, t_, DSA_H, DSA_HEAD)
    return (y_prompt, y_s[:, :ts],
            p_s1[None], p_sh1.reshape(1, bp, rw_cols),
            heads(k, bp, tp), heads(v, bp, tp), kiw[:, :, :IDX_D][None],
            mk_p.reshape(1, bp, mem_len, XA_H, XA_HEAD), mv_p.reshape(1, bp, mem_len, XA_H, XA_HEAD),
            s_s1[None], s_sh1.reshape(1, bs, rw_cols),
            heads(sk[:, :ts], bs, ts), heads(sv[:, :ts], bs, ts), skiw[:, :ts, :IDX_D][None])
```

```python
import functools
import math

import numpy as np
import jax
import jax.numpy as jnp
from jax import lax
from jax.experimental import pallas as pl
from jax.experimental.pallas import tpu as pltpu

F32 = jnp.float32
BF16 = jnp.bfloat16
I32 = jnp.int32

LANES = 128
VMEM_LIMIT = 56 * 1024 * 1024

NORM_EPS = 1e-5
RW_EPS = 64e-5
NEG_INF = -1e30
RW_H = 8
RW_HEAD = 64
RW_W = RW_H * RW_HEAD
DSA_H = 8
DSA_HEAD = 64
DSA_W = DSA_H * DSA_HEAD
IDX_H = 8
IDX_D = 64
TOPK_MAX = 256
XA_H = 4
XA_HEAD = 128
N_EXPERTS = 32
TOP_K = 4
SWIGLU_LIMIT = 7.0
SWIGLU_ALPHA = 1.702
NUM_BUCKETS = 32
MAX_DISTANCE = 128
PAGE_SIZE = 128
IDX_SCALE = IDX_H ** -0.5 * IDX_D ** -0.5

QB = 128
CW = 512
ATTN_GROUP = 8
PAGES_PER_STEP = 16

INT_MIN = -2 ** 31
INT_MAX = 2 ** 31 - 1


def _bucket_starts():
    max_exact = NUM_BUCKETS // 2
    d = np.arange(0, 4 * MAX_DISTANCE)
    large = max_exact + (np.log(np.maximum(d, 1).astype(np.float32) / np.float32(max_exact))
                         / np.float32(math.log(MAX_DISTANCE / max_exact))
                         * np.float32(NUM_BUCKETS - max_exact)).astype(np.int32)
    b = np.where(d < max_exact, d, np.minimum(large, NUM_BUCKETS - 1))
    starts = []
    for j in range(NUM_BUCKETS):
        hit = np.nonzero(b == j)[0]
        starts.append(int(hit[0]) if hit.size else None)
    return starts


BUCKET_START = _bucket_starts()


def _cparams(*sem):
    return pltpu.CompilerParams(dimension_semantics=sem, vmem_limit_bytes=VMEM_LIMIT)


def _rms(x, g):
    return x * lax.rsqrt(jnp.mean(x * x, axis=-1, keepdims=True) + NORM_EPS) * g


def _dot(a, b):
    return jnp.dot(a, b, preferred_element_type=F32)


def _dot_nt(a, b):
    return lax.dot_general(a, b, (((1,), (1,)), ((), ())), preferred_element_type=F32)


def _softplus(x):
    return jnp.maximum(x, 0.0) + jnp.log1p(jnp.exp(-jnp.abs(x)))


def _sigmoid(x):
    return 1.0 / (1.0 + jnp.exp(-x))


C_RW = 0
C_Q = 1792
C_K = C_Q + DSA_W
C_V = C_K + DSA_W
C_QI = C_V + DSA_W
C_KIW = C_QI + IDX_H * IDX_D
C_END = C_KIW + LANES


def _in_proj_kernel(x_ref, g_ref, w_ref, prw_ref, q_ref, k_ref, v_ref, qi_ref, kiw_ref,
                    kT_ref, vb_ref, kiT_ref):
    h = _rms(x_ref[0], g_ref[...]).astype(BF16)

    def mm(lo, hi):
        return _dot(h, w_ref[:, lo:hi])

    prw_ref[0] = mm(C_RW, C_Q)
    q_ref[0] = mm(C_Q, C_K).astype(BF16)
    k = mm(C_K, C_V)
    k_ref[0] = k
    kT_ref[0] = k.T.astype(BF16)
    v = mm(C_V, C_QI)
    v_ref[0] = v
    vb_ref[0] = v.astype(BF16)
    qi_ref[0] = mm(C_QI, C_KIW).astype(BF16)
    kiw = mm(C_KIW, C_END)
    kiw_ref[0] = kiw
    kiT_ref[0] = kiw.T[0:IDX_D, :].astype(BF16)


def _in_proj(x, g, w_pad, tm):
    b, t, d = x.shape
    grid = (b, t // tm)
    row = lambda w, dt: jax.ShapeDtypeStruct((b, t, w), dt)
    rspec = lambda w: pl.BlockSpec((1, tm, w), lambda i, j: (i, j, 0))
    tspec = lambda w: pl.BlockSpec((1, w, tm), lambda i, j: (i, 0, j))
    return pl.pallas_call(
        _in_proj_kernel,
        grid=grid,
        in_specs=[rspec(d),
                  pl.BlockSpec((1, d), lambda i, j: (0, 0)),
                  pl.BlockSpec((d, C_END), lambda i, j: (0, 0))],
        out_specs=[rspec(C_Q), rspec(DSA_W), rspec(DSA_W), rspec(DSA_W), rspec(IDX_H * IDX_D),
                   rspec(LANES), tspec(DSA_W), rspec(DSA_W), tspec(IDX_D)],
        out_shape=[row(C_Q, F32), row(DSA_W, BF16), row(DSA_W, F32), row(DSA_W, F32),
                   row(IDX_H * IDX_D, BF16), row(LANES, F32),
                   jax.ShapeDtypeStruct((b, DSA_W, t), BF16), row(DSA_W, BF16),
                   jax.ShapeDtypeStruct((b, IDX_D, t), BF16)],
        compiler_params=_cparams("arbitrary", "arbitrary"),
        name="in_proj",
    )(x, g, w_pad)


def _rwkv_kernel(n_steps, prw_ref, sh0_ref, s0_ref, mu_ref, w0_ref, w2_ref, a0_ref, a2_ref, g2_ref,
                 kkw_ref, kaw_ref, rk_ref, lnw_ref, lnb_ref,
                 y_ref, s1_ref, sh1_ref,
                 st_scr, carry_scr, r_s, w_s, k_s, v_s, kk_s, b_s, y_s, g_s):
    nb, tc = prw_ref.shape[0], prw_ref.shape[1]
    c = pl.program_id(1)

    @pl.when(c == 0)
    def _():
        st_scr[...] = s0_ref[...]
        carry_scr[...] = sh0_ref[...]

    row = lax.broadcasted_iota(I32, (tc, 1), 0)
    lane = lax.broadcasted_iota(I32, (1, LANES), 1)
    for b in range(nb):
        x = prw_ref[b]
        prev = jnp.where(row == 0, carry_scr[b], pltpu.roll(x, 1, 0))
        carry_scr[b] = x[n_steps - 1:n_steps, :]
        ps = x + (prev - x) * mu_ref[...]
        r = ps[:, 0:RW_W]
        k = ps[:, RW_W:2 * RW_W]
        v = ps[:, 2 * RW_W:3 * RW_W]
        la = ps[:, 3 * RW_W:3 * RW_W + LANES]
        gd = ps[:, 3 * RW_W + LANES:3 * RW_W + 2 * LANES]
        z = jnp.where(lane < 64, jnp.tanh(la), la).astype(BF16)
        w_log = -_softplus(-(w0_ref[...] + _dot(z, w2_ref[...]))) - 0.5
        decay = jnp.exp(-jnp.exp(w_log))
        a = _sigmoid(a0_ref[...] + _dot(z, a2_ref[...]))
        g_s[b] = _dot(_sigmoid(gd).astype(BF16), g2_ref[...])
        kk = k * kkw_ref[...]
        k2 = k * (1.0 + (a - 1.0) * kaw_ref[...])
        for h in range(RW_H):
            sl = slice(RW_HEAD * h, RW_HEAD * (h + 1))
            kkh = kk[:, sl]
            nrm = jnp.sqrt(jnp.sum(kkh * kkh, axis=-1, keepdims=True))
            kkh = kkh / jnp.maximum(nrm, 1e-12)
            r_s[b, h] = r[:, sl]
            w_s[b, h] = decay[:, sl]
            k_s[b, h] = k2[:, sl]
            v_s[b, h] = v[:, sl]
            kk_s[b, h] = kkh
            b_s[b, h] = kkh * a[:, sl]

    eye = (lax.broadcasted_iota(I32, (RW_HEAD, RW_HEAD), 0)
           == lax.broadcasted_iota(I32, (RW_HEAD, RW_HEAD), 1))

    def step(t, carry):
        for b in range(nb):
            for h in range(RW_H):
                s = st_scr[b, h]
                tt = pl.ds(t, 1)
                sa = -jnp.sum(s * kk_s[b, h, tt, :], axis=1, keepdims=True)
                vcol = jnp.sum(jnp.where(eye, v_s[b, h, tt, :], 0.0), axis=1, keepdims=True)
                s = s * w_s[b, h, tt, :] + sa * b_s[b, h, tt, :] + vcol * k_s[b, h, tt, :]
                st_scr[b, h] = s
                ycol = jnp.sum(s * r_s[b, h, tt, :], axis=1, keepdims=True)
                y_s[b, h, tt, :] = jnp.sum(jnp.where(eye, ycol, 0.0), axis=0, keepdims=True)
        return carry

    lax.fori_loop(0, n_steps, step, 0)

    for b in range(nb):
        for h in range(RW_H):
            sl = slice(RW_HEAD * h, RW_HEAD * (h + 1))
            y = y_s[b, h]
            mean = jnp.mean(y, axis=-1, keepdims=True)
            var = jnp.mean(jnp.square(y - mean), axis=-1, keepdims=True)
            yn = (y - mean) * lax.rsqrt(var + RW_EPS) * lnw_ref[h:h + 1, :] + lnb_ref[h:h + 1, :]
            rr, kk2, vv = r_s[b, h], k_s[b, h], v_s[b, h]
            bonus = jnp.sum(rr * kk2 * rk_ref[h:h + 1, :], axis=-1, keepdims=True) * vv
            y_ref[b, :, sl] = ((yn + bonus) * g_s[b][:, sl]).astype(BF16)
    s1_ref[...] = st_scr[...]
    sh1_ref[...] = carry_scr[...]


def _rwkv(prw, shift0, s0, P, nb, tc, n_valid):
    b, t, cols = prw.shape
    n_steps = min(tc, n_valid)
    grid = (b // nb, t // tc)
    full = lambda a: pl.BlockSpec(a.shape, lambda i, j: (0,) * a.ndim)
    params = [P['mu'], P['w0'], P['w2'], P['a0'], P['a2'], P['g2'], P['kk'], P['ka'], P['rk'],
              P['lnw'], P['lnb']]
    hs = lambda: pltpu.VMEM((nb, RW_H, tc, RW_HEAD), F32)
    return pl.pallas_call(
        functools.partial(_rwkv_kernel, n_steps),
        grid=grid,
        in_specs=[pl.BlockSpec((nb, tc, cols), lambda i, j: (i, j, 0)),
                  pl.BlockSpec((nb, 1, cols), lambda i, j: (i, 0, 0)),
                  pl.BlockSpec((nb, RW_H, RW_HEAD, RW_HEAD), lambda i, j: (i, 0, 0, 0))]
                 + [full(a) for a in params],
        out_specs=[pl.BlockSpec((nb, tc, RW_W), lambda i, j: (i, j, 0)),
                   pl.BlockSpec((nb, RW_H, RW_HEAD, RW_HEAD), lambda i, j: (i, 0, 0, 0)),
                   pl.BlockSpec((nb, 1, cols), lambda i, j: (i, 0, 0))],
        out_shape=[jax.ShapeDtypeStruct((b, t, RW_W), BF16),
                   jax.ShapeDtypeStruct((b, RW_H, RW_HEAD, RW_HEAD), F32),
                   jax.ShapeDtypeStruct((b, 1, cols), F32)],
        scratch_shapes=[pltpu.VMEM((nb, RW_H, RW_HEAD, RW_HEAD), F32),
                        pltpu.VMEM((nb, 1, cols), F32),
                        hs(), hs(), hs(), hs(), hs(), hs(), hs(),
                        pltpu.VMEM((nb, tc, RW_W), F32)],
        compiler_params=_cparams("arbitrary", "arbitrary"),
        name="rwkv",
    )(prw, shift0, s0, *params)


RC = 64
RC_GROUP = 16


def _split3(x):
    hi = x.astype(BF16)
    r1 = x - hi.astype(F32)
    mid = r1.astype(BF16)
    lo = (r1 - mid.astype(F32)).astype(BF16)
    return hi, mid, lo


def _mm(a, b, passes):
    if passes == 1:
        return _dot(a.astype(BF16), b.astype(BF16))
    ah, al = _split_bf16(a)
    bh, bl = _split_bf16(b)
    return _dot(ah, bh) + _dot(ah, bl) + _dot(al, bh)


def _mm_nt(a, b, passes):
    if passes == 1:
        return _dot_nt(a.astype(BF16), b.astype(BF16))
    ah, al = _split_bf16(a)
    bh, bl = _split_bf16(b)
    return _dot_nt(ah, bh) + _dot_nt(ah, bl) + _dot_nt(al, bh)


def _exact_left(m_bf16, x):
    hi, mid, lo = _split3(x)
    return _dot(m_bf16, hi) + _dot(m_bf16, mid) + _dot(m_bf16, lo)


def _exact_right(x, m_bf16):
    hi, mid, lo = _split3(x)
    return _dot(hi, m_bf16) + _dot(mid, m_bf16) + _dot(lo, m_bf16)


def _rwkv_chunk_kernel(passes, prw_ref, sh0_ref, s0_ref, mu_ref, w0_ref, w2_ref, a0_ref, a2_ref,
                       g2_ref, kkw_ref, kaw_ref, rk_ref, lnw_ref, lnb_ref, ltri_ref, lones_ref,
                       hones_ref, y_ref, s1_ref, sh1_ref,
                       st_scr, carry_scr, rh_s, kkh_s, bt_s, kt_s, bg_s, kg_s, v_s, gc_s, y_s,
                       bonus_s, g_s):
    nb, tc = prw_ref.shape[0], prw_ref.shape[1]
    nh = nb * RW_H
    c = pl.program_id(1)

    @pl.when(c == 0)
    def _():
        for b in range(nb):
            for h in range(RW_H):
                st_scr[b * RW_H + h] = s0_ref[b, h]
        carry_scr[...] = sh0_ref[...]

    row = lax.broadcasted_iota(I32, (tc, 1), 0)
    lane = lax.broadcasted_iota(I32, (1, LANES), 1)
    hones = hones_ref[...]
    for b in range(nb):
        x = prw_ref[b]
        prev = jnp.where(row == 0, carry_scr[b], pltpu.roll(x, 1, 0))
        carry_scr[b] = x[tc - 1:tc, :]
        ps = x + (prev - x) * mu_ref[...]
        r = ps[:, 0:RW_W]
        k = ps[:, RW_W:2 * RW_W]
        v = ps[:, 2 * RW_W:3 * RW_W]
        la = ps[:, 3 * RW_W:3 * RW_W + LANES]
        gd = ps[:, 3 * RW_W + LANES:3 * RW_W + 2 * LANES]
        z = jnp.where(lane < 64, jnp.tanh(la), la).astype(BF16)
        w_log = -_softplus(-(w0_ref[...] + _dot(z, w2_ref[...]))) - 0.5
        lw = -jnp.exp(w_log)
        a = _sigmoid(a0_ref[...] + _dot(z, a2_ref[...]))
        g_s[b] = _dot(_sigmoid(gd).astype(BF16), g2_ref[...])
        kk = k * kkw_ref[...]
        kk = kk / jnp.maximum(jnp.sqrt(_exact_right(kk * kk, hones)), 1e-12)
        k2 = k * (1.0 + (a - 1.0) * kaw_ref[...])
        bb = kk * a
        bonus_s[b] = _exact_right(r * k2 * rk_ref[...], hones) * v
        gsum = _exact_left(ltri_ref[...], lw)
        gend = _exact_left(lones_ref[...], lw)
        e_neg = jnp.exp(-gsum)
        e_end = jnp.exp(gend - gsum)
        cols = {'rh': r * jnp.exp(gsum), 'kkh': kk * jnp.exp(gsum - lw), 'bt': bb * e_neg,
                'kt': k2 * e_neg, 'bg': bb * e_end, 'kg': k2 * e_end, 'v': v, 'gc': jnp.exp(gend)}
        dst = {'rh': rh_s, 'kkh': kkh_s, 'bt': bt_s, 'kt': kt_s, 'bg': bg_s, 'kg': kg_s, 'v': v_s,
               'gc': gc_s}
        for name, val in cols.items():
            for h in range(RW_H):
                dst[name][b * RW_H + h] = val[:, RW_HEAD * h:RW_HEAD * (h + 1)]

    ri = lax.broadcasted_iota(I32, (RC, RC), 0)
    ci = lax.broadcasted_iota(I32, (RC, RC), 1)
    strict = ri > ci
    incl = ri >= ci
    eye = ri == ci
    eye_f = jnp.where(eye, 1.0, 0.0)
    n_double = int(math.log2(RC)) - 1

    def head_group(heads, sub):
        base = pl.multiple_of(sub * RC, RC)
        rows = pl.ds(base, RC)
        mm = lambda a, b: _mm(a, b, passes)
        each = lambda f: [f(i) for i in range(len(heads))]
        rh = [rh_s[hd, rows, :] for hd in heads]
        kkh = [kkh_s[hd, rows, :] for hd in heads]
        vv = [v_s[hd, rows, :] for hd in heads]
        gram = each(lambda i: _mm_nt(
            jnp.concatenate([kkh[i], rh[i]], axis=0),
            jnp.concatenate([bt_s[heads[i], rows, :], kt_s[heads[i], rows, :]], axis=0), passes))
        mb = each(lambda i: jnp.where(strict, gram[i][0:RC, 0:RC], 0.0))
        mkv = each(lambda i: mm(jnp.where(strict, gram[i][0:RC, RC:2 * RC], 0.0), vv[i]))
        nkv = each(lambda i: mm(jnp.where(incl, gram[i][RC:2 * RC, RC:2 * RC], 0.0), vv[i]))
        nb_ = each(lambda i: jnp.where(incl, gram[i][RC:2 * RC, 0:RC], 0.0))
        inv = each(lambda i: eye_f - mb[i])
        q = mb
        for _ in range(n_double):
            q = each(lambda i: mm(q[i], q[i]))
            inv = each(lambda i: inv[i] + mm(inv[i], q[i]))
        u = each(lambda i: -mm(inv[i], jnp.concatenate([kkh[i], mkv[i]], axis=1)))
        yy = each(lambda i: mm(nb_[i], u[i]))
        zz = each(lambda i: mm(u[i].T, bg_s[heads[i], rows, :]))
        vk = each(lambda i: mm(vv[i].T, kg_s[heads[i], rows, :]))
        s = [st_scr[hd] for hd in heads]
        ys = each(lambda i: _mm_nt(rh[i] + yy[i][:, 0:RC], s[i], passes))
        z1 = each(lambda i: jnp.where(eye, gc_s[heads[i], pl.ds(base, 1), :], 0.0)
                  + zz[i][0:RW_HEAD])
        sn = each(lambda i: mm(s[i], z1[i]))
        for i, hd in enumerate(heads):
            y_s[hd, rows, :] = ys[i] + yy[i][:, RC:2 * RC] + nkv[i]
            st_scr[hd] = sn[i] + zz[i][RW_HEAD:2 * RW_HEAD] + vk[i]

    def chunk_step(sub, carry):
        for g0 in range(0, nh, RC_GROUP):
            head_group(list(range(g0, min(g0 + RC_GROUP, nh))), sub)
        return carry

    lax.fori_loop(0, tc // RC, chunk_step, 0)

    for b in range(nb):
        for h in range(RW_H):
            sl = slice(RW_HEAD * h, RW_HEAD * (h + 1))
            y = y_s[b * RW_H + h]
            mean = jnp.mean(y, axis=-1, keepdims=True)
            var = jnp.mean(jnp.square(y - mean), axis=-1, keepdims=True)
            yn = (y - mean) * lax.rsqrt(var + RW_EPS) * lnw_ref[h:h + 1, :] + lnb_ref[h:h + 1, :]
            y_ref[b, :, sl] = ((yn + bonus_s[b, :, sl]) * g_s[b, :, sl]).astype(BF16)
            s1_ref[b, h] = st_scr[b * RW_H + h]
    sh1_ref[...] = carry_scr[...]


def _rwkv_chunk(prw, shift0, s0, P, nb, tc, passes):
    b, t, cols = prw.shape
    tok = np.arange(tc)
    same = (tok[:, None] // RC) == (tok[None, :] // RC)
    ltri = jnp.asarray(same & (tok[:, None] >= tok[None, :]), BF16)
    lones = jnp.asarray(same, BF16)
    col = np.arange(RW_W)
    hones = jnp.asarray((col[:, None] // RW_HEAD) == (col[None, :] // RW_HEAD), BF16)
    full = lambda a: pl.BlockSpec(a.shape, lambda i, j: (0,) * a.ndim)
    params = [P['mu'], P['w0'], P['w2'], P['a0'], P['a2'], P['g2'], P['kk'], P['ka'],
              P['rk'].reshape(1, RW_W), P['lnw'], P['lnb'], ltri, lones, hones]
    nh = nb * RW_H
    hs = lambda: pltpu.VMEM((nh, tc, RW_HEAD), F32)
    return pl.pallas_call(
        functools.partial(_rwkv_chunk_kernel, passes),
        grid=(b // nb, t // tc),
        in_specs=[pl.BlockSpec((nb, tc, cols), lambda i, j: (i, j, 0)),
                  pl.BlockSpec((nb, 1, cols), lambda i, j: (i, 0, 0)),
                  pl.BlockSpec((nb, RW_H, RW_HEAD, RW_HEAD), lambda i, j: (i, 0, 0, 0))]
                 + [full(a) for a in params],
        out_specs=[pl.BlockSpec((nb, tc, RW_W), lambda i, j: (i, j, 0)),
                   pl.BlockSpec((nb, RW_H, RW_HEAD, RW_HEAD), lambda i, j: (i, 0, 0, 0)),
                   pl.BlockSpec((nb, 1, cols), lambda i, j: (i, 0, 0))],
        out_shape=[jax.ShapeDtypeStruct((b, t, RW_W), BF16),
                   jax.ShapeDtypeStruct((b, RW_H, RW_HEAD, RW_HEAD), F32),
                   jax.ShapeDtypeStruct((b, 1, cols), F32)],
        scratch_shapes=[pltpu.VMEM((nh, RW_HEAD, RW_HEAD), F32),
                        pltpu.VMEM((nb, 1, cols), F32),
                        hs(), hs(), hs(), hs(), hs(), hs(), hs(), hs(), hs(),
                        pltpu.VMEM((nb, tc, RW_W), F32), pltpu.VMEM((nb, tc, RW_W), F32)],
        compiler_params=_cparams("arbitrary", "arbitrary"),
        name="rwkv_chunk",
    )(prw, shift0, s0, *params)


def _pattern_to_f32(u):
    key = u ^ INT_MIN
    return lax.bitcast_convert_type(jnp.where(key >= 0, key, key ^ 0x7FFFFFFF), F32)


def _topk_mask(sc_ref, madd_ref, p_scr, rows, nch, k_sel, lim, idx_bits):
    ntile = CW // LANES
    lane = lax.broadcasted_iota(I32, (rows, LANES), 1)

    def count(pred):
        def body(c, acc):
            off = pl.multiple_of(c * CW, CW)
            x = sc_ref[:, pl.ds(off, CW)]
            for t in range(ntile):
                acc = acc + pred(x[:, LANES * t:LANES * (t + 1)], off + LANES * t)
            return acc
        acc = lax.fori_loop(0, nch, body, jnp.zeros((rows, LANES), F32))
        return jnp.sum(acc, axis=1, keepdims=True)

    def bcast(col):
        return jnp.broadcast_to(col, (rows, LANES))

    def searching(state):
        it, alive, _, _ = state
        return (it < 32) & (alive > 0)

    def value_bit(state):
        it, _, tau, cnt_tau = state
        cand = tau | jnp.left_shift(jnp.int32(1), 31 - it)
        cb = bcast(_pattern_to_f32(cand))
        cnt = count(lambda x, off: jnp.where(x >= cb, 1.0, 0.0))
        take = cnt >= k_sel
        tau = jnp.where(take, cand, tau)
        cnt_tau = jnp.where(take, cnt, cnt_tau)
        settled = jnp.where(cnt_tau == k_sel, 1, jnp.where(lim < 0, 1, 0))
        return it + 1, 1 - jnp.min(settled), tau, cnt_tau

    _, _, tau, _ = lax.while_loop(
        searching, value_bit,
        (jnp.int32(0), jnp.int32(1), jnp.zeros((rows, 1), I32),
         jnp.zeros((rows, 1), F32) + jnp.asarray(nch * CW, F32)))
    thr = _pattern_to_f32(tau)
    thr_b = bcast(thr)
    cnt_gt = count(lambda x, off: jnp.where(x > thr_b, 1.0, 0.0))
    cnt_ge = count(lambda x, off: jnp.where(x >= thr_b, 1.0, 0.0))
    need = k_sel - cnt_gt
    tie = jnp.where(cnt_ge > k_sel, jnp.where(thr > NEG_INF, 1, 0), 0)

    p_scr[...] = jnp.full((rows, LANES), INT_MAX, I32)

    @pl.when(jnp.max(tie) > 0)
    def _():
        def index_bit(it, p):
            cand = p | jnp.left_shift(jnp.int32(1), idx_bits - 1 - it)
            cb = bcast(cand)
            cnt = count(lambda x, off: jnp.where(x == thr_b,
                                                 jnp.where(off + lane < cb, 1.0, 0.0), 0.0))
            return jnp.where(cnt < need, cand, p)
        p = lax.fori_loop(0, idx_bits, index_bit, jnp.zeros((rows, 1), I32))
        p_scr[...] = bcast(p)

    p_b = p_scr[...]
    lim_b = bcast(lim)

    def fin(c, carry):
        off = pl.multiple_of(c * CW, CW)
        x = sc_ref[:, pl.ds(off, CW)]
        for t in range(ntile):
            xt = x[:, LANES * t:LANES * (t + 1)]
            idx = off + LANES * t + lane
            sel = jnp.where(xt > thr_b, 0.0,
                            jnp.where(xt == thr_b, jnp.where(idx <= p_b, 0.0, NEG_INF), NEG_INF))
            madd_ref[:, pl.ds(pl.multiple_of(off + LANES * t, LANES), LANES)] = (
                jnp.where(idx <= lim_b, sel, NEG_INF))
        return carry

    lax.fori_loop(0, nch, fin, 0)


def _bias_table(rb_ref, h, d):
    bias = jnp.full(d.shape, rb_ref[0, h], F32)
    for j in range(1, NUM_BUCKETS):
        if BUCKET_START[j] is not None:
            bias = jnp.where(d >= BUCKET_START[j], rb_ref[j, h], bias)
    return bias


def _dsa_prompt_kernel(k_sel, idx_bits, rb_ref, q_ref, qi_ref, kiw_ref, kT_ref, vb_ref, kiT_ref,
                       o_ref, sc_scr, madd_scr, p_scr, btab_scr):
    i = pl.program_id(1)
    nch = ((i + 1) * QB + CW - 1) // CW

    @pl.when((pl.program_id(0) == 0) & (i == 0))
    def _():
        d = (lax.broadcasted_iota(I32, (QB, 2 * QB), 0)
             - lax.broadcasted_iota(I32, (QB, 2 * QB), 1) + QB)
        for h in range(DSA_H):
            btab_scr[h] = _bias_table(rb_ref, h, d) - rb_ref[NUM_BUCKETS - 1, h]

    tq = i * QB + lax.broadcasted_iota(I32, (QB, 1), 0)
    qi = qi_ref[0]
    wi = kiw_ref[0][:, IDX_D:IDX_D + IDX_H]
    qi_all = jnp.concatenate([qi[:, IDX_D * h:IDX_D * (h + 1)] for h in range(IDX_H)], axis=0)
    wi_h = [wi[:, h:h + 1] for h in range(IDX_H)]

    def score_chunk(c, carry):
        off = pl.multiple_of(c * CW, CW)
        s = _dot(qi_all, kiT_ref[0, :, pl.ds(off, CW)])
        acc = jnp.zeros((QB, CW), F32)
        for h in range(IDX_H):
            acc = acc + jnp.maximum(s[QB * h:QB * (h + 1)], 0.0) * wi_h[h]
        spos = off + lax.broadcasted_iota(I32, (1, CW), 1)
        sc_scr[:, pl.ds(off, CW)] = jnp.where(spos <= tq, acc * IDX_SCALE + 0.0, NEG_INF)
        return carry

    lax.fori_loop(0, nch, score_chunk, 0)

    _topk_mask(sc_scr, madd_scr, p_scr, QB, nch, k_sel, tq, idx_bits)

    q = q_ref[0] * (DSA_HEAD ** -0.5)
    tiles = CW // QB
    hsl = [slice(DSA_HEAD * h, DSA_HEAD * (h + 1)) for h in range(DSA_H)]
    qh = [q[:, sl] for sl in hsl]

    def attend(heads, near, c, carry):
        n = len(heads)
        off = pl.multiple_of(c * CW, CW)
        madd = madd_scr[:, pl.ds(off, CW)]
        s = [_dot(qh[h], kT_ref[0, hsl[h], pl.ds(off, CW)]) + madd for h in heads]
        if near:
            for j, h in enumerate(heads):
                parts = []
                for t in range(tiles):
                    delta = i - (c * tiles + t)
                    parts.append(jnp.where(delta == 0, btab_scr[h, :, QB:2 * QB],
                                           jnp.where(delta == 1, btab_scr[h, :, 0:QB], 0.0)))
                s[j] = s[j] + jnp.concatenate(parts, axis=1)
        m_new = [jnp.maximum(carry[j][0], jnp.max(s[j], axis=1, keepdims=True)) for j in range(n)]
        p = [jnp.exp(s[j] - m_new[j]) for j in range(n)]
        pv = [_dot(p[j].astype(BF16), vb_ref[0, pl.ds(off, CW), hsl[h]])
              for j, h in enumerate(heads)]
        out = []
        for j in range(n):
            m, l, acc = carry[j]
            alpha = jnp.exp(m - m_new[j])
            out.append((m_new[j], l * alpha + jnp.sum(p[j], axis=1, keepdims=True),
                        acc * alpha + pv[j]))
        return tuple(out)

    n_far = jnp.maximum(i - 1, 0) // tiles
    for g0 in range(0, DSA_H, ATTN_GROUP):
        heads = list(range(g0, g0 + ATTN_GROUP))
        init = tuple((jnp.full((QB, 1), NEG_INF, F32), jnp.zeros((QB, 1), F32),
                      jnp.zeros((QB, DSA_HEAD), F32)) for _ in heads)
        carry = lax.fori_loop(0, n_far, functools.partial(attend, heads, False), init)
        carry = lax.fori_loop(n_far, nch, functools.partial(attend, heads, True), carry)
        for j, h in enumerate(heads):
            _, l, acc = carry[j]
            o_ref[0, :, hsl[h]] = (acc / l).astype(BF16)


def _dsa_prompt(rel_bias, q, qi, kiw, kT, vb, kiT):
    b, t, _ = q.shape
    k_sel = min(TOPK_MAX, t // 4)
    idx_bits = max(1, int(math.ceil(math.log2(t))))
    blk = lambda w: pl.BlockSpec((1, QB, w), lambda bi, i: (bi, i, 0))
    whole = lambda shape: pl.BlockSpec(shape, lambda bi, i: (bi, 0, 0))
    return pl.pallas_call(
        functools.partial(_dsa_prompt_kernel, k_sel, idx_bits),
        grid=(b, t // QB),
        in_specs=[pl.BlockSpec(memory_space=pltpu.SMEM),
                  blk(DSA_W), blk(IDX_H * IDX_D), blk(LANES),
                  whole((1, DSA_W, t)), whole((1, t, DSA_W)), whole((1, IDX_D, t))],
        out_specs=blk(DSA_W),
        out_shape=jax.ShapeDtypeStruct((b, t, DSA_W), BF16),
        scratch_shapes=[pltpu.VMEM((QB, t), F32), pltpu.VMEM((QB, t), F32),
                        pltpu.VMEM((QB, LANES), I32),
                        pltpu.VMEM((DSA_H, QB, 2 * QB), F32)],
        compiler_params=_cparams("arbitrary", "arbitrary"),
        name="dsa_prompt",
    )(rel_bias, q, qi, kiw, kT, vb, kiT)


SROWS = 8
SCHUNK = PAGES_PER_STEP * PAGE_SIZE


SCORE_PAGES = 2 * PAGES_PER_STEP
SCORE_CHUNK = SCORE_PAGES * PAGE_SIZE


def _dsa_s_scores_kernel(n_tok, n_past_chunks, pt_ref, qi_ref, wcol_ref, kinew_ref, *rest):
    pages, out_ref = rest[:SCORE_PAGES], rest[SCORE_PAGES]
    c = pl.program_id(1)
    qi = qi_ref[0]
    wcol = wcol_ref[0]

    def head_sum(s):
        s = jnp.maximum(s, 0.0) * wcol
        return jnp.sum(s.reshape(n_tok, IDX_H, s.shape[-1]), axis=1) * IDX_SCALE + 0.0

    @pl.when(c < n_past_chunks)
    def _():
        kit = jnp.concatenate([pg[0] for pg in pages], axis=1).astype(BF16)
        out_ref[0, 0:n_tok, :] = head_sum(_dot(qi, kit))
        out_ref[0, n_tok:SROWS, :] = jnp.full((SROWS - n_tok, SCORE_CHUNK), NEG_INF, F32)

    @pl.when(c == n_past_chunks)
    def _():
        s = head_sum(_dot_nt(qi, kinew_ref[0]))
        tr = lax.broadcasted_iota(I32, (n_tok, SROWS), 0)
        tc = lax.broadcasted_iota(I32, (n_tok, SROWS), 1)
        out_ref[...] = jnp.full((1, SROWS, SCORE_CHUNK), NEG_INF, F32)
        out_ref[0, 0:n_tok, 0:SROWS] = jnp.where(tc <= tr, s, NEG_INF)


def _dsa_s_scores(page_table, cache_ki, qi_rows, wcol, ki_new, n_tok):
    b, n_pages = page_table.shape
    npc = n_pages // SCORE_PAGES
    page_specs = [
        pl.BlockSpec((1, IDX_D, PAGE_SIZE),
                     lambda bi, c, pt, j=j: (pt[bi, jnp.minimum(c, npc - 1) * SCORE_PAGES + j], 0, 0))
        for j in range(SCORE_PAGES)]
    rows = qi_rows.shape[1]
    grid_spec = pltpu.PrefetchScalarGridSpec(
        num_scalar_prefetch=1,
        grid=(b, npc + 1),
        in_specs=[pl.BlockSpec((1, rows, IDX_D), lambda bi, c, pt: (bi, 0, 0)),
                  pl.BlockSpec((1, rows, 1), lambda bi, c, pt: (bi, 0, 0)),
                  pl.BlockSpec((1, SROWS, IDX_D), lambda bi, c, pt: (bi, 0, 0))] + page_specs,
        out_specs=pl.BlockSpec((1, SROWS, SCORE_CHUNK), lambda bi, c, pt: (bi, 0, c)),
    )
    return pl.pallas_call(
        functools.partial(_dsa_s_scores_kernel, n_tok, npc),
        grid_spec=grid_spec,
        out_shape=jax.ShapeDtypeStruct((b, SROWS, (npc + 1) * SCORE_CHUNK), F32),
        compiler_params=_cparams("arbitrary", "arbitrary"),
        name="dsa_s_scores",
    )(page_table, qi_rows, wcol, ki_new, *([cache_ki] * SCORE_PAGES))


def _topk_rows_kernel(k_sel, idx_bits, sc_ref, lim_ref, madd_ref, p_scr):
    rows, width = sc_ref.shape
    _topk_mask(sc_ref, madd_ref, p_scr, rows, width // CW, k_sel, lim_ref[...], idx_bits)


def _topk_rows(scores, lim, k_sel, rows_per_step):
    n, width = scores.shape
    idx_bits = max(1, int(math.ceil(math.log2(width))))
    return pl.pallas_call(
        functools.partial(_topk_rows_kernel, k_sel, idx_bits),
        grid=(n // rows_per_step,),
        in_specs=[pl.BlockSpec((rows_per_step, width), lambda i: (i, 0)),
                  pl.BlockSpec((rows_per_step, 1), lambda i: (i, 0))],
        out_specs=pl.BlockSpec((rows_per_step, width), lambda i: (i, 0)),
        out_shape=jax.ShapeDtypeStruct((n, width), F32),
        scratch_shapes=[pltpu.VMEM((rows_per_step, LANES), I32)],
        compiler_params=_cparams("arbitrary"),
        name="topk_rows",
    )(scores, lim)


def _dsa_s_attn_kernel(n_past_chunks, past, pt_ref, rb_ref, q_ref, madd_ref, knew_ref, vnew_ref,
                       *rest):
    kp = rest[:PAGES_PER_STEP]
    vp = rest[PAGES_PER_STEP:2 * PAGES_PER_STEP]
    o_ref, m_scr, l_scr, acc_scr = rest[2 * PAGES_PER_STEP:]
    c = pl.program_id(1)
    rows = DSA_H * SROWS

    @pl.when(c == 0)
    def _():
        m_scr[...] = jnp.full(m_scr.shape, NEG_INF, F32)
        l_scr[...] = jnp.zeros(l_scr.shape, F32)
        acc_scr[...] = jnp.zeros(acc_scr.shape, F32)

    q = q_ref[0] * (DSA_HEAD ** -0.5)
    head_of_row = lax.broadcasted_iota(I32, (rows, DSA_W), 0) // SROWS
    head_of_col = lax.broadcasted_iota(I32, (rows, DSA_W), 1) // DSA_HEAD
    qblk = jnp.where(head_of_row == head_of_col, jnp.concatenate([q] * DSA_H, axis=0),
                     jnp.zeros((), BF16))

    def near_bias(width, key0):
        tok = lax.broadcasted_iota(I32, (SROWS, width), 0)
        d = past + tok - (key0 + lax.broadcasted_iota(I32, (SROWS, width), 1))
        return jnp.concatenate(
            [_bias_table(rb_ref, h, d) - rb_ref[NUM_BUCKETS - 1, h] for h in range(DSA_H)], axis=0)

    def update(s, pv_fn):
        m = m_scr[...]
        m_new = jnp.maximum(m, jnp.max(s, axis=1, keepdims=True))
        alpha = jnp.exp(m - m_new)
        p = jnp.exp(s - m_new)
        l_scr[...] = l_scr[...] * alpha + jnp.sum(p, axis=1, keepdims=True)
        acc_scr[...] = acc_scr[...] * alpha + pv_fn(p.astype(BF16))
        m_scr[...] = m_new

    def chunk_t(pages):
        return jnp.concatenate([pg[0].reshape(DSA_W, PAGE_SIZE) for pg in pages],
                               axis=1).astype(BF16)

    @pl.when(c < n_past_chunks)
    def _():
        s = _dot(qblk, chunk_t(kp)) + jnp.concatenate([madd_ref[0]] * DSA_H, axis=0)
        is_last = c == n_past_chunks - 1
        tail = SCHUNK - PAGE_SIZE
        nb_ = jnp.where(is_last, near_bias(PAGE_SIZE, past - PAGE_SIZE), 0.0)
        s = jnp.concatenate([s[:, 0:tail], s[:, tail:SCHUNK] + nb_], axis=1)
        vt = chunk_t(vp)
        update(s, lambda pb: _dot_nt(pb, vt))

    @pl.when(c == n_past_chunks)
    def _():
        s = (_dot_nt(qblk, knew_ref[0]) + jnp.concatenate([madd_ref[0, :, 0:SROWS]] * DSA_H, axis=0)
             + near_bias(SROWS, past))
        update(s, lambda pb: _dot(pb, vnew_ref[0]))
        for h in range(DSA_H):
            r = slice(SROWS * h, SROWS * (h + 1))
            cs = slice(DSA_HEAD * h, DSA_HEAD * (h + 1))
            o_ref[0, :, cs] = (acc_scr[r, cs] / l_scr[r, :]).astype(o_ref.dtype)


def _dsa_s_attn(page_table, rel_bias, cache_k, cache_v, q, madd, k_new, v_new):
    b, n_pages = page_table.shape
    npc = n_pages // PAGES_PER_STEP
    rows = DSA_H * SROWS
    assert BUCKET_START[NUM_BUCKETS - 1] <= PAGE_SIZE

    def page_spec(j):
        return pl.BlockSpec(
            (1, DSA_H, DSA_HEAD, PAGE_SIZE),
            lambda bi, c, pt, j=j: (pt[bi, jnp.minimum(c, npc - 1) * PAGES_PER_STEP + j], 0, 0, 0))

    page_specs = [page_spec(j) for j in range(PAGES_PER_STEP)]
    grid_spec = pltpu.PrefetchScalarGridSpec(
        num_scalar_prefetch=1,
        grid=(b, npc + 1),
        in_specs=[pl.BlockSpec(memory_space=pltpu.SMEM),
                  pl.BlockSpec((1, SROWS, DSA_W), lambda bi, c, pt: (bi, 0, 0)),
                  pl.BlockSpec((1, SROWS, SCHUNK), lambda bi, c, pt: (bi, 0, c)),
                  pl.BlockSpec((1, SROWS, DSA_W), lambda bi, c, pt: (bi, 0, 0)),
                  pl.BlockSpec((1, SROWS, DSA_W), lambda bi, c, pt: (bi, 0, 0))]
                 + page_specs + page_specs,
        out_specs=pl.BlockSpec((1, SROWS, DSA_W), lambda bi, c, pt: (bi, 0, 0)),
        scratch_shapes=[pltpu.VMEM((rows, 1), F32), pltpu.VMEM((rows, 1), F32),
                        pltpu.VMEM((rows, DSA_W), F32)],
    )
    return pl.pallas_call(
        functools.partial(_dsa_s_attn_kernel, npc, n_pages * PAGE_SIZE),
        grid_spec=grid_spec,
        out_shape=jax.ShapeDtypeStruct((b, SROWS, DSA_W), BF16),
        compiler_params=_cparams("arbitrary", "arbitrary"),
        name="dsa_s_attn",
    )(page_table, rel_bias, q, madd, k_new, v_new,
      *([cache_k] * PAGES_PER_STEP), *([cache_v] * PAGES_PER_STEP))


def _mem_kv_kernel(x_ref, g_ref, w_ref, k_ref, v_ref):
    h = _rms(x_ref[...], g_ref[...]).astype(BF16)
    width = k_ref.shape[-1]
    k_ref[...] = _dot(h, w_ref[:, 0:width])
    v_ref[...] = _dot(h, w_ref[:, width:2 * width])


def _mem_kv(mem, g, wkv, tm):
    n, d = mem.shape
    width = wkv.shape[1] // 2
    return pl.pallas_call(
        _mem_kv_kernel,
        grid=(n // tm,),
        in_specs=[pl.BlockSpec((tm, d), lambda i: (i, 0)),
                  pl.BlockSpec((1, d), lambda i: (0, 0)),
                  pl.BlockSpec(wkv.shape, lambda i: (0, 0))],
        out_specs=[pl.BlockSpec((tm, width), lambda i: (i, 0))] * 2,
        out_shape=[jax.ShapeDtypeStruct((n, width), F32)] * 2,
        compiler_params=_cparams("arbitrary"),
        name="mem_kv",
    )(mem, g, wkv)


def _split_bf16(x):
    hi = x.astype(BF16)
    return hi, (x - hi.astype(F32)).astype(BF16)


def _mix_kernel(x_ref, yrw_ref, ydsa_ref, mk_ref, mv_ref, woa_ref, wob_ref, gx_ref, wq_ref, wo_ref,
                gf_ref, rwh_ref, rwl_ref, rb_ref, x2_ref, h3_ref, comb_ref):
    x1 = x_ref[0] + _dot(yrw_ref[0], woa_ref[...]) + _dot(ydsa_ref[0], wob_ref[...])
    h2 = _rms(x1, gx_ref[...]).astype(BF16)
    qx = _dot(h2, wq_ref[...])
    mk = mk_ref[0].astype(BF16)
    mv = mv_ref[0].astype(BF16)
    heads = []
    for h in range(XA_H):
        sl = slice(XA_HEAD * h, XA_HEAD * (h + 1))
        lg = _dot_nt(qx[:, sl].astype(BF16), mk[:, sl]) * (XA_HEAD ** -0.5)
        p = jnp.exp(lg - jnp.max(lg, axis=1, keepdims=True))
        l = jnp.sum(p, axis=1, keepdims=True)
        heads.append(_dot(p.astype(BF16), mv[:, sl]) / l)
    o = jnp.concatenate(heads, axis=1).astype(BF16)
    x2 = x1 + _dot(o, wo_ref[...])
    x2_ref[0] = x2
    h3 = _rms(x2, gf_ref[...])
    h3_ref[0] = h3.astype(BF16)

    hh, hl = _split_bf16(h3)
    logits = (_dot(hh, rwh_ref[...]) + _dot(hh, rwl_ref[...]) + _dot(hl, rwh_ref[...])
              + rb_ref[...])
    lane = lax.broadcasted_iota(I32, logits.shape, 1)
    work = logits
    vals, idxs = [], []
    for _ in range(TOP_K):
        mx = jnp.max(work, axis=1, keepdims=True)
        ix = jnp.min(jnp.where(work == mx, lane, N_EXPERTS), axis=1, keepdims=True)
        vals.append(mx)
        idxs.append(ix)
        work = jnp.where(lane == ix, -jnp.inf, work)
    es = [jnp.exp(v - vals[0]) for v in vals]
    den = es[0] + es[1] + es[2] + es[3]
    comb = jnp.zeros(logits.shape, F32)
    for e, ix in zip(es, idxs):
        comb = comb + jnp.where(lane == ix, e / den, 0.0)
    comb_ref[0] = comb


def _mix(x, yrw, ydsa, mk, mv, W, tm):
    b, t, d = x.shape
    mem_len, xa_w = mk.shape[1], mk.shape[2]
    full = lambda a: pl.BlockSpec(a.shape, lambda i, j: (0,) * a.ndim)
    rspec = lambda w: pl.BlockSpec((1, tm, w), lambda i, j: (i, j, 0))
    mspec = pl.BlockSpec((1, mem_len, xa_w), lambda i, j: (i, 0, 0))
    params = [W['woa'], W['wob'], W['gx'], W['wq'], W['wo'], W['gf'], W['rwh'], W['rwl'], W['rb']]
    return pl.pallas_call(
        _mix_kernel,
        grid=(b, t // tm),
        in_specs=[rspec(d), rspec(RW_W), rspec(DSA_W), mspec, mspec] + [full(a) for a in params],
        out_specs=[rspec(d), rspec(d), rspec(N_EXPERTS)],
        out_shape=[jax.ShapeDtypeStruct((b, t, d), F32), jax.ShapeDtypeStruct((b, t, d), BF16),
                   jax.ShapeDtypeStruct((b, t, N_EXPERTS), F32)],
        compiler_params=_cparams("arbitrary", "arbitrary"),
        name="mix",
    )(x, yrw, ydsa, mk, mv, *params)


MOE_EXPERTS_PER_STEP = 2


def _moe_kernel(h_ref, comb_ref, x_ref, w1_ref, b1_ref, w2_ref, b2_ref, gfin_ref, o_ref, acc_scr):
    e = pl.program_id(1)
    d_ff = w2_ref.shape[1]

    @pl.when(e == 0)
    def _():
        acc_scr[...] = jnp.zeros(acc_scr.shape, F32)

    h = h_ref[...]
    comb = comb_ref[...]
    lane = lax.broadcasted_iota(I32, comb.shape, 1)
    group = range(w1_ref.shape[0])
    gate = [jnp.minimum(_dot(h, w1_ref[g, :, 0:d_ff]) + b1_ref[g, :, 0:d_ff], SWIGLU_LIMIT)
            for g in group]
    up = [jnp.clip(_dot(h, w1_ref[g, :, d_ff:2 * d_ff]) + b1_ref[g, :, d_ff:2 * d_ff],
                   -SWIGLU_LIMIT, SWIGLU_LIMIT) for g in group]
    act = [((up[g] + 1.0) * (gate[g] * _sigmoid(SWIGLU_ALPHA * gate[g]))).astype(BF16)
           for g in group]
    out = [_dot(act[g], w2_ref[g]) + b2_ref[g] for g in group]
    total = None
    for g in group:
        wcol = jnp.sum(jnp.where(lane == e * len(group) + g, comb, 0.0), axis=1, keepdims=True)
        total = wcol * out[g] if total is None else total + wcol * out[g]
    acc_scr[...] += total

    @pl.when(e == pl.num_programs(1) - 1)
    def _():
        o_ref[...] = _rms(x_ref[...] + acc_scr[...], gfin_ref[...])


def _moe(h3, comb, x2, w1, b1, w2, b2, gfin, tm):
    n, d = h3.shape
    n_exp, _, two_ff = w1.shape
    d_ff = two_ff // 2
    eg = MOE_EXPERTS_PER_STEP
    return pl.pallas_call(
        _moe_kernel,
        grid=(n // tm, n_exp // eg),
        in_specs=[pl.BlockSpec((tm, d), lambda i, e: (i, 0)),
                  pl.BlockSpec((tm, n_exp), lambda i, e: (i, 0)),
                  pl.BlockSpec((tm, d), lambda i, e: (i, 0)),
                  pl.BlockSpec((eg, d, two_ff), lambda i, e: (e, 0, 0)),
                  pl.BlockSpec((eg, 1, two_ff), lambda i, e: (e, 0, 0)),
                  pl.BlockSpec((eg, d_ff, d), lambda i, e: (e, 0, 0)),
                  pl.BlockSpec((eg, 1, d), lambda i, e: (e, 0, 0)),
                  pl.BlockSpec((1, d), lambda i, e: (0, 0))],
        out_specs=pl.BlockSpec((tm, d), lambda i, e: (i, 0)),
        out_shape=jax.ShapeDtypeStruct((n, d), F32),
        scratch_shapes=[pltpu.VMEM((tm, d), F32)],
        compiler_params=_cparams("arbitrary", "arbitrary"),
        name="moe",
    )(h3, comb, x2, w1, b1, w2, b2, gfin)


def _pick_tile(n, pref):
    t = min(pref, n)
    while n % t:
        t //= 2
    return t


def kernel(x_prompt, mem_prompt, x_sample, cache_k, cache_v, cache_idx_k, cache_mem_k, cache_mem_v, state_rwkv, state_shift, page_table, rel_bias, norm_final, norm_mix, w_in, mu_shift, rw_w0, rw_w2, rw_a0, rw_a2, rw_g2, rw_k_k, rw_k_a, rw_r_k, rw_ln_w, rw_ln_b, w_out, norm_xattn, norm_mem, xa_wq, xa_wk, xa_wv, xa_wo, norm_ffn, router_w, router_b, moe_w1, moe_b1, moe_w2, moe_b2):
    depth = w_in.shape[0]
    assert depth == 1, "single-layer step"
    l = 0
    bp, tp, d = x_prompt.shape
    bs, ts, _ = x_sample.shape
    n_pool = cache_k.shape[1]
    n_pages = page_table.shape[1]
    past = n_pages * PAGE_SIZE
    mem_len = mem_prompt.shape[1]
    xa_w = XA_H * XA_HEAD
    rw_cols = state_shift.shape[-1]
    assert tp % CW == 0 and n_pages % SCORE_PAGES == 0 and ts <= SROWS

    row = lambda a: a.reshape(1, -1)
    bf = lambda a: a.astype(BF16)

    w_pad = bf(jnp.pad(w_in[l], ((0, 0), (0, C_END - w_in.shape[-1]))))
    lora = rw_w2.shape[1]
    P = {
        'mu': row(mu_shift[l]), 'w0': row(rw_w0[l]), 'a0': row(rw_a0[l]),
        'w2': bf(jnp.pad(rw_w2[l], ((0, LANES - lora), (0, 0)))),
        'a2': bf(jnp.pad(rw_a2[l], ((lora, LANES - lora - rw_a2.shape[1]), (0, 0)))),
        'g2': bf(rw_g2[l]), 'kk': row(rw_k_k[l]), 'ka': row(rw_k_a[l]),
        'rk': rw_r_k[l].reshape(RW_H, RW_HEAD),
        'lnw': rw_ln_w[l].reshape(RW_H, RW_HEAD), 'lnb': rw_ln_b[l].reshape(RW_H, RW_HEAD),
    }
    rwh = bf(router_w[l])
    W = {
        'woa': bf(w_out[l][:RW_W]), 'wob': bf(w_out[l][RW_W:]), 'gx': row(norm_xattn[l]),
        'wq': bf(xa_wq[l]), 'wo': bf(xa_wo[l]), 'gf': row(norm_ffn[l]),
        'rwh': rwh, 'rwl': bf(router_w[l] - rwh.astype(F32)), 'rb': row(router_b[l]),
    }
    w1 = bf(moe_w1[l])
    w2 = bf(moe_w2[l])
    b1 = moe_b1[l][:, None, :]
    b2 = moe_b2[l][:, None, :]
    gfin = row(norm_final)
    gmix = row(norm_mix[l])

    prw, q, k, v, qi, kiw, kT, vb, kiT = _in_proj(x_prompt, gmix, w_pad, _pick_tile(tp, 256))
    y_rw, p_s1, p_sh1 = _rwkv_chunk(prw, jnp.zeros((bp, 1, rw_cols), F32),
                                    jnp.zeros((bp, RW_H, RW_HEAD, RW_HEAD), F32), P,
                                    nb=bp, tc=_pick_tile(tp, 256), passes=1)
    y_dsa = _dsa_prompt(rel_bias, q, qi, kiw, kT, vb, kiT)
    wkv = bf(jnp.concatenate([xa_wk[l], xa_wv[l]], axis=1))
    mk_p, mv_p = _mem_kv(mem_prompt.reshape(bp * mem_len, d), row(norm_mem[l]), wkv,
                         _pick_tile(bp * mem_len, 256))
    mk_p = mk_p.reshape(bp, mem_len, xa_w)
    mv_p = mv_p.reshape(bp, mem_len, xa_w)
    x2, h3, comb = _mix(x_prompt, y_rw, y_dsa, mk_p, mv_p, W, _pick_tile(tp, 256))
    n_p = bp * tp
    y_prompt = _moe(h3.reshape(n_p, d), comb.reshape(n_p, N_EXPERTS), x2.reshape(n_p, d),
                    w1, b1, w2, b2, gfin, _pick_tile(n_p, 512)).reshape(bp, tp, d)

    xs = jnp.pad(x_sample, ((0, 0), (0, SROWS - ts), (0, 0)))
    n_s = bs * SROWS
    sprw, sq, sk, sv, sqi, skiw, _, svb, _ = _in_proj(xs.reshape(1, n_s, d), gmix, w_pad,
                                                      _pick_tile(n_s, 256))
    seq = lambda a: a.reshape(bs, SROWS, a.shape[-1])
    sprw, sq, sk, sv, sqi, skiw, svb = map(seq, (sprw, sq, sk, sv, sqi, skiw, svb))
    sy_rw, s_s1, s_sh1 = _rwkv(sprw, state_shift[l][:, None, :], state_rwkv[l], P,
                               nb=_pick_tile(bs, 4), tc=SROWS, n_valid=ts)

    qi_rows = sqi[:, :ts].reshape(bs, ts * IDX_H, IDX_D)
    wcol = skiw[:, :ts, IDX_D:IDX_D + IDX_H].reshape(bs, ts * IDX_H, 1)
    ki_new = bf(skiw[:, :, :IDX_D])
    scores = _dsa_s_scores(page_table, jnp.transpose(cache_idx_k[l], (0, 2, 1)), qi_rows, wcol,
                           ki_new, ts)
    width = scores.shape[-1]
    tok = jnp.arange(SROWS, dtype=I32)
    lim = jnp.tile(jnp.where(tok < ts, past + tok, -1), bs).reshape(n_s, 1)
    k_sel = min(TOPK_MAX, (past + ts) // 4)
    madd = _topk_rows(scores.reshape(n_s, width), lim, k_sel, _pick_tile(n_s, 64))
    madd = madd.reshape(bs, SROWS, width)
    sy_dsa = _dsa_s_attn(page_table, rel_bias,
                         jnp.transpose(cache_k[l], (0, 2, 3, 1)),
                         jnp.transpose(cache_v[l], (0, 2, 3, 1)),
                         sq, madd, bf(sk), svb)
    sx2, sh3, scomb = _mix(xs, sy_rw, sy_dsa, cache_mem_k[l].reshape(bs, mem_len, xa_w),
                           cache_mem_v[l].reshape(bs, mem_len, xa_w), W, SROWS)
    y_s = _moe(sh3.reshape(n_s, d), scomb.reshape(n_s, N_EXPERTS), sx2.reshape(n_s, d),
               w1, b1, w2, b2, gfin, _pick_tile(n_s, 256)).reshape(bs, SROWS, d)

    heads = lambda a, b_, t_: a.reshape(1, b_, t_, DSA_H, DSA_HEAD)
    return (y_prompt, y_s[:, :ts],
            p_s1[None], p_sh1.reshape(1, bp, rw_cols),
            heads(k, bp, tp), heads(v, bp, tp), kiw[:, :, :IDX_D][None],
            mk_p.reshape(1, bp, mem_len, XA_H, XA_HEAD), mv_p.reshape(1, bp, mem_len, XA_H, XA_HEAD),
            s_s1[None], s_sh1.reshape(1, bs, rw_cols),
            heads(sk[:, :ts], bs, ts), heads(sv[:, :ts], bs, ts), skiw[:, :ts, :IDX_D][None])
```

```python
import functools
import math

import numpy as np
import jax
import jax.numpy as jnp
from jax import lax
from jax.experimental import pallas as pl
from jax.experimental.pallas import tpu as pltpu

F32 = jnp.float32
BF16 = jnp.bfloat16
I32 = jnp.int32

LANES = 128
VMEM_LIMIT = 56 * 1024 * 1024

NORM_EPS = 1e-5
RW_EPS = 64e-5
NEG_INF = -1e30
RW_H = 8
RW_HEAD = 64
RW_W = RW_H * RW_HEAD
DSA_H = 8
DSA_HEAD = 64
DSA_W = DSA_H * DSA_HEAD
IDX_H = 8
IDX_D = 64
TOPK_MAX = 256
XA_H = 4
XA_HEAD = 128
N_EXPERTS = 32
TOP_K = 4
SWIGLU_LIMIT = 7.0
SWIGLU_ALPHA = 1.702
NUM_BUCKETS = 32
MAX_DISTANCE = 128
PAGE_SIZE = 128
IDX_SCALE = IDX_H ** -0.5 * IDX_D ** -0.5

QB = 128
CW = 512
ATTN_GROUP = 8
PAGES_PER_STEP = 16

INT_MIN = -2 ** 31
INT_MAX = 2 ** 31 - 1


def _bucket_starts():
    max_exact = NUM_BUCKETS // 2
    d = np.arange(0, 4 * MAX_DISTANCE)
    large = max_exact + (np.log(np.maximum(d, 1).astype(np.float32) / np.float32(max_exact))
                         / np.float32(math.log(MAX_DISTANCE / max_exact))
                         * np.float32(NUM_BUCKETS - max_exact)).astype(np.int32)
    b = np.where(d < max_exact, d, np.minimum(large, NUM_BUCKETS - 1))
    starts = []
    for j in range(NUM_BUCKETS):
        hit = np.nonzero(b == j)[0]
        starts.append(int(hit[0]) if hit.size else None)
    return starts


BUCKET_START = _bucket_starts()


def _cparams(*sem):
    return pltpu.CompilerParams(dimension_semantics=sem, vmem_limit_bytes=VMEM_LIMIT)


def _rms(x, g):
    return x * lax.rsqrt(jnp.mean(x * x, axis=-1, keepdims=True) + NORM_EPS) * g


def _dot(a, b):
    return jnp.dot(a, b, preferred_element_type=F32)


def _dot_nt(a, b):
    return lax.dot_general(a, b, (((1,), (1,)), ((), ())), preferred_element_type=F32)


def _softplus(x):
    return jnp.maximum(x, 0.0) + jnp.log1p(jnp.exp(-jnp.abs(x)))


def _sigmoid(x):
    return 1.0 / (1.0 + jnp.exp(-x))


C_RW = 0
C_Q = 1792
C_K = C_Q + DSA_W
C_V = C_K + DSA_W
C_QI = C_V + DSA_W
C_KIW = C_QI + IDX_H * IDX_D
C_END = C_KIW + LANES


def _in_proj_kernel(x_ref, g_ref, w_ref, prw_ref, q_ref, k_ref, v_ref, qi_ref, kiw_ref,
                    kT_ref, vb_ref, kiT_ref):
    h = _rms(x_ref[0], g_ref[...]).astype(BF16)

    def mm(lo, hi):
        return _dot(h, w_ref[:, lo:hi])

    prw_ref[0] = mm(C_RW, C_Q)
    q_ref[0] = mm(C_Q, C_K).astype(BF16)
    k = mm(C_K, C_V)
    k_ref[0] = k
    kT_ref[0] = k.T.astype(BF16)
    v = mm(C_V, C_QI)
    v_ref[0] = v
    vb_ref[0] = v.astype(BF16)
    qi_ref[0] = mm(C_QI, C_KIW).astype(BF16)
    kiw = mm(C_KIW, C_END)
    kiw_ref[0] = kiw
    kiT_ref[0] = kiw.T[0:IDX_D, :].astype(BF16)


def _in_proj(x, g, w_pad, tm):
    b, t, d = x.shape
    grid = (b, t // tm)
    row = lambda w, dt: jax.ShapeDtypeStruct((b, t, w), dt)
    rspec = lambda w: pl.BlockSpec((1, tm, w), lambda i, j: (i, j, 0))
    tspec = lambda w: pl.BlockSpec((1, w, tm), lambda i, j: (i, 0, j))
    return pl.pallas_call(
        _in_proj_kernel,
        grid=grid,
        in_specs=[rspec(d),
                  pl.BlockSpec((1, d), lambda i, j: (0, 0)),
                  pl.BlockSpec((d, C_END), lambda i, j: (0, 0))],
        out_specs=[rspec(C_Q), rspec(DSA_W), rspec(DSA_W), rspec(DSA_W), rspec(IDX_H * IDX_D),
                   rspec(LANES), tspec(DSA_W), rspec(DSA_W), tspec(IDX_D)],
        out_shape=[row(C_Q, F32), row(DSA_W, BF16), row(DSA_W, F32), row(DSA_W, F32),
                   row(IDX_H * IDX_D, BF16), row(LANES, F32),
                   jax.ShapeDtypeStruct((b, DSA_W, t), BF16), row(DSA_W, BF16),
                   jax.ShapeDtypeStruct((b, IDX_D, t), BF16)],
        compiler_params=_cparams("arbitrary", "arbitrary"),
        name="in_proj",
    )(x, g, w_pad)


def _rwkv_kernel(n_steps, prw_ref, sh0_ref, s0_ref, mu_ref, w0_ref, w2_ref, a0_ref, a2_ref, g2_ref,
                 kkw_ref, kaw_ref, rk_ref, lnw_ref, lnb_ref,
                 y_ref, s1_ref, sh1_ref,
                 st_scr, carry_scr, r_s, w_s, k_s, v_s, kk_s, b_s, y_s, g_s):
    nb, tc = prw_ref.shape[0], prw_ref.shape[1]
    c = pl.program_id(1)

    @pl.when(c == 0)
    def _():
        st_scr[...] = s0_ref[...]
        carry_scr[...] = sh0_ref[...]

    row = lax.broadcasted_iota(I32, (tc, 1), 0)
    lane = lax.broadcasted_iota(I32, (1, LANES), 1)
    for b in range(nb):
        x = prw_ref[b]
        prev = jnp.where(row == 0, carry_scr[b], pltpu.roll(x, 1, 0))
        carry_scr[b] = x[n_steps - 1:n_steps, :]
        ps = x + (prev - x) * mu_ref[...]
        r = ps[:, 0:RW_W]
        k = ps[:, RW_W:2 * RW_W]
        v = ps[:, 2 * RW_W:3 * RW_W]
        la = ps[:, 3 * RW_W:3 * RW_W + LANES]
        gd = ps[:, 3 * RW_W + LANES:3 * RW_W + 2 * LANES]
        z = jnp.where(lane < 64, jnp.tanh(la), la).astype(BF16)
        w_log = -_softplus(-(w0_ref[...] + _dot(z, w2_ref[...]))) - 0.5
        decay = jnp.exp(-jnp.exp(w_log))
        a = _sigmoid(a0_ref[...] + _dot(z, a2_ref[...]))
        g_s[b] = _dot(_sigmoid(gd).astype(BF16), g2_ref[...])
        kk = k * kkw_ref[...]
        k2 = k * (1.0 + (a - 1.0) * kaw_ref[...])
        for h in range(RW_H):
            sl = slice(RW_HEAD * h, RW_HEAD * (h + 1))
            kkh = kk[:, sl]
            nrm = jnp.sqrt(jnp.sum(kkh * kkh, axis=-1, keepdims=True))
            kkh = kkh / jnp.maximum(nrm, 1e-12)
            r_s[b, h] = r[:, sl]
            w_s[b, h] = decay[:, sl]
            k_s[b, h] = k2[:, sl]
            v_s[b, h] = v[:, sl]
            kk_s[b, h] = kkh
            b_s[b, h] = kkh * a[:, sl]

    eye = (lax.broadcasted_iota(I32, (RW_HEAD, RW_HEAD), 0)
           == lax.broadcasted_iota(I32, (RW_HEAD, RW_HEAD), 1))

    def step(t, carry):
        for b in range(nb):
            for h in range(RW_H):
                s = st_scr[b, h]
                tt = pl.ds(t, 1)
                sa = -jnp.sum(s * kk_s[b, h, tt, :], axis=1, keepdims=True)
                vcol = jnp.sum(jnp.where(eye, v_s[b, h, tt, :], 0.0), axis=1, keepdims=True)
                s = s * w_s[b, h, tt, :] + sa * b_s[b, h, tt, :] + vcol * k_s[b, h, tt, :]
                st_scr[b, h] = s
                ycol = jnp.sum(s * r_s[b, h, tt, :], axis=1, keepdims=True)
                y_s[b, h, tt, :] = jnp.sum(jnp.where(eye, ycol, 0.0), axis=0, keepdims=True)
        return carry

    lax.fori_loop(0, n_steps, step, 0)

    for b in range(nb):
        for h in range(RW_H):
            sl = slice(RW_HEAD * h, RW_HEAD * (h + 1))
            y = y_s[b, h]
            mean = jnp.mean(y, axis=-1, keepdims=True)
            var = jnp.mean(jnp.square(y - mean), axis=-1, keepdims=True)
            yn = (y - mean) * lax.rsqrt(var + RW_EPS) * lnw_ref[h:h + 1, :] + lnb_ref[h:h + 1, :]
            rr, kk2, vv = r_s[b, h], k_s[b, h], v_s[b, h]
            bonus = jnp.sum(rr * kk2 * rk_ref[h:h + 1, :], axis=-1, keepdims=True) * vv
            y_ref[b, :, sl] = ((yn + bonus) * g_s[b][:, sl]).astype(BF16)
    s1_ref[...] = st_scr[...]
    sh1_ref[...] = carry_scr[...]


def _rwkv(prw, shift0, s0, P, nb, tc, n_valid):
    b, t, cols = prw.shape
    n_steps = min(tc, n_valid)
    grid = (b // nb, t // tc)
    full = lambda a: pl.BlockSpec(a.shape, lambda i, j: (0,) * a.ndim)
    params = [P['mu'], P['w0'], P['w2'], P['a0'], P['a2'], P['g2'], P['kk'], P['ka'], P['rk'],
              P['lnw'], P['lnb']]
    hs = lambda: pltpu.VMEM((nb, RW_H, tc, RW_HEAD), F32)
    return pl.pallas_call(
        functools.partial(_rwkv_kernel, n_steps),
        grid=grid,
        in_specs=[pl.BlockSpec((nb, tc, cols), lambda i, j: (i, j, 0)),
                  pl.BlockSpec((nb, 1, cols), lambda i, j: (i, 0, 0)),
                  pl.BlockSpec((nb, RW_H, RW_HEAD, RW_HEAD), lambda i, j: (i, 0, 0, 0))]
                 + [full(a) for a in params],
        out_specs=[pl.BlockSpec((nb, tc, RW_W), lambda i, j: (i, j, 0)),
                   pl.BlockSpec((nb, RW_H, RW_HEAD, RW_HEAD), lambda i, j: (i, 0, 0, 0)),
                   pl.BlockSpec((nb, 1, cols), lambda i, j: (i, 0, 0))],
        out_shape=[jax.ShapeDtypeStruct((b, t, RW_W), BF16),
                   jax.ShapeDtypeStruct((b, RW_H, RW_HEAD, RW_HEAD), F32),
                   jax.ShapeDtypeStruct((b, 1, cols), F32)],
        scratch_shapes=[pltpu.VMEM((nb, RW_H, RW_HEAD, RW_HEAD), F32),
                        pltpu.VMEM((nb, 1, cols), F32),
                        hs(), hs(), hs(), hs(), hs(), hs(), hs(),
                        pltpu.VMEM((nb, tc, RW_W), F32)],
        compiler_params=_cparams("arbitrary", "arbitrary"),
        name="rwkv",
    )(prw, shift0, s0, *params)


RC = 64
RC_GROUP = 16


def _split3(x):
    hi = x.astype(BF16)
    r1 = x - hi.astype(F32)
    mid = r1.astype(BF16)
    lo = (r1 - mid.astype(F32)).astype(BF16)
    return hi, mid, lo


def _mm(a, b, passes):
    if passes == 1:
        return _dot(a.astype(BF16), b.astype(BF16))
    ah, al = _split_bf16(a)
    bh, bl = _split_bf16(b)
    return _dot(ah, bh) + _dot(ah, bl) + _dot(al, bh)


def _mm_nt(a, b, passes):
    if passes == 1:
        return _dot_nt(a.astype(BF16), b.astype(BF16))
    ah, al = _split_bf16(a)
    bh, bl = _split_bf16(b)
    return _dot_nt(ah, bh) + _dot_nt(ah, bl) + _dot_nt(al, bh)


def _exact_left(m_bf16, x):
    hi, mid, lo = _split3(x)
    return _dot(m_bf16, hi) + _dot(m_bf16, mid) + _dot(m_bf16, lo)


def _exact_right(x, m_bf16):
    hi, mid, lo = _split3(x)
    return _dot(hi, m_bf16) + _dot(mid, m_bf16) + _dot(lo, m_bf16)


def _rwkv_chunk_kernel(passes, prw_ref, sh0_ref, s0_ref, mu_ref, w0_ref, w2_ref, a0_ref, a2_ref,
                       g2_ref, kkw_ref, kaw_ref, rk_ref, lnw_ref, lnb_ref, ltri_ref, lones_ref,
                       hones_ref, y_ref, s1_ref, sh1_ref,
                       st_scr, carry_scr, rh_s, kkh_s, bt_s, kt_s, bg_s, kg_s, v_s, gc_s, y_s,
                       bonus_s, g_s):
    nb, tc = prw_ref.shape[0], prw_ref.shape[1]
    nh = nb * RW_H
    c = pl.program_id(1)

    @pl.when(c == 0)
    def _():
        for b in range(nb):
            for h in range(RW_H):
                st_scr[b * RW_H + h] = s0_ref[b, h]
        carry_scr[...] = sh0_ref[...]

    row = lax.broadcasted_iota(I32, (tc, 1), 0)
    lane = lax.broadcasted_iota(I32, (1, LANES), 1)
    hones = hones_ref[...]
    for b in range(nb):
        x = prw_ref[b]
        prev = jnp.where(row == 0, carry_scr[b], pltpu.roll(x, 1, 0))
        carry_scr[b] = x[tc - 1:tc, :]
        ps = x + (prev - x) * mu_ref[...]
        r = ps[:, 0:RW_W]
        k = ps[:, RW_W:2 * RW_W]
        v = ps[:, 2 * RW_W:3 * RW_W]
        la = ps[:, 3 * RW_W:3 * RW_W + LANES]
        gd = ps[:, 3 * RW_W + LANES:3 * RW_W + 2 * LANES]
        z = jnp.where(lane < 64, jnp.tanh(la), la).astype(BF16)
        w_log = -_softplus(-(w0_ref[...] + _dot(z, w2_ref[...]))) - 0.5
        lw = -jnp.exp(w_log)
        a = _sigmoid(a0_ref[...] + _dot(z, a2_ref[...]))
        g_s[b] = _dot(_sigmoid(gd).astype(BF16), g2_ref[...])
        kk = k * kkw_ref[...]
        kk = kk / jnp.maximum(jnp.sqrt(_exact_right(kk * kk, hones)), 1e-12)
        k2 = k * (1.0 + (a - 1.0) * kaw_ref[...])
        bb = kk * a
        bonus_s[b] = _exact_right(r * k2 * rk_ref[...], hones) * v
        gsum = _exact_left(ltri_ref[...], lw)
        gend = _exact_left(lones_ref[...], lw)
        e_neg = jnp.exp(-gsum)
        e_end = jnp.exp(gend - gsum)
        cols = {'rh': r * jnp.exp(gsum), 'kkh': kk * jnp.exp(gsum - lw), 'bt': bb * e_neg,
                'kt': k2 * e_neg, 'bg': bb * e_end, 'kg': k2 * e_end, 'v': v, 'gc': jnp.exp(gend)}
        dst = {'rh': rh_s, 'kkh': kkh_s, 'bt': bt_s, 'kt': kt_s, 'bg': bg_s, 'kg': kg_s, 'v': v_s,
               'gc': gc_s}
        for name, val in cols.items():
            for h in range(RW_H):
                dst[name][b * RW_H + h] = val[:, RW_HEAD * h:RW_HEAD * (h + 1)]

    ri = lax.broadcasted_iota(I32, (RC, RC), 0)
    ci = lax.broadcasted_iota(I32, (RC, RC), 1)
    strict = ri > ci
    incl = ri >= ci
    eye = ri == ci
    eye_f = jnp.where(eye, 1.0, 0.0)
    n_double = int(math.log2(RC)) - 1

    def head_group(heads, sub):
        base = pl.multiple_of(sub * RC, RC)
        rows = pl.ds(base, RC)
        mm = lambda a, b: _mm(a, b, passes)
        each = lambda f: [f(i) for i in range(len(heads))]
        rh = [rh_s[hd, rows, :] for hd in heads]
        kkh = [kkh_s[hd, rows, :] for hd in heads]
        vv = [v_s[hd, rows, :] for hd in heads]
        gram = each(lambda i: _mm_nt(
            jnp.concatenate([kkh[i], rh[i]], axis=0),
            jnp.concatenate([bt_s[heads[i], rows, :], kt_s[heads[i], rows, :]], axis=0), passes))
        mb = each(lambda i: jnp.where(strict, gram[i][0:RC, 0:RC], 0.0))
        mkv = each(lambda i: mm(jnp.where(strict, gram[i][0:RC, RC:2 * RC], 0.0), vv[i]))
        nkv = each(lambda i: mm(jnp.where(incl, gram[i][RC:2 * RC, RC:2 * RC], 0.0), vv[i]))
        nb_ = each(lambda i: jnp.where(incl, gram[i][RC:2 * RC, 0:RC], 0.0))
        inv = each(lambda i: eye_f - mb[i])
        q = mb
        for _ in range(n_double):
            q = each(lambda i: mm(q[i], q[i]))
            inv = each(lambda i: inv[i] + mm(inv[i], q[i]))
        u = each(lambda i: -mm(inv[i], jnp.concatenate([kkh[i], mkv[i]], axis=1)))
        yy = each(lambda i: mm(nb_[i], u[i]))
        zz = each(lambda i: mm(u[i].T, bg_s[heads[i], rows, :]))
        vk = each(lambda i: mm(vv[i].T, kg_s[heads[i], rows, :]))
        s = [st_scr[hd] for hd in heads]
        ys = each(lambda i: _mm_nt(rh[i] + yy[i][:, 0:RC], s[i], passes))
        z1 = each(lambda i: jnp.where(eye, gc_s[heads[i], pl.ds(base, 1), :], 0.0)
                  + zz[i][0:RW_HEAD])
        sn = each(lambda i: mm(s[i], z1[i]))
        for i, hd in enumerate(heads):
            y_s[hd, rows, :] = ys[i] + yy[i][:, RC:2 * RC] + nkv[i]
            st_scr[hd] = sn[i] + zz[i][RW_HEAD:2 * RW_HEAD] + vk[i]

    def chunk_step(sub, carry):
        for g0 in range(0, nh, RC_GROUP):
            head_group(list(range(g0, min(g0 + RC_GROUP, nh))), sub)
        return carry

    lax.fori_loop(0, tc // RC, chunk_step, 0)

    for b in range(nb):
        for h in range(RW_H):
            sl = slice(RW_HEAD * h, RW_HEAD * (h + 1))
            y = y_s[b * RW_H + h]
            mean = jnp.mean(y, axis=-1, keepdims=True)
            var = jnp.mean(jnp.square(y - mean), axis=-1, keepdims=True)
            yn = (y - mean) * lax.rsqrt(var + RW_EPS) * lnw_ref[h:h + 1, :] + lnb_ref[h:h + 1, :]
            y_ref[b, :, sl] = ((yn + bonus_s[b, :, sl]) * g_s[b, :, sl]).astype(BF16)
            s1_ref[b, h] = st_scr[b * RW_H + h]
    sh1_ref[...] = carry_scr[...]


def _rwkv_chunk(prw, shift0, s0, P, nb, tc, passes):
    b, t, cols = prw.shape
    tok = np.arange(tc)
    same = (tok[:, None] // RC) == (tok[None, :] // RC)
    ltri = jnp.asarray(same & (tok[:, None] >= tok[None, :]), BF16)
    lones = jnp.asarray(same, BF16)
    col = np.arange(RW_W)
    hones = jnp.asarray((col[:, None] // RW_HEAD) == (col[None, :] // RW_HEAD), BF16)
    full = lambda a: pl.BlockSpec(a.shape, lambda i, j: (0,) * a.ndim)
    params = [P['mu'], P['w0'], P['w2'], P['a0'], P['a2'], P['g2'], P['kk'], P['ka'],
              P['rk'].reshape(1, RW_W), P['lnw'], P['lnb'], ltri, lones, hones]
    nh = nb * RW_H
    hs = lambda: pltpu.VMEM((nh, tc, RW_HEAD), F32)
    return pl.pallas_call(
        functools.partial(_rwkv_chunk_kernel, passes),
        grid=(b // nb, t // tc),
        in_specs=[pl.BlockSpec((nb, tc, cols), lambda i, j: (i, j, 0)),
                  pl.BlockSpec((nb, 1, cols), lambda i, j: (i, 0, 0)),
                  pl.BlockSpec((nb, RW_H, RW_HEAD, RW_HEAD), lambda i, j: (i, 0, 0, 0))]
                 + [full(a) for a in params],
        out_specs=[pl.BlockSpec((nb, tc, RW_W), lambda i, j: (i, j, 0)),
                   pl.BlockSpec((nb, RW_H, RW_HEAD, RW_HEAD), lambda i, j: (i, 0, 0, 0)),
                   pl.BlockSpec((nb, 1, cols), lambda i, j: (i, 0, 0))],
        out_shape=[jax.ShapeDtypeStruct((b, t, RW_W), BF16),
                   jax.ShapeDtypeStruct((b, RW_H, RW_HEAD, RW_HEAD), F32),
                   jax.ShapeDtypeStruct((b, 1, cols), F32)],
        scratch_shapes=[pltpu.VMEM((nh, RW_HEAD, RW_HEAD), F32),
                        pltpu.VMEM((nb, 1, cols), F32),
                        hs(), hs(), hs(), hs(), hs(), hs(), hs(), hs(), hs(),
                        pltpu.VMEM((nb, tc, RW_W), F32), pltpu.VMEM((nb, tc, RW_W), F32)],
        compiler_params=_cparams("arbitrary", "arbitrary"),
        name="rwkv_chunk",
    )(prw, shift0, s0, *params)


def _pattern_to_f32(u):
    key = u ^ INT_MIN
    return lax.bitcast_convert_type(jnp.where(key >= 0, key, key ^ 0x7FFFFFFF), F32)


def _topk_mask(sc_ref, madd_ref, p_scr, rows, nch, k_sel, lim, idx_bits):
    ntile = CW // LANES
    lane = lax.broadcasted_iota(I32, (rows, LANES), 1)

    def count(pred):
        def body(c, acc):
            off = pl.multiple_of(c * CW, CW)
            x = sc_ref[:, pl.ds(off, CW)]
            for t in range(ntile):
                acc = acc + pred(x[:, LANES * t:LANES * (t + 1)], off + LANES * t)
            return acc
        acc = lax.fori_loop(0, nch, body, jnp.zeros((rows, LANES), F32))
        return jnp.sum(acc, axis=1, keepdims=True)

    def bcast(col):
        return jnp.broadcast_to(col, (rows, LANES))

    def searching(state):
        it, alive, _, _ = state
        return (it < 32) & (alive > 0)

    def value_bit(state):
        it, _, tau, cnt_tau = state
        cand = tau | jnp.left_shift(jnp.int32(1), 31 - it)
        cb = bcast(_pattern_to_f32(cand))
        cnt = count(lambda x, off: jnp.where(x >= cb, 1.0, 0.0))
        take = cnt >= k_sel
        tau = jnp.where(take, cand, tau)
        cnt_tau = jnp.where(take, cnt, cnt_tau)
        settled = jnp.where(cnt_tau == k_sel, 1, jnp.where(lim < 0, 1, 0))
        return it + 1, 1 - jnp.min(settled), tau, cnt_tau

    _, _, tau, cnt_ge = lax.while_loop(
        searching, value_bit,
        (jnp.int32(0), jnp.int32(1), jnp.zeros((rows, 1), I32),
         jnp.zeros((rows, 1), F32) + jnp.asarray(nch * CW, F32)))
    thr = _pattern_to_f32(tau)
    thr_b = bcast(thr)
    cnt_gt = count(lambda x, off: jnp.where(x > thr_b, 1.0, 0.0))
    need = k_sel - cnt_gt
    tie = jnp.where(cnt_ge > k_sel, jnp.where(thr > NEG_INF, 1, 0), 0)

    p_scr[...] = jnp.full((rows, LANES), INT_MAX, I32)

    @pl.when(jnp.max(tie) > 0)
    def _():
        def index_bit(it, p):
            cand = p | jnp.left_shift(jnp.int32(1), idx_bits - 1 - it)
            cb = bcast(cand)
            cnt = count(lambda x, off: jnp.where(x == thr_b,
                                                 jnp.where(off + lane < cb, 1.0, 0.0), 0.0))
            return jnp.where(cnt < need, cand, p)
        p = lax.fori_loop(0, idx_bits, index_bit, jnp.zeros((rows, 1), I32))
        p_scr[...] = bcast(p)

    p_b = p_scr[...]
    lim_b = bcast(lim)

    def fin(c, carry):
        off = pl.multiple_of(c * CW, CW)
        x = sc_ref[:, pl.ds(off, CW)]
        for t in range(ntile):
            xt = x[:, LANES * t:LANES * (t + 1)]
            idx = off + LANES * t + lane
            sel = jnp.where(xt > thr_b, 0.0,
                            jnp.where(xt == thr_b, jnp.where(idx <= p_b, 0.0, NEG_INF), NEG_INF))
            madd_ref[:, pl.ds(pl.multiple_of(off + LANES * t, LANES), LANES)] = (
                jnp.where(idx <= lim_b, sel, NEG_INF))
        return carry

    lax.fori_loop(0, nch, fin, 0)


def _bias_table(rb_ref, h, d):
    bias = jnp.full(d.shape, rb_ref[0, h], F32)
    for j in range(1, NUM_BUCKETS):
        if BUCKET_START[j] is not None:
            bias = jnp.where(d >= BUCKET_START[j], rb_ref[j, h], bias)
    return bias


def _dsa_prompt_kernel(k_sel, idx_bits, rb_ref, q_ref, qi_ref, kiw_ref, kT_ref, vb_ref, kiT_ref,
                       o_ref, sc_scr, madd_scr, p_scr, btab_scr):
    i = pl.program_id(1)
    nch = ((i + 1) * QB + CW - 1) // CW

    @pl.when((pl.program_id(0) == 0) & (i == 0))
    def _():
        d = (lax.broadcasted_iota(I32, (QB, 2 * QB), 0)
             - lax.broadcasted_iota(I32, (QB, 2 * QB), 1) + QB)
        for h in range(DSA_H):
            btab_scr[h] = _bias_table(rb_ref, h, d) - rb_ref[NUM_BUCKETS - 1, h]

    tq = i * QB + lax.broadcasted_iota(I32, (QB, 1), 0)
    qi = qi_ref[0]
    wi = kiw_ref[0][:, IDX_D:IDX_D + IDX_H]
    qi_all = jnp.concatenate([qi[:, IDX_D * h:IDX_D * (h + 1)] for h in range(IDX_H)], axis=0)
    wi_h = [wi[:, h:h + 1] for h in range(IDX_H)]

    def score_chunk(c, carry):
        off = pl.multiple_of(c * CW, CW)
        s = _dot(qi_all, kiT_ref[0, :, pl.ds(off, CW)])
        acc = jnp.zeros((QB, CW), F32)
        for h in range(IDX_H):
            acc = acc + jnp.maximum(s[QB * h:QB * (h + 1)], 0.0) * wi_h[h]
        spos = off + lax.broadcasted_iota(I32, (1, CW), 1)
        sc_scr[:, pl.ds(off, CW)] = jnp.where(spos <= tq, acc * IDX_SCALE + 0.0, NEG_INF)
        return carry

    lax.fori_loop(0, nch, score_chunk, 0)

    _topk_mask(sc_scr, madd_scr, p_scr, QB, nch, k_sel, tq, idx_bits)

    q = q_ref[0] * (DSA_HEAD ** -0.5)
    tiles = CW // QB
    hsl = [slice(DSA_HEAD * h, DSA_HEAD * (h + 1)) for h in range(DSA_H)]
    qh = [q[:, sl] for sl in hsl]

    def attend(heads, near, c, carry):
        n = len(heads)
        off = pl.multiple_of(c * CW, CW)
        madd = madd_scr[:, pl.ds(off, CW)]
        s = [_dot(qh[h], kT_ref[0, hsl[h], pl.ds(off, CW)]) + madd for h in heads]
        if near:
            for j, h in enumerate(heads):
                parts = []
                for t in range(tiles):
                    delta = i - (c * tiles + t)
                    parts.append(jnp.where(delta == 0, btab_scr[h, :, QB:2 * QB],
                                           jnp.where(delta == 1, btab_scr[h, :, 0:QB], 0.0)))
                s[j] = s[j] + jnp.concatenate(parts, axis=1)
        m_new = [jnp.maximum(carry[j][0], jnp.max(s[j], axis=1, keepdims=True)) for j in range(n)]
        p = [jnp.exp(s[j] - m_new[j]) for j in range(n)]
        pv = [_dot(p[j].astype(BF16), vb_ref[0, pl.ds(off, CW), hsl[h]])
              for j, h in enumerate(heads)]
        out = []
        for j in range(n):
            m, l, acc = carry[j]
            alpha = jnp.exp(m - m_new[j])
            out.append((m_new[j], l * alpha + jnp.sum(p[j], axis=1, keepdims=True),
                        acc * alpha + pv[j]))
        return tuple(out)

    n_far = jnp.maximum(i - 1, 0) // tiles
    for g0 in range(0, DSA_H, ATTN_GROUP):
        heads = list(range(g0, g0 + ATTN_GROUP))
        init = tuple((jnp.full((QB, 1), NEG_INF, F32), jnp.zeros((QB, 1), F32),
                      jnp.zeros((QB, DSA_HEAD), F32)) for _ in heads)
        carry = lax.fori_loop(0, n_far, functools.partial(attend, heads, False), init)
        carry = lax.fori_loop(n_far, nch, functools.partial(attend, heads, True), carry)
        for j, h in enumerate(heads):
            _, l, acc = carry[j]
            o_ref[0, :, hsl[h]] = (acc / l).astype(BF16)


def _dsa_prompt(rel_bias, q, qi, kiw, kT, vb, kiT):
    b, t, _ = q.shape
    k_sel = min(TOPK_MAX, t // 4)
    idx_bits = max(1, int(math.ceil(math.log2(t))))
    blk = lambda w: pl.BlockSpec((1, QB, w), lambda bi, i: (bi, i, 0))
    whole = lambda shape: pl.BlockSpec(shape, lambda bi, i: (bi, 0, 0))
    return pl.pallas_call(
        functools.partial(_dsa_prompt_kernel, k_sel, idx_bits),
        grid=(b, t // QB),
        in_specs=[pl.BlockSpec(memory_space=pltpu.SMEM),
                  blk(DSA_W), blk(IDX_H * IDX_D), blk(LANES),
                  whole((1, DSA_W, t)), whole((1, t, DSA_W)), whole((1, IDX_D, t))],
        out_specs=blk(DSA_W),
        out_shape=jax.ShapeDtypeStruct((b, t, DSA_W), BF16),
        scratch_shapes=[pltpu.VMEM((QB, t), F32), pltpu.VMEM((QB, t), F32),
                        pltpu.VMEM((QB, LANES), I32),
                        pltpu.VMEM((DSA_H, QB, 2 * QB), F32)],
        compiler_params=_cparams("arbitrary", "arbitrary"),
        name="dsa_prompt",
    )(rel_bias, q, qi, kiw, kT, vb, kiT)


SROWS = 8
SCHUNK = PAGES_PER_STEP * PAGE_SIZE


SCORE_PAGES = 2 * PAGES_PER_STEP
SCORE_CHUNK = SCORE_PAGES * PAGE_SIZE


def _dsa_s_scores_kernel(n_tok, n_past_chunks, pt_ref, qi_ref, wcol_ref, kinew_ref, *rest):
    pages, out_ref = rest[:SCORE_PAGES], rest[SCORE_PAGES]
    c = pl.program_id(1)
    qi = qi_ref[0]
    wcol = wcol_ref[0]

    def head_sum(s):
        s = jnp.maximum(s, 0.0) * wcol
        return jnp.sum(s.reshape(n_tok, IDX_H, s.shape[-1]), axis=1) * IDX_SCALE + 0.0

    @pl.when(c < n_past_chunks)
    def _():
        kit = jnp.concatenate([pg[0] for pg in pages], axis=1).astype(BF16)
        out_ref[0, 0:n_tok, :] = head_sum(_dot(qi, kit))
        out_ref[0, n_tok:SROWS, :] = jnp.full((SROWS - n_tok, SCORE_CHUNK), NEG_INF, F32)

    @pl.when(c == n_past_chunks)
    def _():
        s = head_sum(_dot_nt(qi, kinew_ref[0]))
        tr = lax.broadcasted_iota(I32, (n_tok, SROWS), 0)
        tc = lax.broadcasted_iota(I32, (n_tok, SROWS), 1)
        out_ref[...] = jnp.full((1, SROWS, SCORE_CHUNK), NEG_INF, F32)
        out_ref[0, 0:n_tok, 0:SROWS] = jnp.where(tc <= tr, s, NEG_INF)


def _dsa_s_scores(page_table, cache_ki, qi_rows, wcol, ki_new, n_tok):
    b, n_pages = page_table.shape
    npc = n_pages // SCORE_PAGES
    page_specs = [
        pl.BlockSpec((1, IDX_D, PAGE_SIZE),
                     lambda bi, c, pt, j=j: (pt[bi, jnp.minimum(c, npc - 1) * SCORE_PAGES + j], 0, 0))
        for j in range(SCORE_PAGES)]
    rows = qi_rows.shape[1]
    grid_spec = pltpu.PrefetchScalarGridSpec(
        num_scalar_prefetch=1,
        grid=(b, npc + 1),
        in_specs=[pl.BlockSpec((1, rows, IDX_D), lambda bi, c, pt: (bi, 0, 0)),
                  pl.BlockSpec((1, rows, 1), lambda bi, c, pt: (bi, 0, 0)),
                  pl.BlockSpec((1, SROWS, IDX_D), lambda bi, c, pt: (bi, 0, 0))] + page_specs,
        out_specs=pl.BlockSpec((1, SROWS, SCORE_CHUNK), lambda bi, c, pt: (bi, 0, c)),
    )
    return pl.pallas_call(
        functools.partial(_dsa_s_scores_kernel, n_tok, npc),
        grid_spec=grid_spec,
        out_shape=jax.ShapeDtypeStruct((b, SROWS, (npc + 1) * SCORE_CHUNK), F32),
        compiler_params=_cparams("arbitrary", "arbitrary"),
        name="dsa_s_scores",
    )(page_table, qi_rows, wcol, ki_new, *([cache_ki] * SCORE_PAGES))


def _topk_rows_kernel(k_sel, idx_bits, sc_ref, lim_ref, madd_ref, p_scr):
    rows, width = sc_ref.shape
    _topk_mask(sc_ref, madd_ref, p_scr, rows, width // CW, k_sel, lim_ref[...], idx_bits)


def _topk_rows(scores, lim, k_sel, rows_per_step):
    n, width = scores.shape
    idx_bits = max(1, int(math.ceil(math.log2(width))))
    return pl.pallas_call(
        functools.partial(_topk_rows_kernel, k_sel, idx_bits),
        grid=(n // rows_per_step,),
        in_specs=[pl.BlockSpec((rows_per_step, width), lambda i: (i, 0)),
                  pl.BlockSpec((rows_per_step, 1), lambda i: (i, 0))],
        out_specs=pl.BlockSpec((rows_per_step, width), lambda i: (i, 0)),
        out_shape=jax.ShapeDtypeStruct((n, width), F32),
        scratch_shapes=[pltpu.VMEM((rows_per_step, LANES), I32)],
        compiler_params=_cparams("arbitrary"),
        name="topk_rows",
    )(scores, lim)


def _dsa_s_attn_kernel(n_past_chunks, past, pt_ref, rb_ref, q_ref, madd_ref, knew_ref, vnew_ref,
                       *rest):
    kp = rest[:PAGES_PER_STEP]
    vp = rest[PAGES_PER_STEP:2 * PAGES_PER_STEP]
    o_ref, m_scr, l_scr, acc_scr = rest[2 * PAGES_PER_STEP:]
    c = pl.program_id(1)
    rows = DSA_H * SROWS

    @pl.when(c == 0)
    def _():
        m_scr[...] = jnp.full(m_scr.shape, NEG_INF, F32)
        l_scr[...] = jnp.zeros(l_scr.shape, F32)
        acc_scr[...] = jnp.zeros(acc_scr.shape, F32)

    q = q_ref[0] * (DSA_HEAD ** -0.5)
    head_of_row = lax.broadcasted_iota(I32, (rows, DSA_W), 0) // SROWS
    head_of_col = lax.broadcasted_iota(I32, (rows, DSA_W), 1) // DSA_HEAD
    qblk = jnp.where(head_of_row == head_of_col, jnp.concatenate([q] * DSA_H, axis=0),
                     jnp.zeros((), BF16))

    def near_bias(width, key0):
        tok = lax.broadcasted_iota(I32, (SROWS, width), 0)
        d = past + tok - (key0 + lax.broadcasted_iota(I32, (SROWS, width), 1))
        return jnp.concatenate(
            [_bias_table(rb_ref, h, d) - rb_ref[NUM_BUCKETS - 1, h] for h in range(DSA_H)], axis=0)

    def update(s, pv_fn):
        m = m_scr[...]
        m_new = jnp.maximum(m, jnp.max(s, axis=1, keepdims=True))
        alpha = jnp.exp(m - m_new)
        p = jnp.exp(s - m_new)
        l_scr[...] = l_scr[...] * alpha + jnp.sum(p, axis=1, keepdims=True)
        acc_scr[...] = acc_scr[...] * alpha + pv_fn(p.astype(BF16))
        m_scr[...] = m_new

    def chunk_t(pages):
        return jnp.concatenate([pg[0].reshape(DSA_W, PAGE_SIZE) for pg in pages],
                               axis=1).astype(BF16)

    @pl.when(c < n_past_chunks)
    def _():
        s = _dot(qblk, chunk_t(kp)) + jnp.concatenate([madd_ref[0]] * DSA_H, axis=0)
        is_last = c == n_past_chunks - 1
        tail = SCHUNK - PAGE_SIZE
        nb_ = jnp.where(is_last, near_bias(PAGE_SIZE, past - PAGE_SIZE), 0.0)
        s = jnp.concatenate([s[:, 0:tail], s[:, tail:SCHUNK] + nb_], axis=1)
        vt = chunk_t(vp)
        update(s, lambda pb: _dot_nt(pb, vt))

    @pl.when(c == n_past_chunks)
    def _():
        s = (_dot_nt(qblk, knew_ref[0]) + jnp.concatenate([madd_ref[0, :, 0:SROWS]] * DSA_H, axis=0)
             + near_bias(SROWS, past))
        update(s, lambda pb: _dot(pb, vnew_ref[0]))
        for h in range(DSA_H):
            r = slice(SROWS * h, SROWS * (h + 1))
            cs = slice(DSA_HEAD * h, DSA_HEAD * (h + 1))
            o_ref[0, :, cs] = (acc_scr[r, cs] / l_scr[r, :]).astype(o_ref.dtype)


def _dsa_s_attn(page_table, rel_bias, cache_k, cache_v, q, madd, k_new, v_new):
    b, n_pages = page_table.shape
    npc = n_pages // PAGES_PER_STEP
    rows = DSA_H * SROWS
    assert BUCKET_START[NUM_BUCKETS - 1] <= PAGE_SIZE

    def page_spec(j):
        return pl.BlockSpec(
            (1, DSA_H, DSA_HEAD, PAGE_SIZE),
            lambda bi, c, pt, j=j: (pt[bi, jnp.minimum(c, npc - 1) * PAGES_PER_STEP + j], 0, 0, 0))

    page_specs = [page_spec(j) for j in range(PAGES_PER_STEP)]
    grid_spec = pltpu.PrefetchScalarGridSpec(
        num_scalar_prefetch=1,
        grid=(b, npc + 1),
        in_specs=[pl.BlockSpec(memory_space=pltpu.SMEM),
                  pl.BlockSpec((1, SROWS, DSA_W), lambda bi, c, pt: (bi, 0, 0)),
                  pl.BlockSpec((1, SROWS, SCHUNK), lambda bi, c, pt: (bi, 0, c)),
                  pl.BlockSpec((1, SROWS, DSA_W), lambda bi, c, pt: (bi, 0, 0)),
                  pl.BlockSpec((1, SROWS, DSA_W), lambda bi, c, pt: (bi, 0, 0))]
                 + page_specs + page_specs,
        out_specs=pl.BlockSpec((1, SROWS, DSA_W), lambda bi, c, pt: (bi, 0, 0)),
        scratch_shapes=[pltpu.VMEM((rows, 1), F32), pltpu.VMEM((rows, 1), F32),
                        pltpu.VMEM((rows, DSA_W), F32)],
    )
    return pl.pallas_call(
        functools.partial(_dsa_s_attn_kernel, npc, n_pages * PAGE_SIZE),
        grid_spec=grid_spec,
        out_shape=jax.ShapeDtypeStruct((b, SROWS, DSA_W), BF16),
        compiler_params=_cparams("arbitrary", "arbitrary"),
        name="dsa_s_attn",
    )(page_table, rel_bias, q, madd, k_new, v_new,
      *([cache_k] * PAGES_PER_STEP), *([cache_v] * PAGES_PER_STEP))


def _mem_kv_kernel(x_ref, g_ref, w_ref, k_ref, v_ref):
    h = _rms(x_ref[...], g_ref[...]).astype(BF16)
    width = k_ref.shape[-1]
    k_ref[...] = _dot(h, w_ref[:, 0:width])
    v_ref[...] = _dot(h, w_ref[:, width:2 * width])


def _mem_kv(mem, g, wkv, tm):
    n, d = mem.shape
    width = wkv.shape[1] // 2
    return pl.pallas_call(
        _mem_kv_kernel,
        grid=(n // tm,),
        in_specs=[pl.BlockSpec((tm, d), lambda i: (i, 0)),
                  pl.BlockSpec((1, d), lambda i: (0, 0)),
                  pl.BlockSpec(wkv.shape, lambda i: (0, 0))],
        out_specs=[pl.BlockSpec((tm, width), lambda i: (i, 0))] * 2,
        out_shape=[jax.ShapeDtypeStruct((n, width), F32)] * 2,
        compiler_params=_cparams("arbitrary"),
        name="mem_kv",
    )(mem, g, wkv)


def _split_bf16(x):
    hi = x.astype(BF16)
    return hi, (x - hi.astype(F32)).astype(BF16)


def _mix_kernel(x_ref, yrw_ref, ydsa_ref, mk_ref, mv_ref, woa_ref, wob_ref, gx_ref, wq_ref, wo_ref,
                gf_ref, rwh_ref, rwl_ref, rb_ref, x2_ref, h3_ref, comb_ref):
    x1 = x_ref[0] + _dot(yrw_ref[0], woa_ref[...]) + _dot(ydsa_ref[0], wob_ref[...])
    h2 = _rms(x1, gx_ref[...]).astype(BF16)
    qx = _dot(h2, wq_ref[...])
    mk = mk_ref[0].astype(BF16)
    mv = mv_ref[0].astype(BF16)
    sls = [slice(XA_HEAD * h, XA_HEAD * (h + 1)) for h in range(XA_H)]
    lg = [_dot_nt(qx[:, sl].astype(BF16), mk[:, sl]) * (XA_HEAD ** -0.5) for sl in sls]
    p = [jnp.exp(x - jnp.max(x, axis=1, keepdims=True)) for x in lg]
    pv = [_dot(ph.astype(BF16), mv[:, sl]) for ph, sl in zip(p, sls)]
    heads = [y / jnp.sum(ph, axis=1, keepdims=True) for y, ph in zip(pv, p)]
    o = jnp.concatenate(heads, axis=1).astype(BF16)
    x2 = x1 + _dot(o, wo_ref[...])
    x2_ref[0] = x2
    h3 = _rms(x2, gf_ref[...])
    h3_ref[0] = h3.astype(BF16)

    hh, hl = _split_bf16(h3)
    logits = (_dot(hh, rwh_ref[...]) + _dot(hh, rwl_ref[...]) + _dot(hl, rwh_ref[...])
              + rb_ref[...])
    lane = lax.broadcasted_iota(I32, logits.shape, 1)
    work = logits
    vals, idxs = [], []
    for _ in range(TOP_K):
        mx = jnp.max(work, axis=1, keepdims=True)
        ix = jnp.min(jnp.where(work == mx, lane, N_EXPERTS), axis=1, keepdims=True)
        vals.append(mx)
        idxs.append(ix)
        work = jnp.where(lane == ix, -jnp.inf, work)
    es = [jnp.exp(v - vals[0]) for v in vals]
    den = es[0] + es[1] + es[2] + es[3]
    comb = jnp.zeros(logits.shape, F32)
    for e, ix in zip(es, idxs):
        comb = comb + jnp.where(lane == ix, e / den, 0.0)
    comb_ref[0] = comb


def _mix(x, yrw, ydsa, mk, mv, W, tm):
    b, t, d = x.shape
    mem_len, xa_w = mk.shape[1], mk.shape[2]
    full = lambda a: pl.BlockSpec(a.shape, lambda i, j: (0,) * a.ndim)
    rspec = lambda w: pl.BlockSpec((1, tm, w), lambda i, j: (i, j, 0))
    mspec = pl.BlockSpec((1, mem_len, xa_w), lambda i, j: (i, 0, 0))
    params = [W['woa'], W['wob'], W['gx'], W['wq'], W['wo'], W['gf'], W['rwh'], W['rwl'], W['rb']]
    return pl.pallas_call(
        _mix_kernel,
        grid=(b, t // tm),
        in_specs=[rspec(d), rspec(RW_W), rspec(DSA_W), mspec, mspec] + [full(a) for a in params],
        out_specs=[rspec(d), rspec(d), rspec(N_EXPERTS)],
        out_shape=[jax.ShapeDtypeStruct((b, t, d), F32), jax.ShapeDtypeStruct((b, t, d), BF16),
                   jax.ShapeDtypeStruct((b, t, N_EXPERTS), F32)],
        compiler_params=_cparams("arbitrary", "arbitrary"),
        name="mix",
    )(x, yrw, ydsa, mk, mv, *params)


MOE_EXPERTS_PER_STEP = 2


def _moe_kernel(h_ref, comb_ref, x_ref, w1_ref, b1_ref, w2_ref, b2_ref, gfin_ref, o_ref, acc_scr):
    e = pl.program_id(1)
    d_ff = w2_ref.shape[1]

    @pl.when(e == 0)
    def _():
        acc_scr[...] = jnp.zeros(acc_scr.shape, F32)

    h = h_ref[...]
    comb = comb_ref[...]
    lane = lax.broadcasted_iota(I32, comb.shape, 1)
    group = range(w1_ref.shape[0])
    gate = [jnp.minimum(_dot(h, w1_ref[g, :, 0:d_ff]) + b1_ref[g, :, 0:d_ff], SWIGLU_LIMIT)
            for g in group]
    up = [jnp.clip(_dot(h, w1_ref[g, :, d_ff:2 * d_ff]) + b1_ref[g, :, d_ff:2 * d_ff],
                   -SWIGLU_LIMIT, SWIGLU_LIMIT) for g in group]
    act = [((up[g] + 1.0) * (gate[g] * _sigmoid(SWIGLU_ALPHA * gate[g]))).astype(BF16)
           for g in group]
    out = [_dot(act[g], w2_ref[g]) + b2_ref[g] for g in group]
    total = None
    for g in group:
        wcol = jnp.sum(jnp.where(lane == e * len(group) + g, comb, 0.0), axis=1, keepdims=True)
        total = wcol * out[g] if total is None else total + wcol * out[g]
    acc_scr[...] += total

    @pl.when(e == pl.num_programs(1) - 1)
    def _():
        o_ref[...] = _rms(x_ref[...] + acc_scr[...], gfin_ref[...])


def _moe(h3, comb, x2, w1, b1, w2, b2, gfin, tm):
    n, d = h3.shape
    n_exp, _, two_ff = w1.shape
    d_ff = two_ff // 2
    eg = MOE_EXPERTS_PER_STEP
    return pl.pallas_call(
        _moe_kernel,
        grid=(n // tm, n_exp // eg),
        in_specs=[pl.BlockSpec((tm, d), lambda i, e: (i, 0)),
                  pl.BlockSpec((tm, n_exp), lambda i, e: (i, 0)),
                  pl.BlockSpec((tm, d), lambda i, e: (i, 0)),
                  pl.BlockSpec((eg, d, two_ff), lambda i, e: (e, 0, 0)),
                  pl.BlockSpec((eg, 1, two_ff), lambda i, e: (e, 0, 0)),
                  pl.BlockSpec((eg, d_ff, d), lambda i, e: (e, 0, 0)),
                  pl.BlockSpec((eg, 1, d), lambda i, e: (e, 0, 0)),
                  pl.BlockSpec((1, d), lambda i, e: (0, 0))],
        out_specs=pl.BlockSpec((tm, d), lambda i, e: (i, 0)),
        out_shape=jax.ShapeDtypeStruct((n, d), F32),
        scratch_shapes=[pltpu.VMEM((tm, d), F32)],
        compiler_params=_cparams("arbitrary", "arbitrary"),
        name="moe",
    )(h3, comb, x2, w1, b1, w2, b2, gfin)


def _pick_tile(n, pref):
    t = min(pref, n)
    while n % t:
        t //= 2
    return t


def kernel(x_prompt, mem_prompt, x_sample, cache_k, cache_v, cache_idx_k, cache_mem_k, cache_mem_v, state_rwkv, state_shift, page_table, rel_bias, norm_final, norm_mix, w_in, mu_shift, rw_w0, rw_w2, rw_a0, rw_a2, rw_g2, rw_k_k, rw_k_a, rw_r_k, rw_ln_w, rw_ln_b, w_out, norm_xattn, norm_mem, xa_wq, xa_wk, xa_wv, xa_wo, norm_ffn, router_w, router_b, moe_w1, moe_b1, moe_w2, moe_b2):
    depth = w_in.shape[0]
    assert depth == 1, "single-layer step"
    l = 0
    bp, tp, d = x_prompt.shape
    bs, ts, _ = x_sample.shape
    n_pool = cache_k.shape[1]
    n_pages = page_table.shape[1]
    past = n_pages * PAGE_SIZE
    mem_len = mem_prompt.shape[1]
    xa_w = XA_H * XA_HEAD
    rw_cols = state_shift.shape[-1]
    assert tp % CW == 0 and n_pages % SCORE_PAGES == 0 and ts <= SROWS

    row = lambda a: a.reshape(1, -1)
    bf = lambda a: a.astype(BF16)

    w_pad = bf(jnp.pad(w_in[l], ((0, 0), (0, C_END - w_in.shape[-1]))))
    lora = rw_w2.shape[1]
    P = {
        'mu': row(mu_shift[l]), 'w0': row(rw_w0[l]), 'a0': row(rw_a0[l]),
        'w2': bf(jnp.pad(rw_w2[l], ((0, LANES - lora), (0, 0)))),
        'a2': bf(jnp.pad(rw_a2[l], ((lora, LANES - lora - rw_a2.shape[1]), (0, 0)))),
        'g2': bf(rw_g2[l]), 'kk': row(rw_k_k[l]), 'ka': row(rw_k_a[l]),
        'rk': rw_r_k[l].reshape(RW_H, RW_HEAD),
        'lnw': rw_ln_w[l].reshape(RW_H, RW_HEAD), 'lnb': rw_ln_b[l].reshape(RW_H, RW_HEAD),
    }
    rwh = bf(router_w[l])
    W = {
        'woa': bf(w_out[l][:RW_W]), 'wob': bf(w_out[l][RW_W:]), 'gx': row(norm_xattn[l]),
        'wq': bf(xa_wq[l]), 'wo': bf(xa_wo[l]), 'gf': row(norm_ffn[l]),
        'rwh': rwh, 'rwl': bf(router_w[l] - rwh.astype(F32)), 'rb': row(router_b[l]),
    }
    w1 = bf(moe_w1[l])
    w2 = bf(moe_w2[l])
    b1 = moe_b1[l][:, None, :]
    b2 = moe_b2[l][:, None, :]
    gfin = row(norm_final)
    gmix = row(norm_mix[l])

    prw, q, k, v, qi, kiw, kT, vb, kiT = _in_proj(x_prompt, gmix, w_pad, _pick_tile(tp, 256))
    y_rw, p_s1, p_sh1 = _rwkv_chunk(prw, jnp.zeros((bp, 1, rw_cols), F32),
                                    jnp.zeros((bp, RW_H, RW_HEAD, RW_HEAD), F32), P,
                                    nb=bp, tc=_pick_tile(tp, 256), passes=1)
    y_dsa = _dsa_prompt(rel_bias, q, qi, kiw, kT, vb, kiT)
    wkv = bf(jnp.concatenate([xa_wk[l], xa_wv[l]], axis=1))
    mk_p, mv_p = _mem_kv(mem_prompt.reshape(bp * mem_len, d), row(norm_mem[l]), wkv,
                         _pick_tile(bp * mem_len, 256))
    mk_p = mk_p.reshape(bp, mem_len, xa_w)
    mv_p = mv_p.reshape(bp, mem_len, xa_w)
    x2, h3, comb = _mix(x_prompt, y_rw, y_dsa, mk_p, mv_p, W, _pick_tile(tp, 256))
    n_p = bp * tp
    y_prompt = _moe(h3.reshape(n_p, d), comb.reshape(n_p, N_EXPERTS), x2.reshape(n_p, d),
                    w1, b1, w2, b2, gfin, _pick_tile(n_p, 512)).reshape(bp, tp, d)

    xs = jnp.pad(x_sample, ((0, 0), (0, SROWS - ts), (0, 0)))
    n_s = bs * SROWS
    sprw, sq, sk, sv, sqi, skiw, _, svb, _ = _in_proj(xs.reshape(1, n_s, d), gmix, w_pad,
                                                      _pick_tile(n_s, 256))
    seq = lambda a: a.reshape(bs, SROWS, a.shape[-1])
    sprw, sq, sk, sv, sqi, skiw, svb = map(seq, (sprw, sq, sk, sv, sqi, skiw, svb))
    sy_rw, s_s1, s_sh1 = _rwkv(sprw, state_shift[l][:, None, :], state_rwkv[l], P,
                               nb=_pick_tile(bs, 4), tc=SROWS, n_valid=ts)

    qi_rows = sqi[:, :ts].reshape(bs, ts * IDX_H, IDX_D)
    wcol = skiw[:, :ts, IDX_D:IDX_D + IDX_H].reshape(bs, ts * IDX_H, 1)
    ki_new = bf(skiw[:, :, :IDX_D])
    scores = _dsa_s_scores(page_table, jnp.transpose(cache_idx_k[l], (0, 2, 1)), qi_rows, wcol,
                           ki_new, ts)
    width = scores.shape[-1]
    tok = jnp.arange(SROWS, dtype=I32)
    lim = jnp.tile(jnp.where(tok < ts, past + tok, -1), bs).reshape(n_s, 1)
    k_sel = min(TOPK_MAX, (past + ts) // 4)
    madd = _topk_rows(scores.reshape(n_s, width), lim, k_sel, _pick_tile(n_s, 64))
    madd = madd.reshape(bs, SROWS, width)
    sy_dsa = _dsa_s_attn(page_table, rel_bias,
                         jnp.transpose(cache_k[l], (0, 2, 3, 1)),
                         jnp.transpose(cache_v[l], (0, 2, 3, 1)),
                         sq, madd, bf(sk), svb)
    sx2, sh3, scomb = _mix(xs, sy_rw, sy_dsa, cache_mem_k[l].reshape(bs, mem_len, xa_w),
                           cache_mem_v[l].reshape(bs, mem_len, xa_w), W, SROWS)
    y_s = _moe(sh3.reshape(n_s, d), scomb.reshape(n_s, N_EXPERTS), sx2.reshape(n_s, d),
               w1, b1, w2, b2, gfin, _pick_tile(n_s, 256)).reshape(bs, SROWS, d)

    heads = lambda a, b_, t_: a.reshape(1, b_, t_, DSA_H, DSA_HEAD)
    return (y_prompt, y_s[:, :ts],
            p_s1[None], p_sh1.reshape(1, bp, rw_cols),
            heads(k, bp, tp), heads(v, bp, tp), kiw[:, :, :IDX_D][None],
            mk_p.reshape(1, bp, mem_len, XA_H, XA_HEAD), mv_p.reshape(1, bp, mem_len, XA_H, XA_HEAD),
            s_s1[None], s_sh1.reshape(1, bs, rw_cols),
            heads(sk[:, :ts], bs, ts), heads(sv[:, :ts], bs, ts), skiw[:, :ts, :IDX_D][None])
```
